```python
import math
import jax
import jax.numpy as jnp
from jax import lax
import numpy as np

D_MODEL = 2048
BATCH = 4
SEQ = 2048
DEPTH = 2
DEC_BATCH = 32
DEC_SEQ = 1
PAST_LEN = 16384
PAGE_SIZE = 128

N_EVEN = (DEPTH + 1) // 2
N_ODD = DEPTH // 2
D_FF = 4 * D_MODEL
EPS = 1e-6
GN_EPS = 1e-5
BLOCK = 128

H_A = 8
DK_A = 64
DV_A = 128
RET_THETA = 10000.0
H_B = 16
KV_B = 2
HD_B = 64
WINDOW = 128
ROPE_THETA_B = 150000.0
H_C = 8
HD_C = 128
H_D = 16
HD_D = 64
G_D = 2
N_D = 128
CONV_W = 4
DI_D = H_D * HD_D
CONV_CH = DI_D + 2 * G_D * N_D

EVEN_SIZES = (H_A * DK_A, H_A * DK_A, H_A * DV_A, H_A * DV_A, H_B * HD_B, KV_B * HD_B, KV_B * HD_B)
ODD_SIZES = (H_C * HD_C, H_C * HD_C, H_C * HD_C, H_C, DI_D, CONV_CH, H_D)
EVEN_IN = sum(EVEN_SIZES)
ODD_IN = sum(ODD_SIZES)
EVEN_SPLITS = tuple(int(s) for s in np.cumsum(EVEN_SIZES)[:-1])
ODD_SPLITS = tuple(int(s) for s in np.cumsum(ODD_SIZES)[:-1])
MIX_EVEN = H_A * DV_A + H_B * HD_B
MIX_ODD = H_C * HD_C + DI_D

kernel_name = 'hybrid_ret_swa_fox_ssd_step'

F32 = jnp.float32


def rms_norm(x, g):
    xf = x.astype(F32)
    y = xf * lax.rsqrt(jnp.mean(xf * xf, axis=-1, keepdims=True) + EPS)
    return (y * g.astype(F32)).astype(x.dtype)


def rope(x, pos, theta):
    d = x.shape[-1]
    inv = 1.0 / (theta ** (jnp.arange(d // 2, dtype=F32) * (2.0 / d)))
    ang = pos.astype(F32)[:, None] * inv[None, :]
    cos = jnp.cos(ang)[None, :, None, :]
    sin = jnp.sin(ang)[None, :, None, :]
    x1, x2 = jnp.split(x.astype(F32), 2, axis=-1)
    return jnp.concatenate([x1 * cos - x2 * sin, x1 * sin + x2 * cos], axis=-1).astype(x.dtype)


def decay_chunk(q, k, v, log_a, s0):
    q, k, v = q.astype(F32), k.astype(F32), v.astype(F32)
    L = q.shape[1]
    cum = jnp.cumsum(log_a.astype(F32), axis=1)
    causal = jnp.tril(jnp.ones((L, L), bool))[None, :, :, None]
    diff = cum[:, :, None, :] - cum[:, None, :, :]
    decay = jnp.exp(jnp.where(causal, diff, -jnp.inf))
    att = jnp.einsum('bthd,bshd->btsh', q, k) * decay
    y = jnp.einsum('btsh,bshv->bthv', att, v) + jnp.einsum('bthd,bhdv->bthv', q, s0) * jnp.exp(cum)[..., None]
    tail = jnp.exp(cum[:, -1:, :] - cum)
    s_new = jnp.exp(cum[:, -1, :])[..., None, None] * s0 + jnp.einsum('bshd,bsh,bshv->bhdv', k, tail, v)
    return y, s_new


def chunked_decay_scan(q, k, v, log_a, s0):
    b, L = q.shape[:2]
    c = min(BLOCK, L)
    n = L // c

    def to_chunks(t):
        return jnp.moveaxis(t.reshape((b, n, c) + t.shape[2:]), 1, 0)

    def step(s, xs):
        y, s = decay_chunk(*xs, s)
        return s, y

    s_fin, ys = lax.scan(step, s0, (to_chunks(q), to_chunks(k), to_chunks(v), to_chunks(log_a)))
    y = jnp.moveaxis(ys, 0, 1).reshape((b, L) + ys.shape[3:])
    return y, s_fin


def head_group_norm(o, g):
    mu = jnp.mean(o, axis=-1, keepdims=True)
    var = jnp.mean(jnp.square(o - mu), axis=-1, keepdims=True)
    return (o - mu) * lax.rsqrt(var + GN_EPS) * g.astype(F32)


def sink_softmax_attend(q, k, v, mask, sinks):
    kvh, grp, d = q.shape[3], q.shape[4], q.shape[5]
    s = jnp.einsum('bnqkgd,bnskd->bnkgqs', q.astype(F32), k.astype(F32)) * (d ** -0.5)
    s = jnp.where(mask[None, :, None, None], s, -jnp.inf)
    sink = jnp.broadcast_to(sinks.astype(F32).reshape(kvh, grp)[None, None, :, :, None, None], s.shape[:-1] + (1,))
    p = jax.nn.softmax(jnp.concatenate([s, sink], axis=-1), axis=-1)[..., :-1]
    return jnp.einsum('bnkgqs,bnskd->bnqkgd', p, v.astype(F32))


def swa_prompt(q, k, v, sinks):
    b, L = q.shape[:2]
    n = L // WINDOW
    qb = q.reshape(b, n, WINDOW, KV_B, H_B // KV_B, HD_B)
    kb = k.reshape(b, n, WINDOW, KV_B, HD_B)
    vb = v.reshape(b, n, WINDOW, KV_B, HD_B)
    pad = ((0, 0), (1, 0), (0, 0), (0, 0), (0, 0))
    kwin = jnp.concatenate([jnp.pad(kb[:, :-1], pad), kb], axis=2)
    vwin = jnp.concatenate([jnp.pad(vb[:, :-1], pad), vb], axis=2)
    i = jnp.arange(WINDOW)[:, None]
    j = jnp.arange(2 * WINDOW)[None, :]
    blk = jnp.arange(n)[:, None, None]
    mask = (j >= i + 1) & (j <= i + WINDOW) & (blk * WINDOW - WINDOW + j >= 0)
    o = sink_softmax_attend(qb, kwin, vwin, mask, sinks)
    return o.reshape(b, L, H_B * HD_B)


def swa_sample(q, k, v, kbuf, vbuf, sinks):
    b, L = q.shape[:2]
    kcat = jnp.concatenate([kbuf.astype(k.dtype), k], axis=1)
    vcat = jnp.concatenate([vbuf.astype(v.dtype), v], axis=1)
    diff = (jnp.arange(L)[:, None] + WINDOW) - jnp.arange(WINDOW + L)[None, :]
    mask = ((diff >= 0) & (diff < WINDOW))[None]
    o = sink_softmax_attend(q.reshape(b, 1, L, KV_B, H_B // KV_B, HD_B), kcat[:, None], vcat[:, None], mask, sinks)
    return o.reshape(b, L, H_B * HD_B), kcat[:, -WINDOW:], vcat[:, -WINDOW:]


def fox_prompt(q, k, v, logf):
    b, L = q.shape[:2]
    n = L // BLOCK
    scale = HD_C ** -0.5
    f_cum = jnp.cumsum(logf, axis=1)
    f_key = f_cum.transpose(0, 2, 1)[:, :, None, :]
    kf, vf = k.astype(F32), v.astype(F32)
    kpos = jnp.arange(L)

    def one_block(xs):
        qb, fq, qpos = xs
        s = jnp.einsum('bqhd,bkhd->bhqk', qb.astype(F32), kf) * scale
        s = s + fq.transpose(0, 2, 1)[..., None] - f_key
        s = jnp.where((kpos[None, :] <= qpos[:, None])[None, None], s, -jnp.inf)
        return jnp.einsum('bhqk,bkhd->bqhd', jax.nn.softmax(s, axis=-1), vf)

    qs = jnp.moveaxis(q.reshape(b, n, BLOCK, H_C, HD_C), 1, 0)
    fs = jnp.moveaxis(f_cum.reshape(b, n, BLOCK, H_C), 1, 0)
    o = lax.map(one_block, (qs, fs, kpos.reshape(n, BLOCK)))
    return jnp.moveaxis(o, 0, 1).reshape(b, L, H_C * HD_C)


def fox_sample(q, k, v, logf, k_past, v_past, logf_past):
    b, L = q.shape[:2]
    scale = HD_C ** -0.5
    kp = k_past.reshape(b, -1, H_C, HD_C).astype(F32)
    vp = v_past.reshape(b, -1, H_C, HD_C).astype(F32)
    f_past = jnp.cumsum(logf_past.reshape(b, -1, H_C).astype(F32), axis=1)
    f_new = f_past[:, -1:] + jnp.cumsum(logf, axis=1)
    fq = f_new.transpose(0, 2, 1)[..., None]
    qf = q.astype(F32)
    s_past = jnp.einsum('bqhd,bkhd->bhqk', qf, kp) * scale + fq - f_past.transpose(0, 2, 1)[:, :, None, :]
    s_new = jnp.einsum('bqhd,bkhd->bhqk', qf, k.astype(F32)) * scale + fq - f_new.transpose(0, 2, 1)[:, :, None, :]
    s_new = jnp.where(jnp.tril(jnp.ones((L, L), bool))[None, None], s_new, -jnp.inf)
    p = jax.nn.softmax(jnp.concatenate([s_past, s_new], axis=-1), axis=-1)
    n_past = kp.shape[1]
    o = jnp.einsum('bhqk,bkhd->bqhd', p[..., :n_past], vp) + jnp.einsum('bhqk,bkhd->bqhd', p[..., n_past:], v.astype(F32))
    return o.reshape(b, L, H_C * HD_C)


def ssd(z, xbc, dt_raw, conv_state, conv_w, conv_b, dt_bias, a_log, d_skip, norm_w, s0):
    b, L, _ = z.shape
    xpad = jnp.concatenate([conv_state.astype(xbc.dtype), xbc], axis=1)
    conv = lax.conv_general_dilated(xpad, conv_w.astype(xpad.dtype)[:, None, :], (1,), 'VALID',
                                    dimension_numbers=('NWC', 'WIO', 'NWC'), feature_group_count=CONV_CH)
    xbc = jax.nn.silu(conv + conv_b.astype(conv.dtype))
    xs, bm, cm = jnp.split(xbc, [DI_D, DI_D + G_D * N_D], axis=-1)
    xs = xs.reshape(b, L, H_D, HD_D).astype(F32)
    rep = H_D // G_D
    bh = jnp.repeat(bm.reshape(b, L, G_D, N_D), rep, axis=2)
    ch = jnp.repeat(cm.reshape(b, L, G_D, N_D), rep, axis=2)
    dt = jax.nn.softplus(dt_raw.astype(F32) + dt_bias.astype(F32))
    log_a = dt * (-jnp.exp(a_log.astype(F32)))
    y, s_new = chunked_decay_scan(ch, bh, xs * dt[..., None], log_a, s0)
    y = y + xs * d_skip.astype(F32)[:, None]
    y = y.reshape(b, L, DI_D) * jax.nn.silu(z.astype(F32))
    return rms_norm(y, norm_w), s_new, xpad[:, -(CONV_W - 1):]


def even_mix(h, pos, w_in, w_out, gn_w, sinks, state=None):
    b, L, _ = h.shape
    qa, ka, va, ga, qb, kb, vb = jnp.split(h @ w_in, EVEN_SPLITS, axis=-1)
    qa = rope(qa.reshape(b, L, H_A, DK_A), pos, RET_THETA)
    ka = rope(ka.reshape(b, L, H_A, DK_A), pos, RET_THETA) * (DK_A ** -0.5)
    va = va.reshape(b, L, H_A, DV_A)
    log_gamma = jnp.log1p(-jnp.exp2(-5.0 - jnp.arange(H_A, dtype=F32)))
    log_a = jnp.broadcast_to(log_gamma, (b, L, H_A))
    s0 = jnp.zeros((b, H_A, DK_A, DV_A), F32) if state is None else state[0].astype(F32)
    ret, s_new = chunked_decay_scan(qa, ka, va, log_a, s0)
    ret = head_group_norm(ret, gn_w.reshape(H_A, DV_A)).reshape(b, L, H_A * DV_A)
    out_a = jax.nn.silu(ga.astype(F32)) * ret
    qb = rope(qb.reshape(b, L, H_B, HD_B), pos, ROPE_THETA_B)
    kb = rope(kb.reshape(b, L, KV_B, HD_B), pos, ROPE_THETA_B)
    vb = vb.reshape(b, L, KV_B, HD_B)
    if state is None:
        out_b = swa_prompt(qb, kb, vb, sinks)
        kbuf, vbuf = kb[:, -WINDOW:], vb[:, -WINDOW:]
    else:
        out_b, kbuf, vbuf = swa_sample(qb, kb, vb, state[1], state[2], sinks)
    y = jnp.concatenate([out_a, out_b], axis=-1).astype(h.dtype) @ w_out
    return y, (s_new.astype(h.dtype), kbuf, vbuf)


def odd_mix(h, w_in, w_out, fox_fb, conv_w, conv_b, dt_bias, a_log, d_skip, ssd_norm_w, state=None):
    b, L, _ = h.shape
    qc, kc, vc, fc, z, xbc, dt = jnp.split(h @ w_in, ODD_SPLITS, axis=-1)
    qc = qc.reshape(b, L, H_C, HD_C)
    kc = kc.reshape(b, L, H_C, HD_C)
    vc = vc.reshape(b, L, H_C, HD_C)
    logf = jax.nn.log_sigmoid(fc.astype(F32) + fox_fb.astype(F32))
    if state is None:
        out_c = fox_prompt(qc, kc, vc, logf)
        conv_state = jnp.zeros((b, CONV_W - 1, CONV_CH), h.dtype)
        s0 = jnp.zeros((b, H_D, N_D, HD_D), F32)
    else:
        k_past, v_past, logf_past, s_ssm, conv_state = state
        out_c = fox_sample(qc, kc, vc, logf, k_past, v_past, logf_past)
        s0 = s_ssm.astype(F32)
    out_d, s_new, conv_new = ssd(z, xbc, dt, conv_state, conv_w, conv_b, dt_bias, a_log, d_skip, ssd_norm_w, s0)
    y = jnp.concatenate([out_c, out_d.astype(F32)], axis=-1).astype(h.dtype) @ w_out
    return y, (kc, vc, logf.astype(h.dtype), s_new.astype(h.dtype), conv_new)


def sq_relu_mlp(x, w_up, w_down):
    return jnp.square(jax.nn.relu(x @ w_up)) @ w_down


def setup_inputs(seed: int = 0) -> dict:
    key = jax.random.key(seed)
    ks = iter(jax.random.split(key, 48))

    def nrm(shape, scale):
        return scale * jax.random.normal(next(ks), shape, F32)

    n_pages = PAST_LEN // PAGE_SIZE
    n_phys = (5 * DEC_BATCH * n_pages) // 4
    page_table = jax.random.permutation(next(ks), n_phys)[: DEC_BATCH * n_pages].reshape(DEC_BATCH, n_pages).astype(jnp.int32)
    dt0 = jnp.exp(jax.random.uniform(next(ks), (N_ODD, H_D), F32, math.log(1e-3), math.log(1e-1)))
    inp = {}
    inp['x_prompt'] = nrm((BATCH, SEQ, D_MODEL), 1.0)
    inp['x_sample'] = nrm((DEC_BATCH, DEC_SEQ, D_MODEL), 1.0)
    inp['state_ret'] = nrm((N_EVEN, DEC_BATCH, H_A, DK_A, DV_A), 0.1)
    inp['cache_swa_k'] = nrm((N_EVEN, DEC_BATCH, WINDOW, KV_B, HD_B), 1.0)
    inp['cache_swa_v'] = nrm((N_EVEN, DEC_BATCH, WINDOW, KV_B, HD_B), 1.0)
    inp['cache_fox_k'] = nrm((N_ODD, n_phys, PAGE_SIZE, H_C, HD_C), 1.0)
    inp['cache_fox_v'] = nrm((N_ODD, n_phys, PAGE_SIZE, H_C, HD_C), 1.0)
    inp['cache_fox_logf'] = jax.nn.log_sigmoid(3.0 + nrm((N_ODD, n_phys, PAGE_SIZE, H_C), 1.0))
    inp['state_ssm'] = nrm((N_ODD, DEC_BATCH, H_D, N_D, HD_D), 0.1)
    inp['state_conv'] = nrm((N_ODD, DEC_BATCH, CONV_W - 1, CONV_CH), 1.0)
    inp['page_table'] = page_table
    inp['norm_mix_pre'] = 1.0 + nrm((DEPTH, D_MODEL), 0.02)
    inp['norm_mix_post'] = 1.0 + nrm((DEPTH, D_MODEL), 0.02)
    inp['norm_mlp_pre'] = 1.0 + nrm((DEPTH, D_MODEL), 0.02)
    inp['norm_mlp_post'] = 1.0 + nrm((DEPTH, D_MODEL), 0.02)
    inp['w_in_even'] = nrm((N_EVEN, D_MODEL, EVEN_IN), D_MODEL ** -0.5)
    inp['w_out_even'] = nrm((N_EVEN, MIX_EVEN, D_MODEL), MIX_EVEN ** -0.5)
    inp['ret_norm_w'] = 1.0 + nrm((N_EVEN, H_A * DV_A), 0.02)
    inp['swa_sinks'] = nrm((N_EVEN, H_B), 0.5)
    inp['w_in_odd'] = nrm((N_ODD, D_MODEL, ODD_IN), D_MODEL ** -0.5)
    inp['w_out_odd'] = nrm((N_ODD, MIX_ODD, D_MODEL), MIX_ODD ** -0.5)
    inp['fox_fb'] = jnp.linspace(1.0, 6.0, H_C, dtype=F32)[None, :] + nrm((N_ODD, H_C), 0.1)
    inp['conv_w'] = nrm((N_ODD, CONV_W, CONV_CH), 0.5)
    inp['conv_b'] = nrm((N_ODD, CONV_CH), 0.02)
    inp['dt_bias'] = dt0 + jnp.log(-jnp.expm1(-dt0))
    inp['a_log'] = jnp.log(jax.random.uniform(next(ks), (N_ODD, H_D), F32, 1.0, 16.0))
    inp['d_skip'] = 1.0 + nrm((N_ODD, H_D), 0.1)
    inp['ssd_norm_w'] = 1.0 + nrm((N_ODD, DI_D), 0.02)
    inp['w_up'] = nrm((DEPTH, D_MODEL, D_FF), D_MODEL ** -0.5)
    inp['w_down'] = nrm((DEPTH, D_FF, D_MODEL), D_FF ** -0.5)
    return inp


def reference(x_prompt, x_sample, state_ret, cache_swa_k, cache_swa_v, cache_fox_k, cache_fox_v, cache_fox_logf,
              state_ssm, state_conv, page_table, norm_mix_pre, norm_mix_post, norm_mlp_pre, norm_mlp_post,
              w_in_even, w_out_even, ret_norm_w, swa_sinks, w_in_odd, w_out_odd, fox_fb, conv_w, conv_b,
              dt_bias, a_log, d_skip, ssd_norm_w, w_up, w_down):
    pos_p = jnp.arange(SEQ, dtype=jnp.int32)
    pos_s = PAST_LEN + jnp.arange(DEC_SEQ, dtype=jnp.int32)
    xp, xs = x_prompt, x_sample
    even_p, even_s, odd_p, odd_s = [], [], [], []
    for layer in range(DEPTH):
        i = layer // 2
        hp = rms_norm(xp, norm_mix_pre[layer])
        hs = rms_norm(xs, norm_mix_pre[layer])
        if layer % 2 == 0:
            ev = (w_in_even[i], w_out_even[i], ret_norm_w[i], swa_sinks[i])
            mp, stp = even_mix(hp, pos_p, *ev)
            ms, sts = even_mix(hs, pos_s, *ev, state=(state_ret[i], cache_swa_k[i], cache_swa_v[i]))
            even_p.append(stp)
            even_s.append(sts)
        else:
            od = (w_in_odd[i], w_out_odd[i], fox_fb[i], conv_w[i], conv_b[i], dt_bias[i], a_log[i], d_skip[i], ssd_norm_w[i])
            mp, stp = odd_mix(hp, *od)
            past = (cache_fox_k[i, page_table], cache_fox_v[i, page_table], cache_fox_logf[i, page_table],
                    state_ssm[i], state_conv[i])
            ms, sts = odd_mix(hs, *od, state=past)
            odd_p.append(stp)
            odd_s.append(sts)
        xp = xp + rms_norm(mp, norm_mix_post[layer])
        xs = xs + rms_norm(ms, norm_mix_post[layer])
        xp = xp + rms_norm(sq_relu_mlp(rms_norm(xp, norm_mlp_pre[layer]), w_up[layer], w_down[layer]), norm_mlp_post[layer])
        xs = xs + rms_norm(sq_relu_mlp(rms_norm(xs, norm_mlp_pre[layer]), w_up[layer], w_down[layer]), norm_mlp_post[layer])
    ret_state_p, swa_k_p, swa_v_p = [jnp.stack(t) for t in zip(*even_p)]
    ret_state_s, swa_k_s, swa_v_s = [jnp.stack(t) for t in zip(*even_s)]
    fox_k_p, fox_v_p, fox_logf_p, ssm_p, conv_p = [jnp.stack(t) for t in zip(*odd_p)]
    fox_k_s, fox_v_s, fox_logf_s, ssm_s, conv_s = [jnp.stack(t) for t in zip(*odd_s)]
    return (xp, xs, ret_state_p, ret_state_s, swa_k_p, swa_v_p, swa_k_s, swa_v_s,
            fox_k_p, fox_v_p, fox_logf_p, fox_k_s, fox_v_s, fox_logf_s, ssm_p, ssm_s, conv_p, conv_s)
```

```python
import functools

import numpy as np
import jax
import jax.numpy as jnp
from jax import lax
from jax.experimental import pallas as pl
from jax.experimental.pallas import tpu as pltpu

F32 = jnp.float32
BF16 = jnp.bfloat16

D_MODEL = 2048
BATCH = 4
SEQ = 2048
DEC_BATCH = 32
PAST_LEN = 16384
PAGE_SIZE = 128
D_FF = 4 * D_MODEL
EPS = 1e-6
GN_EPS = 1e-5
CHUNK = 128

H_A, DK_A, DV_A = 8, 64, 128
RET_THETA = 10000.0
H_B, KV_B, HD_B = 16, 2, 64
WINDOW = 128
ROPE_THETA_B = 150000.0
H_C, HD_C = 8, 128
H_D, HD_D, G_D, N_D = 16, 64, 2, 128
CONV_W = 4
DI_D = H_D * HD_D
CONV_CH = DI_D + 2 * G_D * N_D

EVEN_MAIN = 4096
EVEN_SMALL = 256
ODD_MAIN = 5632
ODD_SMALL = 128
DT_LANE = 8

NEG = -1e30
VMEM_LIMIT = 56 * 1024 * 1024

_LOG_GAMMA = [float(v) for v in np.log1p(-np.exp2(-5.0 - np.arange(H_A, dtype=np.float64)))]


def _cparams(sem):
    return pltpu.CompilerParams(dimension_semantics=sem, vmem_limit_bytes=VMEM_LIMIT)


def _silu(x):
    return x * jax.nn.sigmoid(x)


def _softplus(x):
    return jnp.maximum(x, 0.0) + jnp.log1p(jnp.exp(-jnp.abs(x)))


def _log_sigmoid(x):
    return jnp.minimum(x, 0.0) - jnp.log1p(jnp.exp(-jnp.abs(x)))


def _rms(x, g):
    ms = jnp.mean(x * x, axis=-1, keepdims=True)
    return (x * lax.rsqrt(ms + EPS)) * g


def _dot(a, b):
    return jnp.dot(a, b, preferred_element_type=F32)


def _dot_nt(a, b):
    return lax.dot_general(a, b, (((1,), (1,)), ((), ())), preferred_element_type=F32)


def _split3(x):
    hi = x.astype(BF16)
    r = x - hi.astype(F32)
    mid = r.astype(BF16)
    lo = (r - mid.astype(F32)).astype(BF16)
    return hi, mid, lo


def _exact_left01(m01, x):
    hi, mid, lo = _split3(x)
    return _dot(m01, hi) + _dot(m01, mid) + _dot(m01, lo)


def _exact_right01(x, m01):
    hi, mid, lo = _split3(x)
    return _dot(hi, m01) + _dot(mid, m01) + _dot(lo, m01)


def _tri_lower(n):
    r = lax.broadcasted_iota(jnp.int32, (n, n), 0)
    c = lax.broadcasted_iota(jnp.int32, (n, n), 1)
    return jnp.where(r >= c, 1.0, 0.0).astype(BF16)


def _rope64(x, c, s):
    w = x.shape[-1]
    ax = x.ndim - 1
    lane = lax.broadcasted_iota(jnp.int32, x.shape, ax)
    first = (lane & 32) == 0
    left = pltpu.roll(x, w - 32, axis=ax)
    right = pltpu.roll(x, 32, axis=ax)
    return x * c + jnp.where(first, left, right) * s


def _rope_tables(pos, theta, reps):
    inv = 1.0 / (theta ** (jnp.arange(32, dtype=F32) * (2.0 / 64)))
    ang = pos.astype(F32)[:, None] * inv[None, :]
    cos, sin = jnp.cos(ang), jnp.sin(ang)
    c = jnp.concatenate([cos, cos], axis=-1)
    s = jnp.concatenate([-sin, sin], axis=-1)
    return jnp.tile(c, (1, reps)), jnp.tile(s, (1, reps))


def _proj_kernel(*refs, odd, cumsum, tm, seq):
    if odd:
        if cumsum:
            (x_ref, g_ref, w_ref, wsm_ref, fb_ref, o_ref, osm_ref, lf_ref, fc_ref, h_scr, carry_scr) = refs
        else:
            (x_ref, g_ref, w_ref, wsm_ref, fb_ref, o_ref, osm_ref, lf_ref, h_scr) = refs
    else:
        (x_ref, g_ref, w_ref, wsm_ref, o_ref, osm_ref, h_scr) = refs
    i = pl.program_id(0)
    j = pl.program_id(1)

    @pl.when(j == 0)
    def _():
        hb = _rms(x_ref[...], g_ref[...]).astype(BF16)
        h_scr[...] = hb
        sm = _dot(hb, wsm_ref[...])
        osm_ref[...] = sm
        if odd:
            lf = _log_sigmoid(sm + fb_ref[...])
            lf_ref[...] = lf
            if cumsum:
                blk = min(tm, 256)
                tri = _tri_lower(blk)

                @pl.when((i * tm) % seq == 0)
                def _():
                    carry_scr[...] = jnp.zeros_like(carry_scr)

                carry = carry_scr[...]
                for r in range(tm // blk):
                    f = _exact_left01(tri, lf[r * blk:(r + 1) * blk]) + carry
                    fc_ref[r * blk:(r + 1) * blk, :] = f
                    carry = f[blk - 1:blk, :]
                carry_scr[...] = carry

    o_ref[...] = _dot(h_scr[...], w_ref[...])


def _proj(x, g, w_main, w_small, fb=None, *, odd, cumsum, tm, tn, seq):
    m = x.shape[0]
    n_main = w_main.shape[1]
    n_small = w_small.shape[1]
    grid = (m // tm, n_main // tn)
    in_specs = [
        pl.BlockSpec((tm, D_MODEL), lambda i, j: (i, 0)),
        pl.BlockSpec((1, D_MODEL), lambda i, j: (0, 0)),
        pl.BlockSpec((D_MODEL, tn), lambda i, j: (0, j)),
        pl.BlockSpec((D_MODEL, n_small), lambda i, j: (0, 0)),
    ]
    args = [x, g, w_main, w_small]
    out_shape = [jax.ShapeDtypeStruct((m, n_main), F32), jax.ShapeDtypeStruct((m, n_small), F32)]
    out_specs = [pl.BlockSpec((tm, tn), lambda i, j: (i, j)), pl.BlockSpec((tm, n_small), lambda i, j: (i, 0))]
    scratch = [pltpu.VMEM((tm, D_MODEL), BF16)]
    if odd:
        in_specs.append(pl.BlockSpec((1, n_small), lambda i, j: (0, 0)))
        args.append(fb)
        out_shape.append(jax.ShapeDtypeStruct((m, n_small), F32))
        out_specs.append(pl.BlockSpec((tm, n_small), lambda i, j: (i, 0)))
        if cumsum:
            out_shape.append(jax.ShapeDtypeStruct((m, n_small), F32))
            out_specs.append(pl.BlockSpec((tm, n_small), lambda i, j: (i, 0)))
            scratch.append(pltpu.VMEM((1, n_small), F32))
    return pl.pallas_call(
        functools.partial(_proj_kernel, odd=odd, cumsum=cumsum, tm=tm, seq=seq),
        grid=grid, in_specs=in_specs, out_specs=out_specs, out_shape=out_shape,
        scratch_shapes=scratch, compiler_params=_cparams(("arbitrary", "arbitrary")),
        name="proj_odd" if odd else "proj_even",
    )(*args)


def _outproj_kernel(a_ref, b_ref, x_ref, wa_ref, wb_ref, g_ref, o_ref):
    y = _dot(a_ref[...], wa_ref[...]) + _dot(b_ref[...], wb_ref[...])
    o_ref[...] = x_ref[...] + _rms(y, g_ref[...])


def _outproj(a, b, x, wa, wb, g, *, tm):
    m = x.shape[0]
    ka, kb = a.shape[1], b.shape[1]
    return pl.pallas_call(
        _outproj_kernel,
        grid=(m // tm,),
        in_specs=[
            pl.BlockSpec((tm, ka), lambda i: (i, 0)),
            pl.BlockSpec((tm, kb), lambda i: (i, 0)),
            pl.BlockSpec((tm, D_MODEL), lambda i: (i, 0)),
            pl.BlockSpec((ka, D_MODEL), lambda i: (0, 0)),
            pl.BlockSpec((kb, D_MODEL), lambda i: (0, 0)),
            pl.BlockSpec((1, D_MODEL), lambda i: (0, 0)),
        ],
        out_specs=pl.BlockSpec((tm, D_MODEL), lambda i: (i, 0)),
        out_shape=jax.ShapeDtypeStruct((m, D_MODEL), F32),
        compiler_params=_cparams(("arbitrary",)),
        name="outproj",
    )(a, b, x, wa, wb, g)


def _mlp_kernel(x_ref, gpre_ref, gpost_ref, wu_ref, wd_ref, o_ref, h_scr, acc_scr):
    j = pl.program_id(1)

    @pl.when(j == 0)
    def _():
        h_scr[...] = _rms(x_ref[...], gpre_ref[...]).astype(BF16)
        acc_scr[...] = jnp.zeros_like(acc_scr)

    u = jnp.maximum(_dot(h_scr[...], wu_ref[...]), 0.0)
    acc_scr[...] += _dot((u * u).astype(BF16), wd_ref[...])

    @pl.when(j == pl.num_programs(1) - 1)
    def _():
        o_ref[...] = x_ref[...] + _rms(acc_scr[...], gpost_ref[...])


def _mlp(x, gpre, gpost, wu, wd, *, tm, tf):
    m = x.shape[0]
    return pl.pallas_call(
        _mlp_kernel,
        grid=(m // tm, D_FF // tf),
        in_specs=[
            pl.BlockSpec((tm, D_MODEL), lambda i, j: (i, 0)),
            pl.BlockSpec((1, D_MODEL), lambda i, j: (0, 0)),
            pl.BlockSpec((1, D_MODEL), lambda i, j: (0, 0)),
            pl.BlockSpec((D_MODEL, tf), lambda i, j: (0, j)),
            pl.BlockSpec((tf, D_MODEL), lambda i, j: (j, 0)),
        ],
        out_specs=pl.BlockSpec((tm, D_MODEL), lambda i, j: (i, 0)),
        out_shape=jax.ShapeDtypeStruct((m, D_MODEL), F32),
        scratch_shapes=[pltpu.VMEM((tm, D_MODEL), BF16), pltpu.VMEM((tm, D_MODEL), F32)],
        compiler_params=_cparams(("arbitrary", "arbitrary")),
        name="mlp",
    )(x, gpre, gpost, wu, wd)


def _group_norm_gate(y, gate, gw):
    mu = jnp.mean(y, axis=-1, keepdims=True)
    d = y - mu
    var = jnp.mean(d * d, axis=-1, keepdims=True)
    return _silu(gate) * (d * lax.rsqrt(var + GN_EPS) * gw)


def _ret_prompt_kernel(q_ref, k_ref, v_ref, g_ref, cos_ref, sin_ref, gn_ref, oa_ref, st_ref,
                       s_scr, d_scr, e_scr, t_scr):
    b = pl.program_id(0)
    c = pl.program_id(1)
    t_i = lax.broadcasted_iota(jnp.int32, (CHUNK, CHUNK), 0)
    s_i = lax.broadcasted_iota(jnp.int32, (CHUNK, CHUNK), 1)
    lo = s_i < 64

    @pl.when((b == 0) & (c == 0))
    def _():
        tf = t_i.astype(F32)
        sf = s_i.astype(F32)
        for h in range(H_A):
            d_scr[h] = jnp.where(t_i >= s_i, jnp.exp((tf - sf) * _LOG_GAMMA[h]), 0.0)
            e_scr[h] = jnp.exp((tf + 1.0) * _LOG_GAMMA[h])
        for p in range(H_A // 2):
            lg = jnp.where(lo, _LOG_GAMMA[2 * p], _LOG_GAMMA[2 * p + 1])
            t_scr[p] = jnp.exp((CHUNK - 1.0 - tf) * lg)

    @pl.when(c == 0)
    def _():
        s_scr[...] = jnp.zeros_like(s_scr)

    cos, sin = cos_ref[...], sin_ref[...]
    qr = _rope64(q_ref[...], cos, sin)
    kr = _rope64(k_ref[...], cos, sin) * (DK_A ** -0.5)
    top = t_i < 64
    for p in range(H_A // 2):
        sl = slice(128 * p, 128 * (p + 1))
        qp, kp = qr[:, sl], kr[:, sl]
        kb = kp.astype(BF16)
        s_old = s_scr[p]
        s_old_b = s_old.astype(BF16)
        ktt = jnp.transpose(kp * t_scr[p]).astype(BF16)
        upd = []
        for e in range(2):
            h = 2 * p + e
            hs = slice(128 * h, 128 * (h + 1))
            qm = jnp.where(lo if e == 0 else ~lo, qp, 0.0).astype(BF16)
            vh = v_ref[:, hs].astype(BF16)
            att = (_dot_nt(qm, kb) * d_scr[h]).astype(BF16)
            y = _dot(att, vh) + _dot(qm, s_old_b) * e_scr[h]
            oa_ref[:, hs] = _group_norm_gate(y, g_ref[:, hs], gn_ref[:, hs]).astype(BF16)
            upd.append(_dot(ktt, vh))
        g128 = jnp.where(top, float(np.exp(CHUNK * _LOG_GAMMA[2 * p])), float(np.exp(CHUNK * _LOG_GAMMA[2 * p + 1])))
        s_scr[p] = g128 * s_old + jnp.where(top, upd[0], upd[1])

    @pl.when(c == pl.num_programs(1) - 1)
    def _():
        st_ref[...] = s_scr[...]


def _ret_prompt(p_main, cos, sin, gn_w, *, nb, nc):
    m = nb * nc * CHUNK
    row = lambda b, c: b * nc + c
    return pl.pallas_call(
        _ret_prompt_kernel,
        grid=(nb, nc),
        in_specs=[
            pl.BlockSpec((CHUNK, 512), lambda b, c: (row(b, c), 0)),
            pl.BlockSpec((CHUNK, 512), lambda b, c: (row(b, c), 1)),
            pl.BlockSpec((CHUNK, 1024), lambda b, c: (row(b, c), 1)),
            pl.BlockSpec((CHUNK, 1024), lambda b, c: (row(b, c), 2)),
            pl.BlockSpec((CHUNK, 512), lambda b, c: (c, 0)),
            pl.BlockSpec((CHUNK, 512), lambda b, c: (c, 0)),
            pl.BlockSpec((1, 1024), lambda b, c: (0, 0)),
        ],
        out_specs=[
            pl.BlockSpec((CHUNK, 1024), lambda b, c: (row(b, c), 0)),
            pl.BlockSpec((None, 4, 128, 128), lambda b, c: (b, 0, 0, 0)),
        ],
        out_shape=[jax.ShapeDtypeStruct((m, 1024), BF16), jax.ShapeDtypeStruct((nb, 4, 128, 128), F32)],
        scratch_shapes=[pltpu.VMEM((4, 128, 128), F32), pltpu.VMEM((H_A, 128, 128), F32),
                        pltpu.VMEM((H_A, 128, 128), F32), pltpu.VMEM((4, 128, 128), F32)],
        compiler_params=_cparams(("arbitrary", "arbitrary")),
        name="ret_prompt",
    )(p_main, p_main, p_main, p_main, cos, sin, gn_w)


def _swa_padded(x2):
    lane = lax.broadcasted_iota(jnp.int32, x2.shape, 1)
    lo = lane < 64
    xr = pltpu.roll(x2, 64, axis=1)
    z = jnp.zeros_like(x2)
    return {
        (0, 0): jnp.where(lo, x2, z).astype(BF16), (0, 1): jnp.where(lo, z, xr).astype(BF16),
        (1, 0): jnp.where(lo, xr, z).astype(BF16), (1, 1): jnp.where(lo, z, x2).astype(BF16),
    }


def _swa_attend(q_pair_fn, kpad, vpad, valid, sink_ref, store_fn):
    for jj in range(H_B // 2):
        g = (2 * jj) // (H_B // KV_B)
        qp = q_pair_fn(jj)
        acc = None
        for e in range(2):
            s = _dot_nt(qp, kpad[(g, e)])
            if valid is not None:
                s = jnp.where(valid, s, NEG)
            sink = sink_ref[2 * jj + e]
            mx = jnp.maximum(jnp.max(s, axis=-1, keepdims=True), sink)
            pr = jnp.exp(s - mx)
            den = jnp.sum(pr, axis=-1, keepdims=True) + jnp.exp(sink - mx)
            o = _dot((pr / den).astype(BF16), vpad[(g, e)])
            acc = o if acc is None else acc + o
        store_fn(jj, acc)


def _swa_prompt_kernel(sink_ref, q_ref, kvc_ref, kvp_ref, cc_ref, sc_ref, cp_ref, sp_ref,
                       ob_ref, ko_ref, vo_ref):
    n = pl.program_id(1)
    cc, sc = cc_ref[...], sc_ref[...]
    kvc, kvp = kvc_ref[...], kvp_ref[...]
    kc = _rope64(kvc[:, :128], cc, sc)
    kp = _rope64(kvp[:, :128], cp_ref[...], sp_ref[...])
    vc = kvc[:, 128:]
    kpad = _swa_padded(jnp.concatenate([kp, kc], axis=0))
    vpad = _swa_padded(jnp.concatenate([kvp[:, 128:], vc], axis=0))
    i = lax.broadcasted_iota(jnp.int32, (WINDOW, 2 * WINDOW), 0)
    j = lax.broadcasted_iota(jnp.int32, (WINDOW, 2 * WINDOW), 1)
    valid = (j >= i + 1) & (j <= i + WINDOW) & ((n > 0) | (j >= WINDOW))

    def q_pair(jj):
        return (_rope64(q_ref[:, 128 * jj:128 * (jj + 1)], cc, sc) * (HD_B ** -0.5)).astype(BF16)

    def store(jj, acc):
        ob_ref[:, 128 * jj:128 * (jj + 1)] = acc.astype(BF16)

    _swa_attend(q_pair, kpad, vpad, valid, sink_ref, store)

    @pl.when(n == pl.num_programs(1) - 1)
    def _():
        ko_ref[...] = kc
        vo_ref[...] = vc


def _swa_prompt(p_main, p_small, cos, sin, sinks, *, nb, nc):
    m = nb * nc * CHUNK
    row = lambda b, n: b * nc + n
    prev = lambda b, n: b * nc + jnp.maximum(n - 1, 0)
    return pl.pallas_call(
        _swa_prompt_kernel,
        grid=(nb, nc),
        in_specs=[
            pl.BlockSpec(memory_space=pltpu.SMEM),
            pl.BlockSpec((CHUNK, 1024), lambda b, n: (row(b, n), 3)),
            pl.BlockSpec((CHUNK, 256), lambda b, n: (row(b, n), 0)),
            pl.BlockSpec((CHUNK, 256), lambda b, n: (prev(b, n), 0)),
            pl.BlockSpec((CHUNK, 128), lambda b, n: (n, 0)),
            pl.BlockSpec((CHUNK, 128), lambda b, n: (n, 0)),
            pl.BlockSpec((CHUNK, 128), lambda b, n: (jnp.maximum(n - 1, 0), 0)),
            pl.BlockSpec((CHUNK, 128), lambda b, n: (jnp.maximum(n - 1, 0), 0)),
        ],
        out_specs=[
            pl.BlockSpec((CHUNK, 1024), lambda b, n: (row(b, n), 0)),
            pl.BlockSpec((None, WINDOW, 128), lambda b, n: (b, 0, 0)),
            pl.BlockSpec((None, WINDOW, 128), lambda b, n: (b, 0, 0)),
        ],
        out_shape=[jax.ShapeDtypeStruct((m, 1024), BF16), jax.ShapeDtypeStruct((nb, WINDOW, 128), F32),
                   jax.ShapeDtypeStruct((nb, WINDOW, 128), F32)],
        compiler_params=_cparams(("arbitrary", "arbitrary")),
        name="swa_prompt",
    )(sinks, p_main, p_small, p_small, cos, sin, cos, sin)


def _col_bcast(row128):
    return jnp.transpose(jnp.broadcast_to(row128, (128, 128)))


def _even_sample_kernel(sink_ref, p_ref, ps_ref, s0_ref, kbuf_ref, vbuf_ref, ca_ref, sa_ref, cb_ref, sb_ref,
                        gn_ref, mix_ref, st_ref, ko_ref, vo_ref):
    row = p_ref[...]
    ca, sa = ca_ref[...], sa_ref[...]
    qr = _rope64(jnp.broadcast_to(row[:, 0:512], (8, 512)), ca, sa)
    kr = _rope64(jnp.broadcast_to(row[:, 512:1024], (8, 512)), ca, sa) * (DK_A ** -0.5)
    r_i = lax.broadcasted_iota(jnp.int32, (128, 128), 0)
    top = r_i < 64
    for p in range(H_A // 2):
        sl = slice(128 * p, 128 * (p + 1))
        kcol = _col_bcast(kr[0:1, sl])
        qcol = _col_bcast(qr[0:1, sl])
        he, ho = 2 * p, 2 * p + 1
        v_e = row[:, 1024 + 128 * he:1024 + 128 * (he + 1)]
        v_o = row[:, 1024 + 128 * ho:1024 + 128 * (ho + 1)]
        gam = jnp.where(top, float(np.exp(_LOG_GAMMA[he])), float(np.exp(_LOG_GAMMA[ho])))
        s_new = gam * s0_ref[p] + kcol * jnp.where(top, v_e, v_o)
        st_ref[p] = s_new
        prod = qcol * s_new
        for e, h in ((0, he), (1, ho)):
            y = jnp.sum(jnp.where(top if e == 0 else ~top, prod, 0.0), axis=0, keepdims=True)
            hs = slice(128 * h, 128 * (h + 1))
            gate = row[:, 2048 + 128 * h:2048 + 128 * (h + 1)]
            mix_ref[:, hs] = _group_norm_gate(y, gate, gn_ref[:, hs])

    cb, sb = cb_ref[...], sb_ref[...]
    ps = ps_ref[...]
    kn = _rope64(jnp.broadcast_to(ps[:, 0:128], (8, 128)), cb, sb)[0:1]
    vn = ps[:, 128:256]
    last = r_i == WINDOW - 1
    k_new = jnp.where(last, kn, pltpu.roll(kbuf_ref[...], WINDOW - 1, axis=0))
    v_new = jnp.where(last, vn, pltpu.roll(vbuf_ref[...], WINDOW - 1, axis=0))
    ko_ref[...] = k_new
    vo_ref[...] = v_new
    kpad = _swa_padded(k_new)
    vpad = _swa_padded(v_new)

    def q_pair(jj):
        q8 = jnp.broadcast_to(row[:, 3072 + 128 * jj:3072 + 128 * (jj + 1)], (8, 128))
        return (_rope64(q8, cb, sb) * (HD_B ** -0.5)).astype(BF16)

    def store(jj, acc):
        mix_ref[:, 1024 + 128 * jj:1024 + 128 * (jj + 1)] = acc[0:1]

    _swa_attend(q_pair, kpad, vpad, None, sink_ref, store)


def _even_sample(p_main, p_small, s0, kbuf, vbuf, ca, sa, cb, sb, gn_w, sinks):
    nb = p_main.shape[0]
    b3 = lambda b: (b, 0, 0)
    b4 = lambda b: (b, 0, 0, 0)
    c2 = lambda b: (0, 0)
    return pl.pallas_call(
        _even_sample_kernel,
        grid=(nb,),
        in_specs=[
            pl.BlockSpec(memory_space=pltpu.SMEM),
            pl.BlockSpec((None, 1, EVEN_MAIN), b3),
            pl.BlockSpec((None, 1, EVEN_SMALL), b3),
            pl.BlockSpec((None, 4, 128, 128), b4),
            pl.BlockSpec((None, WINDOW, 128), b3),
            pl.BlockSpec((None, WINDOW, 128), b3),
            pl.BlockSpec((1, 512), c2), pl.BlockSpec((1, 512), c2),
            pl.BlockSpec((1, 128), c2), pl.BlockSpec((1, 128), c2),
            pl.BlockSpec((1, 1024), c2),
        ],
        out_specs=[
            pl.BlockSpec((None, 1, 2048), b3),
            pl.BlockSpec((None, 4, 128, 128), b4),
            pl.BlockSpec((None, WINDOW, 128), b3),
            pl.BlockSpec((None, WINDOW, 128), b3),
        ],
        out_shape=[jax.ShapeDtypeStruct((nb, 1, 2048), F32), jax.ShapeDtypeStruct((nb, 4, 128, 128), F32),
                   jax.ShapeDtypeStruct((nb, WINDOW, 128), F32), jax.ShapeDtypeStruct((nb, WINDOW, 128), F32)],
        compiler_params=_cparams(("arbitrary",)),
        name="even_sample",
    )(sinks, p_main.reshape(nb, 1, EVEN_MAIN), p_small.reshape(nb, 1, EVEN_SMALL), s0, kbuf, vbuf,
      ca, sa, cb, sb, gn_w)


def _fox_prompt_kernel(q_ref, k_ref, v_ref, fcol_ref, frow_ref, o_ref, kb_scr, vb_scr, *, tq):
    qi = pl.program_id(2)

    @pl.when(qi == 0)
    def _():
        kb_scr[...] = k_ref[...].astype(BF16)
        vb_scr[...] = v_ref[...].astype(BF16)

    q = (q_ref[...] * (HD_C ** -0.5)).astype(BF16)
    fq = fcol_ref[...]
    r_i = lax.broadcasted_iota(jnp.int32, (tq, tq), 0)
    c_i = lax.broadcasted_iota(jnp.int32, (tq, tq), 1)

    def body(j, carry):
        m, l, acc = carry
        off = pl.multiple_of(j * tq, tq)
        s = _dot_nt(q, kb_scr[pl.ds(off, tq), :]) + (fq - frow_ref[j])
        s = jnp.where((j < qi) | (c_i <= r_i), s, NEG)
        m_new = jnp.maximum(m, jnp.max(s, axis=-1, keepdims=True))
        a = jnp.exp(m - m_new)
        p = jnp.exp(s - m_new)
        l = a * l + jnp.sum(p, axis=-1, keepdims=True)
        acc = a * acc + _dot(p.astype(BF16), vb_scr[pl.ds(off, tq), :])
        return m_new, l, acc

    init = (jnp.full((tq, 1), NEG, F32), jnp.zeros((tq, 1), F32), jnp.zeros((tq, HD_C), F32))
    m, l, acc = lax.fori_loop(0, qi + 1, body, init)
    o_ref[...] = (acc / l).astype(BF16)


def _fox_prompt(p_main, fcol, frow, *, nb, seq, tq):
    nq = seq // tq
    m = nb * seq
    return pl.pallas_call(
        functools.partial(_fox_prompt_kernel, tq=tq),
        grid=(nb, H_C, nq),
        in_specs=[
            pl.BlockSpec((tq, HD_C), lambda b, h, i: (b * nq + i, h)),
            pl.BlockSpec((seq, HD_C), lambda b, h, i: (b, H_C + h)),
            pl.BlockSpec((seq, HD_C), lambda b, h, i: (b, 2 * H_C + h)),
            pl.BlockSpec((None, None, tq, 1), lambda b, h, i: (b, h, i, 0)),
            pl.BlockSpec((None, None, nq, 1, tq), lambda b, h, i: (b, h, 0, 0, 0)),
        ],
        out_specs=pl.BlockSpec((tq, HD_C), lambda b, h, i: (b * nq + i, h)),
        out_shape=jax.ShapeDtypeStruct((m, H_C * HD_C), BF16),
        scratch_shapes=[pltpu.VMEM((seq, HD_C), BF16), pltpu.VMEM((seq, HD_C), BF16)],
        compiler_params=_cparams(("arbitrary", "arbitrary", "arbitrary")),
        name="fox_prompt",
    )(p_main, p_main, p_main, fcol, frow)


def _fox_decode_kernel(*refs, pp):
    pt_ref = refs[0]
    q_ref, kn_ref, vn_ref, lfn_ref = refs[1:5]
    k_refs = refs[5:5 + pp]
    v_refs = refs[5 + pp:5 + 2 * pp]
    lf_refs = refs[5 + 2 * pp:5 + 3 * pp]
    o_ref = refs[5 + 3 * pp]
    m_scr, l_scr, acc_scr, carry_scr, rexp_scr, tri_scr = refs[6 + 3 * pp:]
    del pt_ref
    b = pl.program_id(0)
    s_id = pl.program_id(1)
    rows = PAGE_SIZE * H_C

    @pl.when((b == 0) & (s_id == 0))
    def _():
        r = lax.broadcasted_iota(jnp.int32, (PAGE_SIZE, rows), 0)
        c = lax.broadcasted_iota(jnp.int32, (PAGE_SIZE, rows), 1)
        rexp_scr[...] = jnp.where((c >> 3) == r, 1.0, 0.0).astype(BF16)
        r2 = lax.broadcasted_iota(jnp.int32, (PAGE_SIZE, PAGE_SIZE), 0)
        c2 = lax.broadcasted_iota(jnp.int32, (PAGE_SIZE, PAGE_SIZE), 1)
        tri_scr[...] = jnp.where(r2 <= c2, 1.0, 0.0).astype(BF16)

    @pl.when(s_id == 0)
    def _():
        m_scr[...] = jnp.full_like(m_scr, NEG)
        l_scr[...] = jnp.zeros_like(l_scr)
        acc_scr[...] = jnp.zeros_like(acc_scr)
        carry_scr[...] = jnp.zeros_like(carry_scr)

    qs = q_ref[...] * (HD_C ** -0.5)
    h_i = lax.broadcasted_iota(jnp.int32, (H_C, rows), 0)
    c_i = lax.broadcasted_iota(jnp.int32, (H_C, rows), 1)
    diag = (c_i & (H_C - 1)) == h_i
    tri = tri_scr[...]
    rexp = rexp_scr[...]
    m, l, acc, carry = m_scr[...], l_scr[...], acc_scr[...], carry_scr[...]
    for r in range(pp):
        lf = lf_refs[r][...]
        hi, mid, lo = _split3(lf)
        tdot = lambda x: lax.dot_general(x, tri, (((0,), (0,)), ((), ())), preferred_element_type=F32)
        cum = tdot(hi) + tdot(mid) + tdot(lo) + carry
        carry = cum[:, PAGE_SIZE - 1:PAGE_SIZE]
        gexp = _exact_right01(cum, rexp)
        k2 = k_refs[r][...].reshape(rows, HD_C)
        v2 = v_refs[r][...].reshape(rows, HD_C)
        s = jnp.where(diag, _dot_nt(qs, k2) - gexp, NEG)
        m_new = jnp.maximum(m, jnp.max(s, axis=-1, keepdims=True))
        a = jnp.exp(m - m_new)
        p = jnp.exp(s - m_new)
        l = a * l + jnp.sum(p, axis=-1, keepdims=True)
        acc = a * acc + _dot(p, v2)
        m = m_new
    m_scr[...] = m
    l_scr[...] = l
    acc_scr[...] = acc
    carry_scr[...] = carry

    @pl.when(s_id == pl.num_programs(1) - 1)
    def _():
        fq = carry + lfn_ref[...]
        s_new = jnp.sum(qs * kn_ref[...], axis=-1, keepdims=True)
        m_past = m + fq
        mx = jnp.maximum(m_past, s_new)
        wp = jnp.exp(m_past - mx)
        wn = jnp.exp(s_new - mx)
        o_ref[...] = (acc * wp + wn * vn_ref[...]) / (l * wp + wn)


def _fox_decode(page_table, q, k_new, v_new, lf_new, cache_k, cache_v, cache_lf, *, pp):
    nb, n_pages = page_table.shape
    b3 = lambda b, s, pt: (b, 0, 0)
    kv_spec = lambda r: pl.BlockSpec((None, PAGE_SIZE, H_C, HD_C), lambda b, s, pt: (pt[b, s * pp + r], 0, 0, 0))
    lf_spec = lambda r: pl.BlockSpec((None, PAGE_SIZE, H_C), lambda b, s, pt: (pt[b, s * pp + r], 0, 0))
    in_specs = [pl.BlockSpec((None, H_C, HD_C), b3), pl.BlockSpec((None, H_C, HD_C), b3),
                pl.BlockSpec((None, H_C, HD_C), b3), pl.BlockSpec((None, H_C, 1), b3)]
    in_specs += [kv_spec(r) for r in range(pp)] + [kv_spec(r) for r in range(pp)] + [lf_spec(r) for r in range(pp)]
    grid_spec = pltpu.PrefetchScalarGridSpec(
        num_scalar_prefetch=1, grid=(nb, n_pages // pp), in_specs=in_specs,
        out_specs=pl.BlockSpec((None, H_C, HD_C), b3),
        scratch_shapes=[pltpu.VMEM((H_C, 1), F32), pltpu.VMEM((H_C, 1), F32), pltpu.VMEM((H_C, HD_C), F32),
                        pltpu.VMEM((H_C, 1), F32), pltpu.VMEM((PAGE_SIZE, PAGE_SIZE * H_C), BF16),
                        pltpu.VMEM((PAGE_SIZE, PAGE_SIZE), BF16)],
    )
    return pl.pallas_call(
        functools.partial(_fox_decode_kernel, pp=pp),
        grid_spec=grid_spec,
        out_shape=jax.ShapeDtypeStruct((nb, H_C, HD_C), F32),
        compiler_params=_cparams(("arbitrary", "arbitrary")),
        name="fox_decode",
    )(page_table, q, k_new, v_new, lf_new, *([cache_k] * pp), *([cache_v] * pp), *([cache_lf] * pp))


def _lane_col(x, lane):
    return x[:, lane:lane + 1]


def _ssd_prompt_kernel(x_ref, bc_ref, z_ref, dt_ref, cwx_ref, cwbc_ref, cbx_ref, cbbc_ref, dtb_ref, alog_ref,
                       dskip_ref, nw_ref, od_ref, st_ref, cv_ref, xpx_scr, xpbc_scr, s_scr, y_scr):
    c = pl.program_id(1)
    nc = pl.num_programs(1)

    @pl.when(c == 0)
    def _():
        xpx_scr[0:8, :] = jnp.zeros((8, DI_D), F32)
        xpbc_scr[0:8, :] = jnp.zeros((8, 512), F32)
        s_scr[...] = jnp.zeros_like(s_scr)

    xpx_scr[8:8 + CHUNK, :] = x_ref[...]
    xpbc_scr[8:8 + CHUNK, :] = bc_ref[...]
    cx = cbx_ref[...]
    cbc = cbbc_ref[...]
    for k in range(CONV_W):
        w = CONV_W - 1 - k
        cx = cx + cwx_ref[w:w + 1, :] * xpx_scr[8 - k:8 - k + CHUNK, :]
        cbc = cbc + cwbc_ref[w:w + 1, :] * xpbc_scr[8 - k:8 - k + CHUNK, :]

    @pl.when(c == nc - 1)
    def _():
        cv_ref[:, 0:DI_D] = xpx_scr[CHUNK + 5:CHUNK + 8, :]
        cv_ref[:, DI_D:CONV_CH] = xpbc_scr[CHUNK + 5:CHUNK + 8, :]

    xpx_scr[0:8, :] = xpx_scr[CHUNK:CHUNK + 8, :]
    xpbc_scr[0:8, :] = xpbc_scr[CHUNK:CHUNK + 8, :]

    xs = _silu(cx)
    bcs = _silu(cbc)
    dt = _softplus(dt_ref[...] + dtb_ref[...])
    la = dt * (-jnp.exp(alog_ref[...]))
    cum = _exact_left01(_tri_lower(CHUNK), la)
    cum_t = jnp.transpose(cum)
    t_i = lax.broadcasted_iota(jnp.int32, (CHUNK, CHUNK), 0)
    s_i = lax.broadcasted_iota(jnp.int32, (CHUNK, CHUNK), 1)
    causal = t_i >= s_i
    lo = s_i < 64
    att_base, bt = [], []
    for g in range(G_D):
        bg = bcs[:, 128 * g:128 * (g + 1)]
        cg = bcs[:, 256 + 128 * g:256 + 128 * (g + 1)]
        att_base.append(_dot_nt(cg.astype(BF16), bg.astype(BF16)))
        bt.append(jnp.transpose(bg).astype(BF16))
    ss = jnp.zeros((CHUNK, 1), F32)
    for p in range(H_D // 2):
        g = (2 * p) // (H_D // G_D)
        sl = slice(128 * p, 128 * (p + 1))
        le, lo_ = DT_LANE + 2 * p, DT_LANE + 2 * p + 1
        dt_pair = jnp.where(lo, _lane_col(dt, le), _lane_col(dt, lo_))
        cum_pair = jnp.where(lo, _lane_col(cum, le), _lane_col(cum, lo_))
        clast = cum_pair[CHUNK - 1:CHUNK, :]
        xs_p = xs[:, sl]
        xdt = xs_p * dt_pair
        s_old = s_scr[p]
        cgb = bcs[:, 256 + 128 * g:256 + 128 * (g + 1)].astype(BF16)
        y = _dot(cgb, s_old.astype(BF16)) * jnp.exp(cum_pair)
        for e in range(2):
            ln = DT_LANE + 2 * p + e
            diff = _lane_col(cum, ln) - cum_t[ln:ln + 1, :]
            att = (att_base[g] * jnp.exp(jnp.where(causal, diff, NEG))).astype(BF16)
            xm = jnp.where(lo if e == 0 else ~lo, xdt, 0.0).astype(BF16)
            y = y + _dot(att, xm)
        y = (y + xs_p * dskip_ref[:, sl]) * _silu(z_ref[:, sl])
        y_scr[:, sl] = y
        ss = ss + jnp.sum(y * y, axis=-1, keepdims=True)
        txdt = (xdt * jnp.exp(clast - cum_pair)).astype(BF16)
        s_scr[p] = jnp.exp(clast) * s_old + _dot(bt[g], txdt)
    inv = lax.rsqrt(ss * (1.0 / DI_D) + EPS)
    od_ref[...] = (y_scr[...] * inv * nw_ref[...]).astype(BF16)

    @pl.when(c == nc - 1)
    def _():
        st_ref[...] = s_scr[...]


def _ssd_prompt(p_main, p_small, cwx, cwbc, cbx, cbbc, dtb, alog, dskip, nw, *, nb, nc):
    m = nb * nc * CHUNK
    row = lambda b, c: b * nc + c
    c2 = lambda b, c: (0, 0)
    return pl.pallas_call(
        _ssd_prompt_kernel,
        grid=(nb, nc),
        in_specs=[
            pl.BlockSpec((CHUNK, 1024), lambda b, c: (row(b, c), 4)),
            pl.BlockSpec((CHUNK, 512), lambda b, c: (row(b, c), 10)),
            pl.BlockSpec((CHUNK, 1024), lambda b, c: (row(b, c), 3)),
            pl.BlockSpec((CHUNK, ODD_SMALL), lambda b, c: (row(b, c), 0)),
            pl.BlockSpec((CONV_W, DI_D), c2), pl.BlockSpec((CONV_W, 512), c2),
            pl.BlockSpec((1, DI_D), c2), pl.BlockSpec((1, 512), c2),
            pl.BlockSpec((1, ODD_SMALL), c2), pl.BlockSpec((1, ODD_SMALL), c2),
            pl.BlockSpec((1, DI_D), c2), pl.BlockSpec((1, DI_D), c2),
        ],
        out_specs=[
            pl.BlockSpec((CHUNK, DI_D), lambda b, c: (row(b, c), 0)),
            pl.BlockSpec((None, 8, 128, 128), lambda b, c: (b, 0, 0, 0)),
            pl.BlockSpec((None, CONV_W - 1, CONV_CH), lambda b, c: (b, 0, 0)),
        ],
        out_shape=[jax.ShapeDtypeStruct((m, DI_D), BF16), jax.ShapeDtypeStruct((nb, 8, 128, 128), F32),
                   jax.ShapeDtypeStruct((nb, CONV_W - 1, CONV_CH), F32)],
        scratch_shapes=[pltpu.VMEM((CHUNK + 8, DI_D), F32), pltpu.VMEM((CHUNK + 8, 512), F32),
                        pltpu.VMEM((8, 128, 128), F32), pltpu.VMEM((CHUNK, DI_D), F32)],
        compiler_params=_cparams(("arbitrary", "arbitrary")),
        name="ssd_prompt",
    )(p_main, p_main, p_main, p_small, cwx, cwbc, cbx, cbbc, dtb, alog, dskip, nw)


def _ssd_sample_kernel(p_ref, dt_ref, cs_ref, s0_ref, cwx_ref, cwbc_ref, cbx_ref, cbbc_ref, dtb_ref, alog_ref,
                       dskip_ref, nw_ref, od_ref, st_ref, cv_ref):
    row = p_ref[...]
    xn = row[:, 4096:5120]
    bcn = row[:, 5120:5632]
    cx = cbx_ref[...] + cwx_ref[3:4, :] * xn
    cbc = cbbc_ref[...] + cwbc_ref[3:4, :] * bcn
    for w in range(CONV_W - 1):
        cx = cx + cwx_ref[w:w + 1, :] * cs_ref[w:w + 1, 0:DI_D]
        cbc = cbc + cwbc_ref[w:w + 1, :] * cs_ref[w:w + 1, DI_D:CONV_CH]
    cv_ref[0:1, :] = cs_ref[1:2, :]
    cv_ref[1:2, :] = cs_ref[2:3, :]
    cv_ref[2:3, 0:DI_D] = xn
    cv_ref[2:3, DI_D:CONV_CH] = bcn
    xs = _silu(cx)
    bcs = _silu(cbc)
    dt = _softplus(dt_ref[...] + dtb_ref[...])
    da = jnp.exp(dt * (-jnp.exp(alog_ref[...])))
    lane = lax.broadcasted_iota(jnp.int32, (1, 128), 1)
    lo = lane < 64
    bcol = [_col_bcast(bcs[:, 128 * g:128 * (g + 1)]) for g in range(G_D)]
    ccol = [_col_bcast(bcs[:, 256 + 128 * g:256 + 128 * (g + 1)]) for g in range(G_D)]
    ys = []
    ss = jnp.zeros((1, 1), F32)
    for p in range(H_D // 2):
        g = (2 * p) // (H_D // G_D)
        sl = slice(128 * p, 128 * (p + 1))
        le, lo_ = DT_LANE + 2 * p, DT_LANE + 2 * p + 1
        dt_pair = jnp.where(lo, _lane_col(dt, le), _lane_col(dt, lo_))
        da_pair = jnp.where(lo, _lane_col(da, le), _lane_col(da, lo_))
        xs_p = xs[:, sl]
        s_new = da_pair * s0_ref[p] + bcol[g] * (xs_p * dt_pair)
        st_ref[p] = s_new
        y = jnp.sum(ccol[g] * s_new, axis=0, keepdims=True)
        y = (y + xs_p * dskip_ref[:, sl]) * _silu(row[:, 3072 + 128 * p:3072 + 128 * (p + 1)])
        ys.append(y)
        ss = ss + jnp.sum(y * y, axis=-1, keepdims=True)
    inv = lax.rsqrt(ss * (1.0 / DI_D) + EPS)
    for p in range(H_D // 2):
        sl = slice(128 * p, 128 * (p + 1))
        od_ref[:, sl] = ys[p] * inv * nw_ref[:, sl]


def _ssd_sample(p_main, p_small, cs, s0, cwx, cwbc, cbx, cbbc, dtb, alog, dskip, nw):
    nb = p_main.shape[0]
    b3 = lambda b: (b, 0, 0)
    b4 = lambda b: (b, 0, 0, 0)
    c2 = lambda b: (0, 0)
    return pl.pallas_call(
        _ssd_sample_kernel,
        grid=(nb,),
        in_specs=[
            pl.BlockSpec((None, 1, ODD_MAIN), b3),
            pl.BlockSpec((None, 1, ODD_SMALL), b3),
            pl.BlockSpec((None, CONV_W - 1, CONV_CH), b3),
            pl.BlockSpec((None, 8, 128, 128), b4),
            pl.BlockSpec((CONV_W, DI_D), c2), pl.BlockSpec((CONV_W, 512), c2),
            pl.BlockSpec((1, DI_D), c2), pl.BlockSpec((1, 512), c2),
            pl.BlockSpec((1, ODD_SMALL), c2), pl.BlockSpec((1, ODD_SMALL), c2),
            pl.BlockSpec((1, DI_D), c2), pl.BlockSpec((1, DI_D), c2),
        ],
        out_specs=[
            pl.BlockSpec((None, 1, DI_D), b3),
            pl.BlockSpec((None, 8, 128, 128), b4),
            pl.BlockSpec((None, CONV_W - 1, CONV_CH), b3),
        ],
        out_shape=[jax.ShapeDtypeStruct((nb, 1, DI_D), F32), jax.ShapeDtypeStruct((nb, 8, 128, 128), F32),
                   jax.ShapeDtypeStruct((nb, CONV_W - 1, CONV_CH), F32)],
        compiler_params=_cparams(("arbitrary",)),
        name="ssd_sample",
    )(p_main.reshape(nb, 1, ODD_MAIN), p_small.reshape(nb, 1, ODD_SMALL), cs, s0,
      cwx, cwbc, cbx, cbbc, dtb, alog, dskip, nw)


def _pairs_to_heads(s):
    b, p, n, _ = s.shape
    return s.reshape(b, p, n, 2, 64).transpose(0, 1, 3, 2, 4).reshape(b, 2 * p, n, 64)


def _heads_to_pairs(s):
    b, h, n, d = s.shape
    return s.reshape(b, h // 2, 2, n, d).transpose(0, 1, 3, 2, 4).reshape(b, h // 2, n, 2 * d)


def _pad_lanes(v, start, width=ODD_SMALL):
    out = jnp.zeros((1, width), F32)
    return lax.dynamic_update_slice(out, v.reshape(1, -1).astype(F32), (0, start))


def kernel(x_prompt, x_sample, state_ret, cache_swa_k, cache_swa_v, cache_fox_k, cache_fox_v, cache_fox_logf,
           state_ssm, state_conv, page_table, norm_mix_pre, norm_mix_post, norm_mlp_pre, norm_mlp_post,
           w_in_even, w_out_even, ret_norm_w, swa_sinks, w_in_odd, w_out_odd, fox_fb, conv_w, conv_b,
           dt_bias, a_log, d_skip, ssd_norm_w, w_up, w_down):
    nb, seq = BATCH, SEQ
    nc = seq // CHUNK
    mp = nb * seq
    ms = DEC_BATCH
    xp = x_prompt.reshape(mp, D_MODEL)
    xs = x_sample.reshape(ms, D_MODEL)
    row = lambda v: v.reshape(1, -1)

    pos_p = jnp.arange(seq, dtype=jnp.int32)
    pos_s = jnp.full((1,), PAST_LEN, dtype=jnp.int32)
    ca_p, sa_p = _rope_tables(pos_p, RET_THETA, 8)
    cb_p, sb_p = _rope_tables(pos_p, ROPE_THETA_B, 2)
    ca_s, sa_s = _rope_tables(pos_s, RET_THETA, 8)
    cb_s, sb_s = _rope_tables(pos_s, ROPE_THETA_B, 2)

    we = w_in_even[0]
    we_main = we[:, :EVEN_MAIN].astype(BF16)
    we_small = we[:, EVEN_MAIN:].astype(BF16)
    wo = w_out_even[0].astype(BF16)
    wo_a, wo_b = wo[:1024], wo[1024:]
    wu0, wd0 = w_up[0].astype(BF16), w_down[0].astype(BF16)
    g_pre, g_post = row(norm_mix_pre[0]), row(norm_mix_post[0])
    gm_pre, gm_post = row(norm_mlp_pre[0]), row(norm_mlp_post[0])
    gn_w = row(ret_norm_w[0])
    sinks = swa_sinks[0]

    pm, psm = _proj(xp, g_pre, we_main, we_small, odd=False, cumsum=False, tm=1024, tn=512, seq=seq)
    out_a, ret_p = _ret_prompt(pm, ca_p, sa_p, gn_w, nb=nb, nc=nc)
    out_b, swak_p, swav_p = _swa_prompt(pm, psm, cb_p, sb_p, sinks, nb=nb, nc=nc)
    xp = _outproj(out_a, out_b, xp, wo_a, wo_b, g_post, tm=512)
    xp = _mlp(xp, gm_pre, gm_post, wu0, wd0, tm=512, tf=512)

    sm, ssm_ = _proj(xs, g_pre, we_main, we_small, odd=False, cumsum=False, tm=ms, tn=512, seq=seq)
    mix_s, ret_s, swak_s, swav_s = _even_sample(
        sm, ssm_, state_ret[0].reshape(ms, 4, 128, 128), cache_swa_k[0].reshape(ms, WINDOW, 128),
        cache_swa_v[0].reshape(ms, WINDOW, 128), ca_s, sa_s, cb_s, sb_s, gn_w, sinks)
    mix_s = mix_s.reshape(ms, 2048).astype(BF16)
    xs = _outproj(mix_s[:, :1024], mix_s[:, 1024:], xs, wo_a, wo_b, g_post, tm=ms)
    xs = _mlp(xs, gm_pre, gm_post, wu0, wd0, tm=ms, tf=512)

    wod = w_in_odd[0]
    wod_main = jnp.concatenate([wod[:, :3072], wod[:, 3080:5640]], axis=1).astype(BF16)
    wod_small = jnp.concatenate(
        [wod[:, 3072:3080], wod[:, 5640:5656], jnp.zeros((D_MODEL, ODD_SMALL - 24), F32)], axis=1).astype(BF16)
    wo1 = w_out_odd[0].astype(BF16)
    wo_c, wo_d = wo1[:1024], wo1[1024:]
    wu1, wd1 = w_up[1].astype(BF16), w_down[1].astype(BF16)
    g_pre, g_post = row(norm_mix_pre[1]), row(norm_mix_post[1])
    gm_pre, gm_post = row(norm_mlp_pre[1]), row(norm_mlp_post[1])
    fb = _pad_lanes(fox_fb[0], 0)
    dtb = _pad_lanes(dt_bias[0], DT_LANE)
    alog = _pad_lanes(a_log[0], DT_LANE)
    cw = conv_w[0]
    cwx, cwbc = cw[:, :DI_D], cw[:, DI_D:]
    cbx, cbbc = row(conv_b[0][:DI_D]), row(conv_b[0][DI_D:])
    dskip = row(jnp.repeat(d_skip[0], HD_D))
    nw = row(ssd_norm_w[0])

    pm, psm, lf_p, fc_p = _proj(xp, g_pre, wod_main, wod_small, fb, odd=True, cumsum=True, tm=1024, tn=512, seq=seq)
    tq = 256
    fc = fc_p[:, :H_C].reshape(nb, seq, H_C).transpose(0, 2, 1)
    out_c = _fox_prompt(pm, fc[..., None], fc.reshape(nb, H_C, seq // tq, 1, tq), nb=nb, seq=seq, tq=tq)
    out_d, ssm_pairs_p, conv_p = _ssd_prompt(pm, psm, cwx, cwbc, cbx, cbbc, dtb, alog, dskip, nw, nb=nb, nc=nc)
    fox_k_p = pm[:, 1024:2048].reshape(1, nb, seq, H_C, HD_C)
    fox_v_p = pm[:, 2048:3072].reshape(1, nb, seq, H_C, HD_C)
    fox_lf_p = lf_p[:, :H_C].reshape(1, nb, seq, H_C)
    xp = _outproj(out_c, out_d, xp, wo_c, wo_d, g_post, tm=512)
    xp = _mlp(xp, gm_pre, gm_post, wu1, wd1, tm=512, tf=512)

    sm, ssm_, lf_s = _proj(xs, g_pre, wod_main, wod_small, fb, odd=True, cumsum=False, tm=ms, tn=512, seq=seq)
    q_s = sm[:, 0:1024].reshape(ms, H_C, HD_C)
    k_s = sm[:, 1024:2048].reshape(ms, H_C, HD_C)
    v_s = sm[:, 2048:3072].reshape(ms, H_C, HD_C)
    lf_s8 = lf_s[:, :H_C]
    out_c_s = _fox_decode(page_table, q_s, k_s, v_s, lf_s8.reshape(ms, H_C, 1),
                          cache_fox_k[0], cache_fox_v[0], cache_fox_logf[0], pp=8)
    out_d_s, ssm_pairs_s, conv_s = _ssd_sample(sm, ssm_, state_conv[0], _heads_to_pairs(state_ssm[0]),
                                               cwx, cwbc, cbx, cbbc, dtb, alog, dskip, nw)
    xs = _outproj(out_c_s.reshape(ms, 1024).astype(BF16), out_d_s.reshape(ms, DI_D).astype(BF16),
                  xs, wo_c, wo_d, g_post, tm=ms)
    xs = _mlp(xs, gm_pre, gm_post, wu1, wd1, tm=ms, tf=512)

    return (
        xp.reshape(nb, seq, D_MODEL), xs.reshape(ms, 1, D_MODEL),
        ret_p.reshape(1, nb, H_A, DK_A, DV_A), ret_s.reshape(1, ms, H_A, DK_A, DV_A),
        swak_p.reshape(1, nb, WINDOW, KV_B, HD_B), swav_p.reshape(1, nb, WINDOW, KV_B, HD_B),
        swak_s.reshape(1, ms, WINDOW, KV_B, HD_B), swav_s.reshape(1, ms, WINDOW, KV_B, HD_B),
        fox_k_p, fox_v_p, fox_lf_p,
        k_s.reshape(1, ms, 1, H_C, HD_C), v_s.reshape(1, ms, 1, H_C, HD_C), lf_s8.reshape(1, ms, 1, H_C),
        _pairs_to_heads(ssm_pairs_p)[None], _pairs_to_heads(ssm_pairs_s)[None],
        conv_p[None], conv_s[None],
    )
```

```python
import functools

import numpy as np
import jax
import jax.numpy as jnp
from jax import lax
from jax.experimental import pallas as pl
from jax.experimental.pallas import tpu as pltpu

F32 = jnp.float32
BF16 = jnp.bfloat16

D_MODEL = 2048
BATCH = 4
SEQ = 2048
DEC_BATCH = 32
PAST_LEN = 16384
PAGE_SIZE = 128
D_FF = 4 * D_MODEL
EPS = 1e-6
GN_EPS = 1e-5
CHUNK = 128

H_A, DK_A, DV_A = 8, 64, 128
RET_THETA = 10000.0
H_B, KV_B, HD_B = 16, 2, 64
WINDOW = 128
ROPE_THETA_B = 150000.0
H_C, HD_C = 8, 128
H_D, HD_D, G_D, N_D = 16, 64, 2, 128
CONV_W = 4
DI_D = H_D * HD_D
CONV_CH = DI_D + 2 * G_D * N_D

EVEN_MAIN = 4096
EVEN_SMALL = 256
ODD_MAIN = 5632
ODD_SMALL = 128
DT_LANE = 8

NEG = -1e30
VMEM_LIMIT = 56 * 1024 * 1024
MLP_VMEM_LIMIT = 60 * 1024 * 1024

_LOG_GAMMA = [float(v) for v in np.log1p(-np.exp2(-5.0 - np.arange(H_A, dtype=np.float64)))]


def _cparams(sem, vmem_limit=VMEM_LIMIT):
    return pltpu.CompilerParams(dimension_semantics=sem, vmem_limit_bytes=vmem_limit)


def _silu(x):
    return x * jax.nn.sigmoid(x)


def _softplus(x):
    return jnp.maximum(x, 0.0) + jnp.log1p(jnp.exp(-jnp.abs(x)))


def _log_sigmoid(x):
    return jnp.minimum(x, 0.0) - jnp.log1p(jnp.exp(-jnp.abs(x)))


def _rms(x, g):
    ms = jnp.mean(x * x, axis=-1, keepdims=True)
    return (x * lax.rsqrt(ms + EPS)) * g


def _dot(a, b):
    return jnp.dot(a, b, preferred_element_type=F32)


def _dot_nt(a, b):
    return lax.dot_general(a, b, (((1,), (1,)), ((), ())), preferred_element_type=F32)


def _split3(x):
    hi = x.astype(BF16)
    r = x - hi.astype(F32)
    mid = r.astype(BF16)
    lo = (r - mid.astype(F32)).astype(BF16)
    return hi, mid, lo


def _exact_left01(m01, x):
    hi, mid, lo = _split3(x)
    return _dot(m01, hi) + _dot(m01, mid) + _dot(m01, lo)


def _exact_right01(x, m01):
    hi, mid, lo = _split3(x)
    return _dot(hi, m01) + _dot(mid, m01) + _dot(lo, m01)


def _tri_lower(n):
    r = lax.broadcasted_iota(jnp.int32, (n, n), 0)
    c = lax.broadcasted_iota(jnp.int32, (n, n), 1)
    return jnp.where(r >= c, 1.0, 0.0).astype(BF16)


def _rope64(x, c, s):
    w = x.shape[-1]
    ax = x.ndim - 1
    lane = lax.broadcasted_iota(jnp.int32, x.shape, ax)
    first = (lane & 32) == 0
    left = pltpu.roll(x, w - 32, axis=ax)
    right = pltpu.roll(x, 32, axis=ax)
    return x * c + jnp.where(first, left, right) * s


def _rope_tables(pos, theta, reps):
    inv = 1.0 / (theta ** (jnp.arange(32, dtype=F32) * (2.0 / 64)))
    ang = pos.astype(F32)[:, None] * inv[None, :]
    cos, sin = jnp.cos(ang), jnp.sin(ang)
    c = jnp.concatenate([cos, cos], axis=-1)
    s = jnp.concatenate([-sin, sin], axis=-1)
    return jnp.tile(c, (1, reps)), jnp.tile(s, (1, reps))


def _proj_kernel(*refs, odd, tm, seq):
    if odd:
        (x_ref, xs_ref, g_ref, w_ref, wsm_ref, fb_ref,
         o_ref, osm_ref, lf_ref, fc_ref, os_ref, ossm_ref, lfs_ref, h_scr, hs_scr, carry_scr) = refs
    else:
        (x_ref, xs_ref, g_ref, w_ref, wsm_ref, o_ref, osm_ref, os_ref, ossm_ref, h_scr, hs_scr) = refs
    i = pl.program_id(0)
    j = pl.program_id(1)

    @pl.when(j == 0)
    def _():
        hb = _rms(x_ref[...], g_ref[...]).astype(BF16)
        h_scr[...] = hb
        sm = _dot(hb, wsm_ref[...])
        osm_ref[...] = sm
        if odd:
            lf = _log_sigmoid(sm + fb_ref[...])
            lf_ref[...] = lf
            blk = min(tm, 256)
            tri = _tri_lower(blk)

            @pl.when((i * tm) % seq == 0)
            def _():
                carry_scr[...] = jnp.zeros_like(carry_scr)

            carry = carry_scr[...]
            for r in range(tm // blk):
                f = _exact_left01(tri, lf[r * blk:(r + 1) * blk]) + carry
                fc_ref[r * blk:(r + 1) * blk, :] = f
                carry = f[blk - 1:blk, :]
            carry_scr[...] = carry

    @pl.when((i == 0) & (j == 0))
    def _():
        hsb = _rms(xs_ref[...], g_ref[...]).astype(BF16)
        hs_scr[...] = hsb
        sms = _dot(hsb, wsm_ref[...])
        ossm_ref[...] = sms
        if odd:
            lfs_ref[...] = _log_sigmoid(sms + fb_ref[...])

    w = w_ref[...]
    o_ref[...] = _dot(h_scr[...], w)

    @pl.when(i == 0)
    def _():
        os_ref[...] = _dot(hs_scr[...], w)


def _proj(x, xs, g, w_main, w_small, fb=None, *, odd, tm, tn, seq):
    m, ms = x.shape[0], xs.shape[0]
    n_main = w_main.shape[1]
    n_small = w_small.shape[1]
    nj = n_main // tn
    grid = (m // tm, nj)
    c2 = lambda i, j: (0, 0)
    s_tile = lambda i, j: (0, jnp.where(i == 0, j, nj - 1))
    in_specs = [
        pl.BlockSpec((tm, D_MODEL), lambda i, j: (i, 0)),
        pl.BlockSpec((ms, D_MODEL), c2),
        pl.BlockSpec((1, D_MODEL), c2),
        pl.BlockSpec((D_MODEL, tn), lambda i, j: (0, j)),
        pl.BlockSpec((D_MODEL, n_small), c2),
    ]
    args = [x, xs, g, w_main, w_small]
    small = jax.ShapeDtypeStruct((m, n_small), F32)
    small_s = jax.ShapeDtypeStruct((ms, n_small), F32)
    small_spec = pl.BlockSpec((tm, n_small), lambda i, j: (i, 0))
    small_s_spec = pl.BlockSpec((ms, n_small), c2)
    out_shape = [jax.ShapeDtypeStruct((m, n_main), F32), small]
    out_specs = [pl.BlockSpec((tm, tn), lambda i, j: (i, j)), small_spec]
    scratch = [pltpu.VMEM((tm, D_MODEL), BF16), pltpu.VMEM((ms, D_MODEL), BF16)]
    if odd:
        in_specs.append(pl.BlockSpec((1, n_small), c2))
        args.append(fb)
        out_shape += [small, small]
        out_specs += [small_spec, small_spec]
        scratch.append(pltpu.VMEM((1, n_small), F32))
    out_shape += [jax.ShapeDtypeStruct((ms, n_main), F32), small_s]
    out_specs += [pl.BlockSpec((ms, tn), s_tile), small_s_spec]
    if odd:
        out_shape.append(small_s)
        out_specs.append(small_s_spec)
    return pl.pallas_call(
        functools.partial(_proj_kernel, odd=odd, tm=tm, seq=seq),
        grid=grid, in_specs=in_specs, out_specs=out_specs, out_shape=out_shape,
        scratch_shapes=scratch, compiler_params=_cparams(("arbitrary", "arbitrary")),
        name="proj_odd" if odd else "proj_even",
    )(*args)


def _outproj_kernel(a_ref, b_ref, x_ref, as_ref, bs_ref, xs_ref, wa_ref, wb_ref, g_ref, o_ref, os_ref):
    wa, wb, g = wa_ref[...], wb_ref[...], g_ref[...]
    y = _dot(a_ref[...], wa) + _dot(b_ref[...], wb)
    o_ref[...] = x_ref[...] + _rms(y, g)

    @pl.when(pl.program_id(0) == 0)
    def _():
        ys = _dot(as_ref[...], wa) + _dot(bs_ref[...], wb)
        os_ref[...] = xs_ref[...] + _rms(ys, g)


def _outproj(a, b, x, a_s, b_s, xs, wa, wb, g, *, tm):
    m, ms = x.shape[0], xs.shape[0]
    ka, kb = a.shape[1], b.shape[1]
    c2 = lambda i: (0, 0)
    return pl.pallas_call(
        _outproj_kernel,
        grid=(m // tm,),
        in_specs=[
            pl.BlockSpec((tm, ka), lambda i: (i, 0)),
            pl.BlockSpec((tm, kb), lambda i: (i, 0)),
            pl.BlockSpec((tm, D_MODEL), lambda i: (i, 0)),
            pl.BlockSpec((ms, ka), c2),
            pl.BlockSpec((ms, kb), c2),
            pl.BlockSpec((ms, D_MODEL), c2),
            pl.BlockSpec((ka, D_MODEL), c2),
            pl.BlockSpec((kb, D_MODEL), c2),
            pl.BlockSpec((1, D_MODEL), c2),
        ],
        out_specs=[pl.BlockSpec((tm, D_MODEL), lambda i: (i, 0)), pl.BlockSpec((ms, D_MODEL), c2)],
        out_shape=[jax.ShapeDtypeStruct((m, D_MODEL), F32), jax.ShapeDtypeStruct((ms, D_MODEL), F32)],
        compiler_params=_cparams(("arbitrary",)),
        name="outproj",
    )(a, b, x, a_s, b_s, xs, wa, wb, g)


def _mlp_kernel(x_ref, xs_ref, gpre_ref, gpost_ref, wu_ref, wd_ref, o_ref, os_ref, h_scr, hs_scr):
    i = pl.program_id(0)
    j = pl.program_id(1)
    last = pl.num_programs(1) - 1
    wu = wu_ref[...].astype(BF16)
    wd = wd_ref[...].astype(BF16)

    def group(x_r, o_r, h_s):
        @pl.when(j == 0)
        def _():
            h_s[...] = _rms(x_r[...], gpre_ref[...]).astype(BF16)

        u = jnp.maximum(_dot(h_s[...], wu), 0.0)
        uu = (u * u).astype(BF16)

        @pl.when(j == 0)
        def _():
            o_r[...] = _dot(uu, wd)

        @pl.when(j > 0)
        def _():
            o_r[...] += _dot(uu, wd)

        @pl.when(j == last)
        def _():
            o_r[...] = x_r[...] + _rms(o_r[...], gpost_ref[...])

    group(x_ref, o_ref, h_scr)

    @pl.when(i == 0)
    def _():
        group(xs_ref, os_ref, hs_scr)


def _mlp(x, xs, gpre, gpost, w_up, w_down, layer, *, tm, tf):
    m, ms = x.shape[0], xs.shape[0]
    c2 = lambda i, j: (0, 0)
    return pl.pallas_call(
        _mlp_kernel,
        grid=(m // tm, D_FF // tf),
        in_specs=[
            pl.BlockSpec((tm, D_MODEL), lambda i, j: (i, 0), pipeline_mode=pl.Buffered(1)),
            pl.BlockSpec((ms, D_MODEL), c2),
            pl.BlockSpec((1, D_MODEL), c2),
            pl.BlockSpec((1, D_MODEL), c2),
            pl.BlockSpec((None, D_MODEL, tf), lambda i, j: (layer, 0, j)),
            pl.BlockSpec((None, tf, D_MODEL), lambda i, j: (layer, j, 0)),
        ],
        out_specs=[pl.BlockSpec((tm, D_MODEL), lambda i, j: (i, 0)), pl.BlockSpec((ms, D_MODEL), c2)],
        out_shape=[jax.ShapeDtypeStruct((m, D_MODEL), F32), jax.ShapeDtypeStruct((ms, D_MODEL), F32)],
        scratch_shapes=[pltpu.VMEM((tm, D_MODEL), BF16), pltpu.VMEM((ms, D_MODEL), BF16)],
        compiler_params=_cparams(("arbitrary", "arbitrary"), MLP_VMEM_LIMIT),
        name="mlp",
    )(x, xs, gpre, gpost, w_up, w_down)


def _group_norm_gate(y, gate, gw):
    mu = jnp.mean(y, axis=-1, keepdims=True)
    d = y - mu
    var = jnp.mean(d * d, axis=-1, keepdims=True)
    return _silu(gate) * (d * lax.rsqrt(var + GN_EPS) * gw)


def _ret_prompt_kernel(q_ref, k_ref, v_ref, g_ref, cos_ref, sin_ref, gn_ref, oa_ref, st_ref,
                       s_scr, d_scr, e_scr, t_scr):
    b = pl.program_id(0)
    c = pl.program_id(1)
    t_i = lax.broadcasted_iota(jnp.int32, (CHUNK, CHUNK), 0)
    s_i = lax.broadcasted_iota(jnp.int32, (CHUNK, CHUNK), 1)
    lo = s_i < 64

    @pl.when((b == 0) & (c == 0))
    def _():
        tf = t_i.astype(F32)
        sf = s_i.astype(F32)
        for h in range(H_A):
            d_scr[h] = jnp.where(t_i >= s_i, jnp.exp((tf - sf) * _LOG_GAMMA[h]), 0.0)
            e_scr[h] = jnp.exp((tf + 1.0) * _LOG_GAMMA[h])
        for p in range(H_A // 2):
            lg = jnp.where(lo, _LOG_GAMMA[2 * p], _LOG_GAMMA[2 * p + 1])
            t_scr[p] = jnp.exp((CHUNK - 1.0 - tf) * lg)

    @pl.when(c == 0)
    def _():
        s_scr[...] = jnp.zeros_like(s_scr)

    cos, sin = cos_ref[...], sin_ref[...]
    qr = _rope64(q_ref[...], cos, sin)
    kr = _rope64(k_ref[...], cos, sin) * (DK_A ** -0.5)
    top = t_i < 64
    for p in range(H_A // 2):
        sl = slice(128 * p, 128 * (p + 1))
        qp, kp = qr[:, sl], kr[:, sl]
        kb = kp.astype(BF16)
        s_old = s_scr[p]
        s_old_b = s_old.astype(BF16)
        ktt = jnp.transpose(kp * t_scr[p]).astype(BF16)
        upd = []
        for e in range(2):
            h = 2 * p + e
            hs = slice(128 * h, 128 * (h + 1))
            qm = jnp.where(lo if e == 0 else ~lo, qp, 0.0).astype(BF16)
            vh = v_ref[:, hs].astype(BF16)
            att = (_dot_nt(qm, kb) * d_scr[h]).astype(BF16)
            y = _dot(att, vh) + _dot(qm, s_old_b) * e_scr[h]
            oa_ref[:, hs] = _group_norm_gate(y, g_ref[:, hs], gn_ref[:, hs]).astype(BF16)
            upd.append(_dot(ktt, vh))
        g128 = jnp.where(top, float(np.exp(CHUNK * _LOG_GAMMA[2 * p])), float(np.exp(CHUNK * _LOG_GAMMA[2 * p + 1])))
        s_scr[p] = g128 * s_old + jnp.where(top, upd[0], upd[1])

    @pl.when(c == pl.num_programs(1) - 1)
    def _():
        st_ref[...] = s_scr[...]


def _ret_prompt(p_main, cos, sin, gn_w, *, nb, nc):
    m = nb * nc * CHUNK
    row = lambda b, c: b * nc + c
    return pl.pallas_call(
        _ret_prompt_kernel,
        grid=(nb, nc),
        in_specs=[
            pl.BlockSpec((CHUNK, 512), lambda b, c: (row(b, c), 0)),
            pl.BlockSpec((CHUNK, 512), lambda b, c: (row(b, c), 1)),
            pl.BlockSpec((CHUNK, 1024), lambda b, c: (row(b, c), 1)),
            pl.BlockSpec((CHUNK, 1024), lambda b, c: (row(b, c), 2)),
            pl.BlockSpec((CHUNK, 512), lambda b, c: (c, 0)),
            pl.BlockSpec((CHUNK, 512), lambda b, c: (c, 0)),
            pl.BlockSpec((1, 1024), lambda b, c: (0, 0)),
        ],
        out_specs=[
            pl.BlockSpec((CHUNK, 1024), lambda b, c: (row(b, c), 0)),
            pl.BlockSpec((None, 4, 128, 128), lambda b, c: (b, 0, 0, 0)),
        ],
        out_shape=[jax.ShapeDtypeStruct((m, 1024), BF16), jax.ShapeDtypeStruct((nb, 4, 128, 128), F32)],
        scratch_shapes=[pltpu.VMEM((4, 128, 128), F32), pltpu.VMEM((H_A, 128, 128), F32),
                        pltpu.VMEM((H_A, 128, 128), F32), pltpu.VMEM((4, 128, 128), F32)],
        compiler_params=_cparams(("arbitrary", "arbitrary")),
        name="ret_prompt",
    )(p_main, p_main, p_main, p_main, cos, sin, gn_w)


def _swa_padded(x2):
    lane = lax.broadcasted_iota(jnp.int32, x2.shape, 1)
    lo = lane < 64
    xr = pltpu.roll(x2, 64, axis=1)
    z = jnp.zeros_like(x2)
    return {
        (0, 0): jnp.where(lo, x2, z).astype(BF16), (0, 1): jnp.where(lo, z, xr).astype(BF16),
        (1, 0): jnp.where(lo, xr, z).astype(BF16), (1, 1): jnp.where(lo, z, x2).astype(BF16),
    }


def _swa_attend(q_pair_fn, kpad, vpad, valid, sink_ref, store_fn, rows):
    per_group = (H_B // KV_B) // 2
    for g in range(KV_B):
        pairs = list(range(g * per_group, (g + 1) * per_group))
        qs = jnp.concatenate([q_pair_fn(jj) for jj in pairs], axis=0).astype(BF16)
        acc = None
        for e in range(2):
            s = _dot_nt(qs, kpad[(g, e)])
            if valid is not None:
                s = jnp.where(valid, s, NEG)
            sink = jnp.concatenate([jnp.full((rows, 1), sink_ref[2 * jj + e], F32) for jj in pairs], axis=0)
            mx = jnp.maximum(jnp.max(s, axis=-1, keepdims=True), sink)
            pr = jnp.exp(s - mx)
            den = jnp.sum(pr, axis=-1, keepdims=True) + jnp.exp(sink - mx)
            o = _dot(pr.astype(BF16), vpad[(g, e)]) / den
            acc = o if acc is None else acc + o
        for t, jj in enumerate(pairs):
            store_fn(jj, acc[t * rows:(t + 1) * rows])


def _swa_prompt_kernel(sink_ref, q_ref, kvc_ref, kvp_ref, cc_ref, sc_ref, cp_ref, sp_ref,
                       ob_ref, ko_ref, vo_ref):
    n = pl.program_id(1)
    cc, sc = cc_ref[...], sc_ref[...]
    kvc, kvp = kvc_ref[...], kvp_ref[...]
    kc = _rope64(kvc[:, :128], cc, sc)
    kp = _rope64(kvp[:, :128], cp_ref[...], sp_ref[...])
    vc = kvc[:, 128:]
    kpad = _swa_padded(jnp.concatenate([kp, kc], axis=0))
    vpad = _swa_padded(jnp.concatenate([kvp[:, 128:], vc], axis=0))
    stacked = WINDOW * (H_B // KV_B) // 2
    i = lax.broadcasted_iota(jnp.int32, (stacked, 2 * WINDOW), 0) & (WINDOW - 1)
    j = lax.broadcasted_iota(jnp.int32, (stacked, 2 * WINDOW), 1)
    valid = (j >= i + 1) & (j <= i + WINDOW) & ((n > 0) | (j >= WINDOW))

    def q_pair(jj):
        return _rope64(q_ref[:, 128 * jj:128 * (jj + 1)], cc, sc) * (HD_B ** -0.5)

    def store(jj, acc):
        ob_ref[:, 128 * jj:128 * (jj + 1)] = acc.astype(BF16)

    _swa_attend(q_pair, kpad, vpad, valid, sink_ref, store, WINDOW)

    @pl.when(n == pl.num_programs(1) - 1)
    def _():
        ko_ref[...] = kc
        vo_ref[...] = vc


def _swa_prompt(p_main, p_small, cos, sin, sinks, *, nb, nc):
    m = nb * nc * CHUNK
    row = lambda b, n: b * nc + n
    prev = lambda b, n: b * nc + jnp.maximum(n - 1, 0)
    return pl.pallas_call(
        _swa_prompt_kernel,
        grid=(nb, nc),
        in_specs=[
            pl.BlockSpec(memory_space=pltpu.SMEM),
            pl.BlockSpec((CHUNK, 1024), lambda b, n: (row(b, n), 3)),
            pl.BlockSpec((CHUNK, 256), lambda b, n: (row(b, n), 0)),
            pl.BlockSpec((CHUNK, 256), lambda b, n: (prev(b, n), 0)),
            pl.BlockSpec((CHUNK, 128), lambda b, n: (n, 0)),
            pl.BlockSpec((CHUNK, 128), lambda b, n: (n, 0)),
            pl.BlockSpec((CHUNK, 128), lambda b, n: (jnp.maximum(n - 1, 0), 0)),
            pl.BlockSpec((CHUNK, 128), lambda b, n: (jnp.maximum(n - 1, 0), 0)),
        ],
        out_specs=[
            pl.BlockSpec((CHUNK, 1024), lambda b, n: (row(b, n), 0)),
            pl.BlockSpec((None, WINDOW, 128), lambda b, n: (b, 0, 0)),
            pl.BlockSpec((None, WINDOW, 128), lambda b, n: (b, 0, 0)),
        ],
        out_shape=[jax.ShapeDtypeStruct((m, 1024), BF16), jax.ShapeDtypeStruct((nb, WINDOW, 128), F32),
                   jax.ShapeDtypeStruct((nb, WINDOW, 128), F32)],
        compiler_params=_cparams(("arbitrary", "arbitrary")),
        name="swa_prompt",
    )(sinks, p_main, p_small, p_small, cos, sin, cos, sin)


def _col_bcast(row128):
    return jnp.transpose(jnp.broadcast_to(row128, (128, 128)))


def _even_sample_kernel(sink_ref, p_ref, ps_ref, s0_ref, kbuf_ref, vbuf_ref, ca_ref, sa_ref, cb_ref, sb_ref,
                        gn_ref, mix_ref, st_ref, ko_ref, vo_ref):
    row = p_ref[...]
    ca, sa = ca_ref[...], sa_ref[...]
    qr = _rope64(jnp.broadcast_to(row[:, 0:512], (8, 512)), ca, sa)
    kr = _rope64(jnp.broadcast_to(row[:, 512:1024], (8, 512)), ca, sa) * (DK_A ** -0.5)
    r_i = lax.broadcasted_iota(jnp.int32, (128, 128), 0)
    top = r_i < 64
    for p in range(H_A // 2):
        sl = slice(128 * p, 128 * (p + 1))
        kcol = _col_bcast(kr[0:1, sl])
        qcol = _col_bcast(qr[0:1, sl])
        he, ho = 2 * p, 2 * p + 1
        v_e = row[:, 1024 + 128 * he:1024 + 128 * (he + 1)]
        v_o = row[:, 1024 + 128 * ho:1024 + 128 * (ho + 1)]
        gam = jnp.where(top, float(np.exp(_LOG_GAMMA[he])), float(np.exp(_LOG_GAMMA[ho])))
        s_new = gam * s0_ref[p] + kcol * jnp.where(top, v_e, v_o)
        st_ref[p] = s_new
        prod = qcol * s_new
        for e, h in ((0, he), (1, ho)):
            y = jnp.sum(jnp.where(top if e == 0 else ~top, prod, 0.0), axis=0, keepdims=True)
            hs = slice(128 * h, 128 * (h + 1))
            gate = row[:, 2048 + 128 * h:2048 + 128 * (h + 1)]
            mix_ref[:, hs] = _group_norm_gate(y, gate, gn_ref[:, hs])

    cb, sb = cb_ref[...], sb_ref[...]
    ps = ps_ref[...]
    kn = _rope64(jnp.broadcast_to(ps[:, 0:128], (8, 128)), cb, sb)[0:1]
    vn = ps[:, 128:256]
    last = r_i == WINDOW - 1
    k_new = jnp.where(last, kn, pltpu.roll(kbuf_ref[...], WINDOW - 1, axis=0))
    v_new = jnp.where(last, vn, pltpu.roll(vbuf_ref[...], WINDOW - 1, axis=0))
    ko_ref[...] = k_new
    vo_ref[...] = v_new
    kpad = _swa_padded(k_new)
    vpad = _swa_padded(v_new)

    def q_pair(jj):
        q8 = jnp.broadcast_to(row[:, 3072 + 128 * jj:3072 + 128 * (jj + 1)], (8, 128))
        return _rope64(q8, cb, sb) * (HD_B ** -0.5)

    def store(jj, acc):
        mix_ref[:, 1024 + 128 * jj:1024 + 128 * (jj + 1)] = acc[0:1]

    _swa_attend(q_pair, kpad, vpad, None, sink_ref, store, 8)


def _even_sample(p_main, p_small, s0, kbuf, vbuf, ca, sa, cb, sb, gn_w, sinks):
    nb = p_main.shape[0]
    b3 = lambda b: (b, 0, 0)
    b4 = lambda b: (b, 0, 0, 0)
    c2 = lambda b: (0, 0)
    return pl.pallas_call(
        _even_sample_kernel,
        grid=(nb,),
        in_specs=[
            pl.BlockSpec(memory_space=pltpu.SMEM),
            pl.BlockSpec((None, 1, EVEN_MAIN), b3),
            pl.BlockSpec((None, 1, EVEN_SMALL), b3),
            pl.BlockSpec((None, 4, 128, 128), b4),
            pl.BlockSpec((None, WINDOW, 128), b3),
            pl.BlockSpec((None, WINDOW, 128), b3),
            pl.BlockSpec((1, 512), c2), pl.BlockSpec((1, 512), c2),
            pl.BlockSpec((1, 128), c2), pl.BlockSpec((1, 128), c2),
            pl.BlockSpec((1, 1024), c2),
        ],
        out_specs=[
            pl.BlockSpec((None, 1, 2048), b3),
            pl.BlockSpec((None, 4, 128, 128), b4),
            pl.BlockSpec((None, WINDOW, 128), b3),
            pl.BlockSpec((None, WINDOW, 128), b3),
        ],
        out_shape=[jax.ShapeDtypeStruct((nb, 1, 2048), F32), jax.ShapeDtypeStruct((nb, 4, 128, 128), F32),
                   jax.ShapeDtypeStruct((nb, WINDOW, 128), F32), jax.ShapeDtypeStruct((nb, WINDOW, 128), F32)],
        compiler_params=_cparams(("arbitrary",)),
        name="even_sample",
    )(sinks, p_main.reshape(nb, 1, EVEN_MAIN), p_small.reshape(nb, 1, EVEN_SMALL), s0, kbuf, vbuf,
      ca, sa, cb, sb, gn_w)


def _fox_prompt_kernel(q_ref, k_ref, v_ref, fcol_ref, frow_ref, o_ref, kb_scr, vb_scr, *, seq, tq):
    kb_scr[...] = k_ref[...].astype(BF16)
    vb_scr[...] = v_ref[...].astype(BF16)
    r_i = lax.broadcasted_iota(jnp.int32, (tq, tq), 0)
    c_i = lax.broadcasted_iota(jnp.int32, (tq, tq), 1)
    for qi in range(seq // tq):
        rows = slice(qi * tq, (qi + 1) * tq)
        q = (q_ref[rows, :] * (HD_C ** -0.5)).astype(BF16)
        fq = fcol_ref[rows, :]
        m = l = acc = None
        for j in range(qi + 1):
            cols = slice(j * tq, (j + 1) * tq)
            s = _dot_nt(q, kb_scr[cols, :]) + (fq - frow_ref[:, cols])
            if j == qi:
                s = jnp.where(c_i <= r_i, s, NEG)
            mx = jnp.max(s, axis=-1, keepdims=True)
            if j == 0:
                m = mx
                p = jnp.exp(s - m)
                l = jnp.sum(p, axis=-1, keepdims=True)
                acc = _dot(p.astype(BF16), vb_scr[cols, :])
            else:
                m_new = jnp.maximum(m, mx)
                a = jnp.exp(m - m_new)
                p = jnp.exp(s - m_new)
                l = a * l + jnp.sum(p, axis=-1, keepdims=True)
                acc = a * acc + _dot(p.astype(BF16), vb_scr[cols, :])
                m = m_new
        o_ref[rows, :] = (acc / l).astype(BF16)


def _fox_prompt(p_main, fcol, frow, *, nb, seq, tq):
    m = nb * seq
    return pl.pallas_call(
        functools.partial(_fox_prompt_kernel, seq=seq, tq=tq),
        grid=(nb, H_C),
        in_specs=[
            pl.BlockSpec((seq, HD_C), lambda b, h: (b, h)),
            pl.BlockSpec((seq, HD_C), lambda b, h: (b, H_C + h)),
            pl.BlockSpec((seq, HD_C), lambda b, h: (b, 2 * H_C + h)),
            pl.BlockSpec((None, None, seq, 1), lambda b, h: (b, h, 0, 0)),
            pl.BlockSpec((None, None, 1, seq), lambda b, h: (b, h, 0, 0)),
        ],
        out_specs=pl.BlockSpec((seq, HD_C), lambda b, h: (b, h)),
        out_shape=jax.ShapeDtypeStruct((m, H_C * HD_C), BF16),
        scratch_shapes=[pltpu.VMEM((seq, HD_C), BF16), pltpu.VMEM((seq, HD_C), BF16)],
        compiler_params=_cparams(("arbitrary", "arbitrary")),
        name="fox_prompt",
    )(p_main, p_main, p_main, fcol, frow)


def _fox_decode_kernel(*refs, pp):
    pt_ref = refs[0]
    q_ref, kn_ref, vn_ref, lfn_ref = refs[1:5]
    k_refs = refs[5:5 + pp]
    v_refs = refs[5 + pp:5 + 2 * pp]
    lf_refs = refs[5 + 2 * pp:5 + 3 * pp]
    o_ref = refs[5 + 3 * pp]
    m_scr, l_scr, acc_scr, carry_scr, rexp_scr, tri_scr = refs[6 + 3 * pp:]
    del pt_ref
    b = pl.program_id(0)
    s_id = pl.program_id(1)
    rows = PAGE_SIZE * H_C

    @pl.when((b == 0) & (s_id == 0))
    def _():
        r = lax.broadcasted_iota(jnp.int32, (PAGE_SIZE, rows), 0)
        c = lax.broadcasted_iota(jnp.int32, (PAGE_SIZE, rows), 1)
        rexp_scr[...] = jnp.where((c >> 3) == r, 1.0, 0.0).astype(BF16)
        r2 = lax.broadcasted_iota(jnp.int32, (PAGE_SIZE, PAGE_SIZE), 0)
        c2 = lax.broadcasted_iota(jnp.int32, (PAGE_SIZE, PAGE_SIZE), 1)
        tri_scr[...] = jnp.where(r2 <= c2, 1.0, 0.0).astype(BF16)

    @pl.when(s_id == 0)
    def _():
        m_scr[...] = jnp.full_like(m_scr, NEG)
        l_scr[...] = jnp.zeros_like(l_scr)
        acc_scr[...] = jnp.zeros_like(acc_scr)
        carry_scr[...] = jnp.zeros_like(carry_scr)

    qs = q_ref[...] * (HD_C ** -0.5)
    qb = qs.astype(BF16)
    h_i = lax.broadcasted_iota(jnp.int32, (H_C, rows), 0)
    c_i = lax.broadcasted_iota(jnp.int32, (H_C, rows), 1)
    diag = (c_i & (H_C - 1)) == h_i
    m, l, acc, carry = m_scr[...], l_scr[...], acc_scr[...], carry_scr[...]

    lf_all = jnp.concatenate([lf_refs[r][...] for r in range(pp)], axis=0)
    cum_in = _exact_right01(lf_all, tri_scr[...])
    offs = []
    for r in range(pp):
        offs.append(carry)
        carry = carry + cum_in[H_C * r:H_C * (r + 1), PAGE_SIZE - 1:PAGE_SIZE]
    gexp = _exact_right01(cum_in + jnp.concatenate(offs, axis=0), rexp_scr[...])

    logits = []
    for r in range(pp):
        k2 = k_refs[r][...].reshape(rows, HD_C).astype(BF16)
        logits.append(jnp.where(diag, _dot_nt(qb, k2) - gexp[H_C * r:H_C * (r + 1)], NEG))
    mx = logits[0]
    for r in range(1, pp):
        mx = jnp.maximum(mx, logits[r])
    m_new = jnp.maximum(m, jnp.max(mx, axis=-1, keepdims=True))
    a = jnp.exp(m - m_new)
    psum = pv = None
    for r in range(pp):
        p = jnp.exp(logits[r] - m_new)
        o = _dot(p.astype(BF16), v_refs[r][...].reshape(rows, HD_C).astype(BF16))
        psum = p if psum is None else psum + p
        pv = o if pv is None else pv + o
    l = a * l + jnp.sum(psum, axis=-1, keepdims=True)
    acc = a * acc + pv
    m = m_new
    m_scr[...] = m
    l_scr[...] = l
    acc_scr[...] = acc
    carry_scr[...] = carry

    @pl.when(s_id == pl.num_programs(1) - 1)
    def _():
        fq = carry + lfn_ref[...]
        s_new = jnp.sum(qs * kn_ref[...], axis=-1, keepdims=True)
        m_past = m + fq
        mx = jnp.maximum(m_past, s_new)
        wp = jnp.exp(m_past - mx)
        wn = jnp.exp(s_new - mx)
        o_ref[...] = (acc * wp + wn * vn_ref[...]) / (l * wp + wn)


def _fox_decode(page_table, q, k_new, v_new, lf_new, cache_k, cache_v, cache_lf, *, pp):
    nb, n_pages = page_table.shape
    b3 = lambda b, s, pt: (b, 0, 0)
    kv_spec = lambda r: pl.BlockSpec((None, PAGE_SIZE, H_C, HD_C), lambda b, s, pt: (pt[b, s * pp + r], 0, 0, 0))
    lf_spec = lambda r: pl.BlockSpec((None, H_C, PAGE_SIZE), lambda b, s, pt: (pt[b, s * pp + r], 0, 0))
    in_specs = [pl.BlockSpec((None, H_C, HD_C), b3), pl.BlockSpec((None, H_C, HD_C), b3),
                pl.BlockSpec((None, H_C, HD_C), b3), pl.BlockSpec((None, H_C, 1), b3)]
    in_specs += [kv_spec(r) for r in range(pp)] + [kv_spec(r) for r in range(pp)] + [lf_spec(r) for r in range(pp)]
    grid_spec = pltpu.PrefetchScalarGridSpec(
        num_scalar_prefetch=1, grid=(nb, n_pages // pp), in_specs=in_specs,
        out_specs=pl.BlockSpec((None, H_C, HD_C), b3),
        scratch_shapes=[pltpu.VMEM((H_C, 1), F32), pltpu.VMEM((H_C, 1), F32), pltpu.VMEM((H_C, HD_C), F32),
                        pltpu.VMEM((H_C, 1), F32), pltpu.VMEM((PAGE_SIZE, PAGE_SIZE * H_C), BF16),
                        pltpu.VMEM((PAGE_SIZE, PAGE_SIZE), BF16)],
    )
    return pl.pallas_call(
        functools.partial(_fox_decode_kernel, pp=pp),
        grid_spec=grid_spec,
        out_shape=jax.ShapeDtypeStruct((nb, H_C, HD_C), F32),
        compiler_params=_cparams(("arbitrary", "arbitrary")),
        name="fox_decode",
    )(page_table, q, k_new, v_new, lf_new, *([cache_k] * pp), *([cache_v] * pp), *([cache_lf] * pp))


def _lane_col(x, lane):
    return x[:, lane:lane + 1]


def _ssd_prompt_kernel(x_ref, bc_ref, z_ref, dt_ref, cwx_ref, cwbc_ref, cbx_ref, cbbc_ref, dtb_ref, alog_ref,
                       dskip_ref, nw_ref, od_ref, st_ref, cv_ref, xpx_scr, xpbc_scr, s_scr, y_scr):
    c = pl.program_id(1)
    nc = pl.num_programs(1)

    @pl.when(c == 0)
    def _():
        xpx_scr[0:8, :] = jnp.zeros((8, DI_D), F32)
        xpbc_scr[0:8, :] = jnp.zeros((8, 512), F32)
        s_scr[...] = jnp.zeros_like(s_scr)

    xpx_scr[8:8 + CHUNK, :] = x_ref[...]
    xpbc_scr[8:8 + CHUNK, :] = bc_ref[...]
    cx = cbx_ref[...]
    cbc = cbbc_ref[...]
    for k in range(CONV_W):
        w = CONV_W - 1 - k
        cx = cx + cwx_ref[w:w + 1, :] * xpx_scr[8 - k:8 - k + CHUNK, :]
        cbc = cbc + cwbc_ref[w:w + 1, :] * xpbc_scr[8 - k:8 - k + CHUNK, :]

    @pl.when(c == nc - 1)
    def _():
        cv_ref[:, 0:DI_D] = xpx_scr[CHUNK + 5:CHUNK + 8, :]
        cv_ref[:, DI_D:CONV_CH] = xpbc_scr[CHUNK + 5:CHUNK + 8, :]

    xpx_scr[0:8, :] = xpx_scr[CHUNK:CHUNK + 8, :]
    xpbc_scr[0:8, :] = xpbc_scr[CHUNK:CHUNK + 8, :]

    xs = _silu(cx)
    bcs = _silu(cbc)
    dt = _softplus(dt_ref[...] + dtb_ref[...])
    la = dt * (-jnp.exp(alog_ref[...]))
    cum = _exact_left01(_tri_lower(CHUNK), la)
    cum_t = jnp.transpose(cum)
    t_i = lax.broadcasted_iota(jnp.int32, (CHUNK, CHUNK), 0)
    s_i = lax.broadcasted_iota(jnp.int32, (CHUNK, CHUNK), 1)
    causal = t_i >= s_i
    lo = s_i < 64
    att_base, bt = [], []
    for g in range(G_D):
        bg = bcs[:, 128 * g:128 * (g + 1)]
        cg = bcs[:, 256 + 128 * g:256 + 128 * (g + 1)]
        att_base.append(_dot_nt(cg.astype(BF16), bg.astype(BF16)))
        bt.append(jnp.transpose(bg).astype(BF16))
    ss = jnp.zeros((CHUNK, 1), F32)
    for p in range(H_D // 2):
        g = (2 * p) // (H_D // G_D)
        sl = slice(128 * p, 128 * (p + 1))
        le, lo_ = DT_LANE + 2 * p, DT_LANE + 2 * p + 1
        dt_pair = jnp.where(lo, _lane_col(dt, le), _lane_col(dt, lo_))
        cum_pair = jnp.where(lo, _lane_col(cum, le), _lane_col(cum, lo_))
        clast = cum_pair[CHUNK - 1:CHUNK, :]
        xs_p = xs[:, sl]
        xdt = xs_p * dt_pair
        s_old = s_scr[p]
        cgb = bcs[:, 256 + 128 * g:256 + 128 * (g + 1)].astype(BF16)
        y = _dot(cgb, s_old.astype(BF16)) * jnp.exp(cum_pair)
        for e in range(2):
            ln = DT_LANE + 2 * p + e
            diff = _lane_col(cum, ln) - cum_t[ln:ln + 1, :]
            att = (att_base[g] * jnp.exp(jnp.where(causal, diff, NEG))).astype(BF16)
            xm = jnp.where(lo if e == 0 else ~lo, xdt, 0.0).astype(BF16)
            y = y + _dot(att, xm)
        y = (y + xs_p * dskip_ref[:, sl]) * _silu(z_ref[:, sl])
        y_scr[:, sl] = y
        ss = ss + jnp.sum(y * y, axis=-1, keepdims=True)
        txdt = (xdt * jnp.exp(clast - cum_pair)).astype(BF16)
        s_scr[p] = jnp.exp(clast) * s_old + _dot(bt[g], txdt)
    inv = lax.rsqrt(ss * (1.0 / DI_D) + EPS)
    od_ref[...] = (y_scr[...] * inv * nw_ref[...]).astype(BF16)

    @pl.when(c == nc - 1)
    def _():
        st_ref[...] = s_scr[...]


def _ssd_prompt(p_main, p_small, cwx, cwbc, cbx, cbbc, dtb, alog, dskip, nw, *, nb, nc):
    m = nb * nc * CHUNK
    row = lambda b, c: b * nc + c
    c2 = lambda b, c: (0, 0)
    return pl.pallas_call(
        _ssd_prompt_kernel,
        grid=(nb, nc),
        in_specs=[
            pl.BlockSpec((CHUNK, 1024), lambda b, c: (row(b, c), 4)),
            pl.BlockSpec((CHUNK, 512), lambda b, c: (row(b, c), 10)),
            pl.BlockSpec((CHUNK, 1024), lambda b, c: (row(b, c), 3)),
            pl.BlockSpec((CHUNK, ODD_SMALL), lambda b, c: (row(b, c), 0)),
            pl.BlockSpec((CONV_W, DI_D), c2), pl.BlockSpec((CONV_W, 512), c2),
            pl.BlockSpec((1, DI_D), c2), pl.BlockSpec((1, 512), c2),
            pl.BlockSpec((1, ODD_SMALL), c2), pl.BlockSpec((1, ODD_SMALL), c2),
            pl.BlockSpec((1, DI_D), c2), pl.BlockSpec((1, DI_D), c2),
        ],
        out_specs=[
            pl.BlockSpec((CHUNK, DI_D), lambda b, c: (row(b, c), 0)),
            pl.BlockSpec((None, 8, 128, 128), lambda b, c: (b, 0, 0, 0)),
            pl.BlockSpec((None, CONV_W - 1, CONV_CH), lambda b, c: (b, 0, 0)),
        ],
        out_shape=[jax.ShapeDtypeStruct((m, DI_D), BF16), jax.ShapeDtypeStruct((nb, 8, 128, 128), F32),
                   jax.ShapeDtypeStruct((nb, CONV_W - 1, CONV_CH), F32)],
        scratch_shapes=[pltpu.VMEM((CHUNK + 8, DI_D), F32), pltpu.VMEM((CHUNK + 8, 512), F32),
                        pltpu.VMEM((8, 128, 128), F32), pltpu.VMEM((CHUNK, DI_D), F32)],
        compiler_params=_cparams(("arbitrary", "arbitrary")),
        name="ssd_prompt",
    )(p_main, p_main, p_main, p_small, cwx, cwbc, cbx, cbbc, dtb, alog, dskip, nw)


def _ssd_sample_kernel(p_ref, dt_ref, cs_ref, s0_ref, cwx_ref, cwbc_ref, cbx_ref, cbbc_ref, dtb_ref, alog_ref,
                       dskip_ref, nw_ref, od_ref, st_ref, cv_ref):
    row = p_ref[...]
    xn = row[:, 4096:5120]
    bcn = row[:, 5120:5632]
    cx = cbx_ref[...] + cwx_ref[3:4, :] * xn
    cbc = cbbc_ref[...] + cwbc_ref[3:4, :] * bcn
    for w in range(CONV_W - 1):
        cx = cx + cwx_ref[w:w + 1, :] * cs_ref[w:w + 1, 0:DI_D]
        cbc = cbc + cwbc_ref[w:w + 1, :] * cs_ref[w:w + 1, DI_D:CONV_CH]
    cv_ref[0:1, :] = cs_ref[1:2, :]
    cv_ref[1:2, :] = cs_ref[2:3, :]
    cv_ref[2:3, 0:DI_D] = xn
    cv_ref[2:3, DI_D:CONV_CH] = bcn
    xs = _silu(cx)
    bcs = _silu(cbc)
    dt = _softplus(dt_ref[...] + dtb_ref[...])
    da = jnp.exp(dt * (-jnp.exp(alog_ref[...])))
    lane = lax.broadcasted_iota(jnp.int32, (1, 128), 1)
    lo = lane < 64
    bcol = [_col_bcast(bcs[:, 128 * g:128 * (g + 1)]) for g in range(G_D)]
    ccol = [_col_bcast(bcs[:, 256 + 128 * g:256 + 128 * (g + 1)]) for g in range(G_D)]
    ys = []
    ss = jnp.zeros((1, 1), F32)
    for p in range(H_D // 2):
        g = (2 * p) // (H_D // G_D)
        sl = slice(128 * p, 128 * (p + 1))
        le, lo_ = DT_LANE + 2 * p, DT_LANE + 2 * p + 1
        dt_pair = jnp.where(lo, _lane_col(dt, le), _lane_col(dt, lo_))
        da_pair = jnp.where(lo, _lane_col(da, le), _lane_col(da, lo_))
        xs_p = xs[:, sl]
        s_new = da_pair * s0_ref[p] + bcol[g] * (xs_p * dt_pair)
        st_ref[p] = s_new
        y = jnp.sum(ccol[g] * s_new, axis=0, keepdims=True)
        y = (y + xs_p * dskip_ref[:, sl]) * _silu(row[:, 3072 + 128 * p:3072 + 128 * (p + 1)])
        ys.append(y)
        ss = ss + jnp.sum(y * y, axis=-1, keepdims=True)
    inv = lax.rsqrt(ss * (1.0 / DI_D) + EPS)
    for p in range(H_D // 2):
        sl = slice(128 * p, 128 * (p + 1))
        od_ref[:, sl] = ys[p] * inv * nw_ref[:, sl]


def _ssd_sample(p_main, p_small, cs, s0, cwx, cwbc, cbx, cbbc, dtb, alog, dskip, nw):
    nb = p_main.shape[0]
    b3 = lambda b: (b, 0, 0)
    b4 = lambda b: (b, 0, 0, 0)
    c2 = lambda b: (0, 0)
    return pl.pallas_call(
        _ssd_sample_kernel,
        grid=(nb,),
        in_specs=[
            pl.BlockSpec((None, 1, ODD_MAIN), b3),
            pl.BlockSpec((None, 1, ODD_SMALL), b3),
            pl.BlockSpec((None, CONV_W - 1, CONV_CH), b3),
            pl.BlockSpec((None, 8, 128, 128), b4),
            pl.BlockSpec((CONV_W, DI_D), c2), pl.BlockSpec((CONV_W, 512), c2),
            pl.BlockSpec((1, DI_D), c2), pl.BlockSpec((1, 512), c2),
            pl.BlockSpec((1, ODD_SMALL), c2), pl.BlockSpec((1, ODD_SMALL), c2),
            pl.BlockSpec((1, DI_D), c2), pl.BlockSpec((1, DI_D), c2),
        ],
        out_specs=[
            pl.BlockSpec((None, 1, DI_D), b3),
            pl.BlockSpec((None, 8, 128, 128), b4),
            pl.BlockSpec((None, CONV_W - 1, CONV_CH), b3),
        ],
        out_shape=[jax.ShapeDtypeStruct((nb, 1, DI_D), F32), jax.ShapeDtypeStruct((nb, 8, 128, 128), F32),
                   jax.ShapeDtypeStruct((nb, CONV_W - 1, CONV_CH), F32)],
        compiler_params=_cparams(("arbitrary",)),
        name="ssd_sample",
    )(p_main.reshape(nb, 1, ODD_MAIN), p_small.reshape(nb, 1, ODD_SMALL), cs, s0,
      cwx, cwbc, cbx, cbbc, dtb, alog, dskip, nw)


def _pairs_to_heads(s):
    b, p, n, _ = s.shape
    return s.reshape(b, p, n, 2, 64).transpose(0, 1, 3, 2, 4).reshape(b, 2 * p, n, 64)


def _heads_to_pairs(s):
    b, h, n, d = s.shape
    return s.reshape(b, h // 2, 2, n, d).transpose(0, 1, 3, 2, 4).reshape(b, h // 2, n, 2 * d)


def _pad_lanes(v, start, width=ODD_SMALL):
    out = jnp.zeros((1, width), F32)
    return lax.dynamic_update_slice(out, v.reshape(1, -1).astype(F32), (0, start))


def kernel(x_prompt, x_sample, state_ret, cache_swa_k, cache_swa_v, cache_fox_k, cache_fox_v, cache_fox_logf,
           state_ssm, state_conv, page_table, norm_mix_pre, norm_mix_post, norm_mlp_pre, norm_mlp_post,
           w_in_even, w_out_even, ret_norm_w, swa_sinks, w_in_odd, w_out_odd, fox_fb, conv_w, conv_b,
           dt_bias, a_log, d_skip, ssd_norm_w, w_up, w_down):
    nb, seq = BATCH, SEQ
    nc = seq // CHUNK
    mp = nb * seq
    ms = DEC_BATCH
    xp = x_prompt.reshape(mp, D_MODEL)
    xs = x_sample.reshape(ms, D_MODEL)
    row = lambda v: v.reshape(1, -1)

    pos_p = jnp.arange(seq, dtype=jnp.int32)
    pos_s = jnp.full((1,), PAST_LEN, dtype=jnp.int32)
    ca_p, sa_p = _rope_tables(pos_p, RET_THETA, 8)
    cb_p, sb_p = _rope_tables(pos_p, ROPE_THETA_B, 2)
    ca_s, sa_s = _rope_tables(pos_s, RET_THETA, 8)
    cb_s, sb_s = _rope_tables(pos_s, ROPE_THETA_B, 2)

    we = w_in_even[0]
    we_main = we[:, :EVEN_MAIN].astype(BF16)
    we_small = we[:, EVEN_MAIN:].astype(BF16)
    wo = w_out_even[0].astype(BF16)
    wo_a, wo_b = wo[:1024], wo[1024:]
    g_pre, g_post = row(norm_mix_pre[0]), row(norm_mix_post[0])
    gm_pre, gm_post = row(norm_mlp_pre[0]), row(norm_mlp_post[0])
    gn_w = row(ret_norm_w[0])
    sinks = swa_sinks[0]

    pm, psm, sm, ssm_ = _proj(xp, xs, g_pre, we_main, we_small, odd=False, tm=1024, tn=512, seq=seq)
    out_a, ret_p = _ret_prompt(pm, ca_p, sa_p, gn_w, nb=nb, nc=nc)
    out_b, swak_p, swav_p = _swa_prompt(pm, psm, cb_p, sb_p, sinks, nb=nb, nc=nc)
    mix_s, ret_s, swak_s, swav_s = _even_sample(
        sm, ssm_, state_ret[0].reshape(ms, 4, 128, 128), cache_swa_k[0].reshape(ms, WINDOW, 128),
        cache_swa_v[0].reshape(ms, WINDOW, 128), ca_s, sa_s, cb_s, sb_s, gn_w, sinks)
    mix_s = mix_s.reshape(ms, 2048).astype(BF16)
    xp, xs = _outproj(out_a, out_b, xp, mix_s[:, :1024], mix_s[:, 1024:], xs, wo_a, wo_b, g_post, tm=512)
    xp, xs = _mlp(xp, xs, gm_pre, gm_post, w_up, w_down, 0, tm=1024, tf=512)

    wod = w_in_odd[0]
    wod_main = jnp.concatenate([wod[:, :3072], wod[:, 3080:5640]], axis=1).astype(BF16)
    wod_small = jnp.concatenate(
        [wod[:, 3072:3080], wod[:, 5640:5656], jnp.zeros((D_MODEL, ODD_SMALL - 24), F32)], axis=1).astype(BF16)
    wo1 = w_out_odd[0].astype(BF16)
    wo_c, wo_d = wo1[:1024], wo1[1024:]
    g_pre, g_post = row(norm_mix_pre[1]), row(norm_mix_post[1])
    gm_pre, gm_post = row(norm_mlp_pre[1]), row(norm_mlp_post[1])
    fb = _pad_lanes(fox_fb[0], 0)
    dtb = _pad_lanes(dt_bias[0], DT_LANE)
    alog = _pad_lanes(a_log[0], DT_LANE)
    cw = conv_w[0]
    cwx, cwbc = cw[:, :DI_D], cw[:, DI_D:]
    cbx, cbbc = row(conv_b[0][:DI_D]), row(conv_b[0][DI_D:])
    dskip = row(jnp.repeat(d_skip[0], HD_D))
    nw = row(ssd_norm_w[0])

    pm, psm, lf_p, fc_p, sm, ssm_, lf_s = _proj(xp, xs, g_pre, wod_main, wod_small, fb,
                                               odd=True, tm=1024, tn=512, seq=seq)
    fc = fc_p[:, :H_C].reshape(nb, seq, H_C).transpose(0, 2, 1)
    out_c = _fox_prompt(pm, fc[..., None], fc[:, :, None, :], nb=nb, seq=seq, tq=256)
    out_d, ssm_pairs_p, conv_p = _ssd_prompt(pm, psm, cwx, cwbc, cbx, cbbc, dtb, alog, dskip, nw, nb=nb, nc=nc)
    fox_k_p = pm[:, 1024:2048].reshape(1, nb, seq, H_C, HD_C)
    fox_v_p = pm[:, 2048:3072].reshape(1, nb, seq, H_C, HD_C)
    fox_lf_p = lf_p[:, :H_C].reshape(1, nb, seq, H_C)

    q_s = sm[:, 0:1024].reshape(ms, H_C, HD_C)
    k_s = sm[:, 1024:2048].reshape(ms, H_C, HD_C)
    v_s = sm[:, 2048:3072].reshape(ms, H_C, HD_C)
    lf_s8 = lf_s[:, :H_C]
    out_c_s = _fox_decode(page_table, q_s, k_s, v_s, lf_s8.reshape(ms, H_C, 1),
                          cache_fox_k[0], cache_fox_v[0], cache_fox_logf[0].transpose(0, 2, 1), pp=16)
    out_d_s, ssm_pairs_s, conv_s = _ssd_sample(sm, ssm_, state_conv[0], _heads_to_pairs(state_ssm[0]),
                                               cwx, cwbc, cbx, cbbc, dtb, alog, dskip, nw)
    xp, xs = _outproj(out_c, out_d, xp, out_c_s.reshape(ms, 1024).astype(BF16),
                      out_d_s.reshape(ms, DI_D).astype(BF16), xs, wo_c, wo_d, g_post, tm=512)
    xp, xs = _mlp(xp, xs, gm_pre, gm_post, w_up, w_down, 1, tm=1024, tf=512)

    return (
        xp.reshape(nb, seq, D_MODEL), xs.reshape(ms, 1, D_MODEL),
        ret_p.reshape(1, nb, H_A, DK_A, DV_A), ret_s.reshape(1, ms, H_A, DK_A, DV_A),
        swak_p.reshape(1, nb, WINDOW, KV_B, HD_B), swav_p.reshape(1, nb, WINDOW, KV_B, HD_B),
        swak_s.reshape(1, ms, WINDOW, KV_B, HD_B), swav_s.reshape(1, ms, WINDOW, KV_B, HD_B),
        fox_k_p, fox_v_p, fox_lf_p,
        k_s.reshape(1, ms, 1, H_C, HD_C), v_s.reshape(1, ms, 1, H_C, HD_C), lf_s8.reshape(1, ms, 1, H_C),
        _pairs_to_heads(ssm_pairs_p)[None], _pairs_to_heads(ssm_pairs_s)[None],
        conv_p[None], conv_s[None],
    )
```

```python
import functools

import numpy as np
import jax
import jax.numpy as jnp
from jax import lax
from jax.experimental import pallas as pl
from jax.experimental.pallas import tpu as pltpu

F32 = jnp.float32
BF16 = jnp.bfloat16

D_MODEL = 2048
BATCH = 4
SEQ = 2048
DEC_BATCH = 32
PAST_LEN = 16384
PAGE_SIZE = 128
D_FF = 4 * D_MODEL
EPS = 1e-6
GN_EPS = 1e-5
CHUNK = 128

H_A, DK_A, DV_A = 8, 64, 128
RET_THETA = 10000.0
H_B, KV_B, HD_B = 16, 2, 64
WINDOW = 128
ROPE_THETA_B = 150000.0
H_C, HD_C = 8, 128
H_D, HD_D, G_D, N_D = 16, 64, 2, 128
CONV_W = 4
DI_D = H_D * HD_D
CONV_CH = DI_D + 2 * G_D * N_D

EVEN_MAIN = 4096
EVEN_SMALL = 256
ODD_MAIN = 5632
ODD_SMALL = 128
DT_LANE = 8

NEG = -1e30
VMEM_LIMIT = 56 * 1024 * 1024
MLP_VMEM_LIMIT = 60 * 1024 * 1024
MLP_SUB_ROWS = 256
MIX_NBAT = 2

_LOG_GAMMA = [float(v) for v in np.log1p(-np.exp2(-5.0 - np.arange(H_A, dtype=np.float64)))]


def _cparams(sem, vmem_limit=VMEM_LIMIT):
    return pltpu.CompilerParams(dimension_semantics=sem, vmem_limit_bytes=vmem_limit)


def _silu(x):
    return x * jax.nn.sigmoid(x)


def _softplus(x):
    return jnp.maximum(x, 0.0) + jnp.log1p(jnp.exp(-jnp.abs(x)))


def _log_sigmoid(x):
    return jnp.minimum(x, 0.0) - jnp.log1p(jnp.exp(-jnp.abs(x)))


def _rms(x, g):
    ms = jnp.mean(x * x, axis=-1, keepdims=True)
    return (x * lax.rsqrt(ms + EPS)) * g


def _dot(a, b):
    return jnp.dot(a, b, preferred_element_type=F32)


def _dot_nt(a, b):
    return lax.dot_general(a, b, (((1,), (1,)), ((), ())), preferred_element_type=F32)


def _split3(x):
    hi = x.astype(BF16)
    r = x - hi.astype(F32)
    mid = r.astype(BF16)
    lo = (r - mid.astype(F32)).astype(BF16)
    return hi, mid, lo


def _exact_left01(m01, x):
    hi, mid, lo = _split3(x)
    return _dot(m01, hi) + _dot(m01, mid) + _dot(m01, lo)


def _exact_right01(x, m01):
    hi, mid, lo = _split3(x)
    return _dot(hi, m01) + _dot(mid, m01) + _dot(lo, m01)


def _tri_lower(n):
    r = lax.broadcasted_iota(jnp.int32, (n, n), 0)
    c = lax.broadcasted_iota(jnp.int32, (n, n), 1)
    return jnp.where(r >= c, 1.0, 0.0).astype(BF16)


def _rope64(x, c, s):
    w = x.shape[-1]
    ax = x.ndim - 1
    lane = lax.broadcasted_iota(jnp.int32, x.shape, ax)
    first = (lane & 32) == 0
    left = pltpu.roll(x, w - 32, axis=ax)
    right = pltpu.roll(x, 32, axis=ax)
    return x * c + jnp.where(first, left, right) * s


def _rope_tables(pos, theta, reps):
    inv = 1.0 / (theta ** (jnp.arange(32, dtype=F32) * (2.0 / 64)))
    ang = pos.astype(F32)[:, None] * inv[None, :]
    cos, sin = jnp.cos(ang), jnp.sin(ang)
    c = jnp.concatenate([cos, cos], axis=-1)
    s = jnp.concatenate([-sin, sin], axis=-1)
    return jnp.tile(c, (1, reps)), jnp.tile(s, (1, reps))


def _proj_kernel(*refs, odd, tm, tn, seq):
    if odd:
        (x_ref, xs_ref, g_ref, w_ref, wsm_ref, wu_ref, wd_ref, fb_ref,
         o_ref, osm_ref, os_ref, ossm_ref, wub_ref, wdb_ref, lf_ref, fc_ref, lfs_ref, h_scr, carry_scr) = refs
    else:
        (x_ref, xs_ref, g_ref, w_ref, wsm_ref, wu_ref, wd_ref,
         o_ref, osm_ref, os_ref, ossm_ref, wub_ref, wdb_ref, h_scr) = refs
    i = pl.program_id(0)
    wub_ref[...] = wu_ref[...].astype(BF16)
    wdb_ref[...] = wd_ref[...].astype(BF16)

    def rows(x_r, o_r, osm_r, n):
        h_scr[0:n, :] = _rms(x_r[...], g_ref[...]).astype(BF16)
        sm = _dot(h_scr[0:n, :], wsm_ref[...])
        osm_r[...] = sm
        for t in range(w_ref.shape[1] // tn):
            cols = slice(t * tn, (t + 1) * tn)
            o_r[:, cols] = _dot(h_scr[0:n, :], w_ref[:, cols])
        return sm

    sm = rows(x_ref, o_ref, osm_ref, tm)
    if odd:
        lf = _log_sigmoid(sm + fb_ref[...])
        lf_ref[...] = lf

        @pl.when((i * tm) % seq == 0)
        def _():
            carry_scr[...] = jnp.zeros_like(carry_scr)

        f = _exact_left01(_tri_lower(tm), lf) + carry_scr[...]
        fc_ref[...] = f
        carry_scr[...] = f[tm - 1:tm, :]

    @pl.when(i == 0)
    def _():
        sms = rows(xs_ref, os_ref, ossm_ref, xs_ref.shape[0])
        if odd:
            lfs_ref[...] = _log_sigmoid(sms + fb_ref[...])


def _proj(x, xs, g, w_main, w_small, w_up, w_down, layer, fb=None, *, odd, tm, tn, seq):
    m, ms = x.shape[0], xs.shape[0]
    n_main = w_main.shape[1]
    n_small = w_small.shape[1]
    ni = m // tm
    ru, rd = D_MODEL // ni, D_FF // ni
    c2 = lambda i: (0, 0)
    r2 = lambda i: (i, 0)
    in_specs = [
        pl.BlockSpec((tm, D_MODEL), r2),
        pl.BlockSpec((ms, D_MODEL), c2),
        pl.BlockSpec((1, D_MODEL), c2),
        pl.BlockSpec((D_MODEL, n_main), c2, pipeline_mode=pl.Buffered(1)),
        pl.BlockSpec((D_MODEL, n_small), c2),
        pl.BlockSpec((None, ru, D_FF), lambda i: (layer, i, 0)),
        pl.BlockSpec((None, rd, D_MODEL), lambda i: (layer, i, 0)),
    ]
    args = [x, xs, g, w_main, w_small, w_up, w_down]
    small = jax.ShapeDtypeStruct((m, n_small), F32)
    small_s = jax.ShapeDtypeStruct((ms, n_small), F32)
    small_spec = pl.BlockSpec((tm, n_small), r2)
    small_s_spec = pl.BlockSpec((ms, n_small), c2)
    out_shape = [jax.ShapeDtypeStruct((m, n_main), F32), small,
                 jax.ShapeDtypeStruct((ms, n_main), F32), small_s,
                 jax.ShapeDtypeStruct((D_MODEL, D_FF), BF16), jax.ShapeDtypeStruct((D_FF, D_MODEL), BF16)]
    out_specs = [pl.BlockSpec((tm, n_main), r2), small_spec,
                 pl.BlockSpec((ms, n_main), c2), small_s_spec,
                 pl.BlockSpec((ru, D_FF), r2), pl.BlockSpec((rd, D_MODEL), r2)]
    scratch = [pltpu.VMEM((tm, D_MODEL), BF16)]
    if odd:
        in_specs.append(pl.BlockSpec((1, n_small), c2))
        args.append(fb)
        out_shape += [small, small, small_s]
        out_specs += [small_spec, small_spec, small_s_spec]
        scratch.append(pltpu.VMEM((1, n_small), F32))
    return pl.pallas_call(
        functools.partial(_proj_kernel, odd=odd, tm=tm, tn=tn, seq=seq),
        grid=(ni,), in_specs=in_specs, out_specs=out_specs, out_shape=out_shape,
        scratch_shapes=scratch, compiler_params=_cparams(("arbitrary",)),
        name="proj_odd" if odd else "proj_even",
    )(*args)


def _outproj_kernel(a_ref, b_ref, x_ref, as_ref, bs_ref, xs_ref, wa_ref, wb_ref, g_ref, o_ref, os_ref):
    g = g_ref[...]
    n = x_ref.shape[0]
    sub = min(n, MLP_SUB_ROWS)
    for r in range(n // sub):
        rows = slice(r * sub, (r + 1) * sub)
        y = _dot(a_ref[rows, :], wa_ref[...]) + _dot(b_ref[rows, :], wb_ref[...])
        o_ref[rows, :] = x_ref[rows, :] + _rms(y, g)

    @pl.when(pl.program_id(0) == 0)
    def _():
        ys = _dot(as_ref[...], wa_ref[...]) + _dot(bs_ref[...], wb_ref[...])
        os_ref[...] = xs_ref[...] + _rms(ys, g)


def _outproj(a, b, x, a_s, b_s, xs, wa, wb, g, *, tm):
    m, ms = x.shape[0], xs.shape[0]
    ka, kb = a.shape[1], b.shape[1]
    c2 = lambda i: (0, 0)
    return pl.pallas_call(
        _outproj_kernel,
        grid=(m // tm,),
        in_specs=[
            pl.BlockSpec((tm, ka), lambda i: (i, 0)),
            pl.BlockSpec((tm, kb), lambda i: (i, 0)),
            pl.BlockSpec((tm, D_MODEL), lambda i: (i, 0)),
            pl.BlockSpec((ms, ka), c2),
            pl.BlockSpec((ms, kb), c2),
            pl.BlockSpec((ms, D_MODEL), c2),
            pl.BlockSpec((ka, D_MODEL), c2),
            pl.BlockSpec((kb, D_MODEL), c2),
            pl.BlockSpec((1, D_MODEL), c2),
        ],
        out_specs=[pl.BlockSpec((tm, D_MODEL), lambda i: (i, 0)), pl.BlockSpec((ms, D_MODEL), c2)],
        out_shape=[jax.ShapeDtypeStruct((m, D_MODEL), F32), jax.ShapeDtypeStruct((ms, D_MODEL), F32)],
        compiler_params=_cparams(("arbitrary",)),
        name="outproj",
    )(a, b, x, a_s, b_s, xs, wa, wb, g)


def _mlp_kernel(x_ref, xs_ref, gpre_ref, gpost_ref, wu_ref, wd_ref, o_ref, os_ref, h_scr, hs_scr):
    i = pl.program_id(0)
    j = pl.program_id(1)
    last = pl.num_programs(1) - 1

    def group(x_r, o_r, h_s):
        @pl.when(j == 0)
        def _():
            h_s[...] = _rms(x_r[...], gpre_ref[...]).astype(BF16)
            o_r[...] = jnp.zeros_like(o_r)

        n = x_r.shape[0]
        sub = min(n, MLP_SUB_ROWS)
        for r in range(n // sub):
            rows = slice(r * sub, (r + 1) * sub)
            u = jnp.maximum(_dot(h_s[rows, :], wu_ref[...]), 0.0)
            o_r[rows, :] += _dot((u * u).astype(BF16), wd_ref[...])

        @pl.when(j == last)
        def _():
            o_r[...] = x_r[...] + _rms(o_r[...], gpost_ref[...])

    group(x_ref, o_ref, h_scr)

    @pl.when(i == 0)
    def _():
        group(xs_ref, os_ref, hs_scr)


def _mlp(x, xs, gpre, gpost, w_up, w_down, *, tm, tf):
    m, ms = x.shape[0], xs.shape[0]
    c2 = lambda i, j: (0, 0)
    return pl.pallas_call(
        _mlp_kernel,
        grid=(m // tm, D_FF // tf),
        in_specs=[
            pl.BlockSpec((tm, D_MODEL), lambda i, j: (i, 0)),
            pl.BlockSpec((ms, D_MODEL), c2),
            pl.BlockSpec((1, D_MODEL), c2),
            pl.BlockSpec((1, D_MODEL), c2),
            pl.BlockSpec((D_MODEL, tf), lambda i, j: (0, j)),
            pl.BlockSpec((tf, D_MODEL), lambda i, j: (j, 0)),
        ],
        out_specs=[pl.BlockSpec((tm, D_MODEL), lambda i, j: (i, 0)), pl.BlockSpec((ms, D_MODEL), c2)],
        out_shape=[jax.ShapeDtypeStruct((m, D_MODEL), F32), jax.ShapeDtypeStruct((ms, D_MODEL), F32)],
        scratch_shapes=[pltpu.VMEM((tm, D_MODEL), BF16), pltpu.VMEM((ms, D_MODEL), BF16)],
        compiler_params=_cparams(("arbitrary", "arbitrary"), MLP_VMEM_LIMIT),
        name="mlp",
    )(x, xs, gpre, gpost, w_up, w_down)


def _group_norm_gate(y, gate, gw):
    mu = jnp.mean(y, axis=-1, keepdims=True)
    d = y - mu
    var = jnp.mean(d * d, axis=-1, keepdims=True)
    return _silu(gate) * (d * lax.rsqrt(var + GN_EPS) * gw)


def _ret_prompt_kernel(q_all, k_all, v_all, g_all, cos_ref, sin_ref, gn_ref, oa_all, st_all,
                       s_all, d_scr, e_scr, t_scr):
    b = pl.program_id(0)
    c = pl.program_id(1)
    t_i = lax.broadcasted_iota(jnp.int32, (CHUNK, CHUNK), 0)
    s_i = lax.broadcasted_iota(jnp.int32, (CHUNK, CHUNK), 1)
    lo = s_i < 64

    @pl.when((b == 0) & (c == 0))
    def _():
        tf = t_i.astype(F32)
        sf = s_i.astype(F32)
        for h in range(H_A):
            d_scr[h] = jnp.where(t_i >= s_i, jnp.exp((tf - sf) * _LOG_GAMMA[h]), 0.0)
            e_scr[h] = jnp.exp((tf + 1.0) * _LOG_GAMMA[h])
        for p in range(H_A // 2):
            lg = jnp.where(lo, _LOG_GAMMA[2 * p], _LOG_GAMMA[2 * p + 1])
            t_scr[p] = jnp.exp((CHUNK - 1.0 - tf) * lg)

    @pl.when(c == 0)
    def _():
        s_all[...] = jnp.zeros_like(s_all)

    cos, sin = cos_ref[...], sin_ref[...]
    top = t_i < 64
    for u in range(MIX_NBAT):
        v_ref, g_ref, oa_ref, s_scr = v_all.at[u], g_all.at[u], oa_all.at[u], s_all.at[u]
        qr = _rope64(q_all[u], cos, sin)
        kr = _rope64(k_all[u], cos, sin) * (DK_A ** -0.5)
        for p in range(H_A // 2):
            sl = slice(128 * p, 128 * (p + 1))
            qp, kp = qr[:, sl], kr[:, sl]
            kb = kp.astype(BF16)
            s_old = s_scr[p]
            s_old_b = s_old.astype(BF16)
            ktt = jnp.transpose(kp * t_scr[p]).astype(BF16)
            upd = []
            for e in range(2):
                h = 2 * p + e
                hs = slice(128 * h, 128 * (h + 1))
                qm = jnp.where(lo if e == 0 else ~lo, qp, 0.0).astype(BF16)
                vh = v_ref[:, hs].astype(BF16)
                att = (_dot_nt(qm, kb) * d_scr[h]).astype(BF16)
                y = _dot(att, vh) + _dot(qm, s_old_b) * e_scr[h]
                oa_ref[:, hs] = _group_norm_gate(y, g_ref[:, hs], gn_ref[:, hs]).astype(BF16)
                upd.append(_dot(ktt, vh))
            g128 = jnp.where(top, float(np.exp(CHUNK * _LOG_GAMMA[2 * p])),
                             float(np.exp(CHUNK * _LOG_GAMMA[2 * p + 1])))
            s_scr[p] = g128 * s_old + jnp.where(top, upd[0], upd[1])

    @pl.when(c == pl.num_programs(1) - 1)
    def _():
        st_all[...] = s_all[...]


def _batch_groups(a, nb, seq):
    return a.reshape(nb // MIX_NBAT, MIX_NBAT, seq, a.shape[-1])


def _ret_prompt(p_main, cos, sin, gn_w, *, nb, nc):
    seq = nc * CHUNK
    pg = _batch_groups(p_main, nb, seq)
    blk = lambda w, col: pl.BlockSpec((None, MIX_NBAT, CHUNK, w), lambda b, c: (b, 0, c, col))
    oa, st = pl.pallas_call(
        _ret_prompt_kernel,
        grid=(nb // MIX_NBAT, nc),
        in_specs=[
            blk(512, 0), blk(512, 1), blk(1024, 1), blk(1024, 2),
            pl.BlockSpec((CHUNK, 512), lambda b, c: (c, 0)),
            pl.BlockSpec((CHUNK, 512), lambda b, c: (c, 0)),
            pl.BlockSpec((1, 1024), lambda b, c: (0, 0)),
        ],
        out_specs=[
            blk(1024, 0),
            pl.BlockSpec((None, MIX_NBAT, 4, 128, 128), lambda b, c: (b, 0, 0, 0, 0)),
        ],
        out_shape=[jax.ShapeDtypeStruct((nb // MIX_NBAT, MIX_NBAT, seq, 1024), BF16),
                   jax.ShapeDtypeStruct((nb // MIX_NBAT, MIX_NBAT, 4, 128, 128), F32)],
        scratch_shapes=[pltpu.VMEM((MIX_NBAT, 4, 128, 128), F32), pltpu.VMEM((H_A, 128, 128), F32),
                        pltpu.VMEM((H_A, 128, 128), F32), pltpu.VMEM((4, 128, 128), F32)],
        compiler_params=_cparams(("arbitrary", "arbitrary")),
        name="ret_prompt",
    )(pg, pg, pg, pg, cos, sin, gn_w)
    return oa.reshape(nb * seq, 1024), st.reshape(nb, 4, 128, 128)


def _swa_padded(x2):
    lane = lax.broadcasted_iota(jnp.int32, x2.shape, 1)
    lo = lane < 64
    xr = pltpu.roll(x2, 64, axis=1)
    z = jnp.zeros_like(x2)
    return {
        (0, 0): jnp.where(lo, x2, z).astype(BF16), (0, 1): jnp.where(lo, z, xr).astype(BF16),
        (1, 0): jnp.where(lo, xr, z).astype(BF16), (1, 1): jnp.where(lo, z, x2).astype(BF16),
    }


def _swa_attend(q_pair_fn, kpad, vpad, valid, sink_ref, store_fn, rows):
    per_group = (H_B // KV_B) // 2
    for g in range(KV_B):
        pairs = list(range(g * per_group, (g + 1) * per_group))
        qs = jnp.concatenate([q_pair_fn(jj) for jj in pairs], axis=0).astype(BF16)
        acc = None
        for e in range(2):
            s = _dot_nt(qs, kpad[(g, e)])
            if valid is not None:
                s = jnp.where(valid, s, NEG)
            sink = jnp.concatenate([jnp.full((rows, 1), sink_ref[2 * jj + e], F32) for jj in pairs], axis=0)
            mx = jnp.maximum(jnp.max(s, axis=-1, keepdims=True), sink)
            pr = jnp.exp(s - mx)
            den = jnp.sum(pr, axis=-1, keepdims=True) + jnp.exp(sink - mx)
            o = _dot(pr.astype(BF16), vpad[(g, e)]) / den
            acc = o if acc is None else acc + o
        for t, jj in enumerate(pairs):
            store_fn(jj, acc[t * rows:(t + 1) * rows])


def _swa_prompt_kernel(sink_ref, q_ref, kvc_ref, kvp_ref, cc_ref, sc_ref, cp_ref, sp_ref,
                       ob_ref, ko_ref, vo_ref):
    n = pl.program_id(1)
    cc, sc = cc_ref[...], sc_ref[...]
    kvc, kvp = kvc_ref[...], kvp_ref[...]
    kc = _rope64(kvc[:, :128], cc, sc)
    kp = _rope64(kvp[:, :128], cp_ref[...], sp_ref[...])
    vc = kvc[:, 128:]
    kpad = _swa_padded(jnp.concatenate([kp, kc], axis=0))
    vpad = _swa_padded(jnp.concatenate([kvp[:, 128:], vc], axis=0))
    stacked = WINDOW * (H_B // KV_B) // 2
    i = lax.broadcasted_iota(jnp.int32, (stacked, 2 * WINDOW), 0) & (WINDOW - 1)
    j = lax.broadcasted_iota(jnp.int32, (stacked, 2 * WINDOW), 1)
    valid = (j >= i + 1) & (j <= i + WINDOW) & ((n > 0) | (j >= WINDOW))

    def q_pair(jj):
        return _rope64(q_ref[:, 128 * jj:128 * (jj + 1)], cc, sc) * (HD_B ** -0.5)

    def store(jj, acc):
        ob_ref[:, 128 * jj:128 * (jj + 1)] = acc.astype(BF16)

    _swa_attend(q_pair, kpad, vpad, valid, sink_ref, store, WINDOW)

    @pl.when(n == pl.num_programs(1) - 1)
    def _():
        ko_ref[...] = kc
        vo_ref[...] = vc


def _swa_prompt(p_main, p_small, cos, sin, sinks, *, nb, nc):
    m = nb * nc * CHUNK
    row = lambda b, n: b * nc + n
    prev = lambda b, n: b * nc + jnp.maximum(n - 1, 0)
    return pl.pallas_call(
        _swa_prompt_kernel,
        grid=(nb, nc),
        in_specs=[
            pl.BlockSpec(memory_space=pltpu.SMEM),
            pl.BlockSpec((CHUNK, 1024), lambda b, n: (row(b, n), 3)),
            pl.BlockSpec((CHUNK, 256), lambda b, n: (row(b, n), 0)),
            pl.BlockSpec((CHUNK, 256), lambda b, n: (prev(b, n), 0)),
            pl.BlockSpec((CHUNK, 128), lambda b, n: (n, 0)),
            pl.BlockSpec((CHUNK, 128), lambda b, n: (n, 0)),
            pl.BlockSpec((CHUNK, 128), lambda b, n: (jnp.maximum(n - 1, 0), 0)),
            pl.BlockSpec((CHUNK, 128), lambda b, n: (jnp.maximum(n - 1, 0), 0)),
        ],
        out_specs=[
            pl.BlockSpec((CHUNK, 1024), lambda b, n: (row(b, n), 0)),
            pl.BlockSpec((None, WINDOW, 128), lambda b, n: (b, 0, 0)),
            pl.BlockSpec((None, WINDOW, 128), lambda b, n: (b, 0, 0)),
        ],
        out_shape=[jax.ShapeDtypeStruct((m, 1024), BF16), jax.ShapeDtypeStruct((nb, WINDOW, 128), F32),
                   jax.ShapeDtypeStruct((nb, WINDOW, 128), F32)],
        compiler_params=_cparams(("arbitrary", "arbitrary")),
        name="swa_prompt",
    )(sinks, p_main, p_small, p_small, cos, sin, cos, sin)


def _col_bcast(row128):
    return jnp.transpose(jnp.broadcast_to(row128, (128, 128)))


def _even_sample_kernel(sink_ref, p_ref, ps_ref, s0_ref, kbuf_ref, vbuf_ref, ca_ref, sa_ref, cb_ref, sb_ref,
                        gn_ref, mix_ref, st_ref, ko_ref, vo_ref):
    row = p_ref[...]
    ca, sa = ca_ref[...], sa_ref[...]
    qr = _rope64(jnp.broadcast_to(row[:, 0:512], (8, 512)), ca, sa)
    kr = _rope64(jnp.broadcast_to(row[:, 512:1024], (8, 512)), ca, sa) * (DK_A ** -0.5)
    r_i = lax.broadcasted_iota(jnp.int32, (128, 128), 0)
    top = r_i < 64
    for p in range(H_A // 2):
        sl = slice(128 * p, 128 * (p + 1))
        kcol = _col_bcast(kr[0:1, sl])
        qcol = _col_bcast(qr[0:1, sl])
        he, ho = 2 * p, 2 * p + 1
        v_e = row[:, 1024 + 128 * he:1024 + 128 * (he + 1)]
        v_o = row[:, 1024 + 128 * ho:1024 + 128 * (ho + 1)]
        gam = jnp.where(top, float(np.exp(_LOG_GAMMA[he])), float(np.exp(_LOG_GAMMA[ho])))
        s_new = gam * s0_ref[p] + kcol * jnp.where(top, v_e, v_o)
        st_ref[p] = s_new
        prod = qcol * s_new
        for e, h in ((0, he), (1, ho)):
            y = jnp.sum(jnp.where(top if e == 0 else ~top, prod, 0.0), axis=0, keepdims=True)
            hs = slice(128 * h, 128 * (h + 1))
            gate = row[:, 2048 + 128 * h:2048 + 128 * (h + 1)]
            mix_ref[:, hs] = _group_norm_gate(y, gate, gn_ref[:, hs])

    cb, sb = cb_ref[...], sb_ref[...]
    ps = ps_ref[...]
    kn = _rope64(jnp.broadcast_to(ps[:, 0:128], (8, 128)), cb, sb)[0:1]
    vn = ps[:, 128:256]
    last = r_i == WINDOW - 1
    k_new = jnp.where(last, kn, pltpu.roll(kbuf_ref[...], WINDOW - 1, axis=0))
    v_new = jnp.where(last, vn, pltpu.roll(vbuf_ref[...], WINDOW - 1, axis=0))
    ko_ref[...] = k_new
    vo_ref[...] = v_new
    kpad = _swa_padded(k_new)
    vpad = _swa_padded(v_new)

    def q_pair(jj):
        q8 = jnp.broadcast_to(row[:, 3072 + 128 * jj:3072 + 128 * (jj + 1)], (8, 128))
        return _rope64(q8, cb, sb) * (HD_B ** -0.5)

    def store(jj, acc):
        mix_ref[:, 1024 + 128 * jj:1024 + 128 * (jj + 1)] = acc[0:1]

    _swa_attend(q_pair, kpad, vpad, None, sink_ref, store, 8)


def _even_sample(p_main, p_small, s0, kbuf, vbuf, ca, sa, cb, sb, gn_w, sinks):
    nb = p_main.shape[0]
    b3 = lambda b: (b, 0, 0)
    b4 = lambda b: (b, 0, 0, 0)
    c2 = lambda b: (0, 0)
    return pl.pallas_call(
        _even_sample_kernel,
        grid=(nb,),
        in_specs=[
            pl.BlockSpec(memory_space=pltpu.SMEM),
            pl.BlockSpec((None, 1, EVEN_MAIN), b3),
            pl.BlockSpec((None, 1, EVEN_SMALL), b3),
            pl.BlockSpec((None, 4, 128, 128), b4),
            pl.BlockSpec((None, WINDOW, 128), b3),
            pl.BlockSpec((None, WINDOW, 128), b3),
            pl.BlockSpec((1, 512), c2), pl.BlockSpec((1, 512), c2),
            pl.BlockSpec((1, 128), c2), pl.BlockSpec((1, 128), c2),
            pl.BlockSpec((1, 1024), c2),
        ],
        out_specs=[
            pl.BlockSpec((None, 1, 2048), b3),
            pl.BlockSpec((None, 4, 128, 128), b4),
            pl.BlockSpec((None, WINDOW, 128), b3),
            pl.BlockSpec((None, WINDOW, 128), b3),
        ],
        out_shape=[jax.ShapeDtypeStruct((nb, 1, 2048), F32), jax.ShapeDtypeStruct((nb, 4, 128, 128), F32),
                   jax.ShapeDtypeStruct((nb, WINDOW, 128), F32), jax.ShapeDtypeStruct((nb, WINDOW, 128), F32)],
        compiler_params=_cparams(("arbitrary",)),
        name="even_sample",
    )(sinks, p_main.reshape(nb, 1, EVEN_MAIN), p_small.reshape(nb, 1, EVEN_SMALL), s0, kbuf, vbuf,
      ca, sa, cb, sb, gn_w)


def _fox_prompt_kernel(q_ref, k_ref, v_ref, fcol_ref, frow_ref, o_ref, kb_scr, vb_scr, *, seq, tq):
    kb_scr[...] = k_ref[...].astype(BF16)
    vb_scr[...] = v_ref[...].astype(BF16)
    r_i = lax.broadcasted_iota(jnp.int32, (tq, tq), 0)
    c_i = lax.broadcasted_iota(jnp.int32, (tq, tq), 1)
    for qi in range(seq // tq):
        rows = slice(qi * tq, (qi + 1) * tq)
        q = (q_ref[rows, :] * (HD_C ** -0.5)).astype(BF16)
        fq = fcol_ref[rows, :]
        m = l = acc = None
        for j in range(qi + 1):
            cols = slice(j * tq, (j + 1) * tq)
            s = _dot_nt(q, kb_scr[cols, :]) + (fq - frow_ref[:, cols])
            if j == qi:
                s = jnp.where(c_i <= r_i, s, NEG)
            mx = jnp.max(s, axis=-1, keepdims=True)
            if j == 0:
                m = mx
                p = jnp.exp(s - m)
                l = jnp.sum(p, axis=-1, keepdims=True)
                acc = _dot(p.astype(BF16), vb_scr[cols, :])
            else:
                m_new = jnp.maximum(m, mx)
                a = jnp.exp(m - m_new)
                p = jnp.exp(s - m_new)
                l = a * l + jnp.sum(p, axis=-1, keepdims=True)
                acc = a * acc + _dot(p.astype(BF16), vb_scr[cols, :])
                m = m_new
        o_ref[rows, :] = (acc / l).astype(BF16)


def _fox_prompt(p_main, fcol, frow, *, nb, seq, tq):
    m = nb * seq
    return pl.pallas_call(
        functools.partial(_fox_prompt_kernel, seq=seq, tq=tq),
        grid=(nb, H_C),
        in_specs=[
            pl.BlockSpec((seq, HD_C), lambda b, h: (b, h)),
            pl.BlockSpec((seq, HD_C), lambda b, h: (b, H_C + h)),
            pl.BlockSpec((seq, HD_C), lambda b, h: (b, 2 * H_C + h)),
            pl.BlockSpec((None, None, seq, 1), lambda b, h: (b, h, 0, 0)),
            pl.BlockSpec((None, None, 1, seq), lambda b, h: (b, h, 0, 0)),
        ],
        out_specs=pl.BlockSpec((seq, HD_C), lambda b, h: (b, h)),
        out_shape=jax.ShapeDtypeStruct((m, H_C * HD_C), BF16),
        scratch_shapes=[pltpu.VMEM((seq, HD_C), BF16), pltpu.VMEM((seq, HD_C), BF16)],
        compiler_params=_cparams(("arbitrary", "arbitrary")),
        name="fox_prompt",
    )(p_main, p_main, p_main, fcol, frow)


def _fox_decode_kernel(*refs, pp):
    pt_ref = refs[0]
    q_ref, kn_ref, vn_ref, lfn_ref = refs[1:5]
    k_refs = refs[5:5 + pp]
    v_refs = refs[5 + pp:5 + 2 * pp]
    lf_refs = refs[5 + 2 * pp:5 + 3 * pp]
    o_ref = refs[5 + 3 * pp]
    m_scr, l_scr, acc_scr, carry_scr, rexp_scr, tri_scr = refs[6 + 3 * pp:]
    del pt_ref
    b = pl.program_id(0)
    s_id = pl.program_id(1)
    rows = PAGE_SIZE * H_C

    @pl.when((b == 0) & (s_id == 0))
    def _():
        r = lax.broadcasted_iota(jnp.int32, (PAGE_SIZE, rows), 0)
        c = lax.broadcasted_iota(jnp.int32, (PAGE_SIZE, rows), 1)
        rexp_scr[...] = jnp.where((c >> 3) == r, 1.0, 0.0).astype(BF16)
        r2 = lax.broadcasted_iota(jnp.int32, (PAGE_SIZE, PAGE_SIZE), 0)
        c2 = lax.broadcasted_iota(jnp.int32, (PAGE_SIZE, PAGE_SIZE), 1)
        tri_scr[...] = jnp.where(r2 <= c2, 1.0, 0.0).astype(BF16)

    @pl.when(s_id == 0)
    def _():
        m_scr[...] = jnp.full_like(m_scr, NEG)
        l_scr[...] = jnp.zeros_like(l_scr)
        acc_scr[...] = jnp.zeros_like(acc_scr)
        carry_scr[...] = jnp.zeros_like(carry_scr)

    qs = q_ref[...] * (HD_C ** -0.5)
    qb = qs.astype(BF16)
    h_i = lax.broadcasted_iota(jnp.int32, (H_C, rows), 0)
    c_i = lax.broadcasted_iota(jnp.int32, (H_C, rows), 1)
    diag = (c_i & (H_C - 1)) == h_i
    m, l, acc, carry = m_scr[...], l_scr[...], acc_scr[...], carry_scr[...]

    lf_all = jnp.concatenate([lf_refs[r][...] for r in range(pp)], axis=0)
    cum_in = _exact_right01(lf_all, tri_scr[...])
    offs = []
    for r in range(pp):
        offs.append(carry)
        carry = carry + cum_in[H_C * r:H_C * (r + 1), PAGE_SIZE - 1:PAGE_SIZE]
    gexp = _exact_right01(cum_in + jnp.concatenate(offs, axis=0), rexp_scr[...])

    logits = []
    for r in range(pp):
        k2 = k_refs[r][...].reshape(rows, HD_C).astype(BF16)
        logits.append(jnp.where(diag, _dot_nt(qb, k2) - gexp[H_C * r:H_C * (r + 1)], NEG))
    mx = logits[0]
    for r in range(1, pp):
        mx = jnp.maximum(mx, logits[r])
    m_new = jnp.maximum(m, jnp.max(mx, axis=-1, keepdims=True))
    a = jnp.exp(m - m_new)
    psum = pv = None
    for r in range(pp):
        p = jnp.exp(logits[r] - m_new)
        o = _dot(p.astype(BF16), v_refs[r][...].reshape(rows, HD_C).astype(BF16))
        psum = p if psum is None else psum + p
        pv = o if pv is None else pv + o
    l = a * l + jnp.sum(psum, axis=-1, keepdims=True)
    acc = a * acc + pv
    m = m_new
    m_scr[...] = m
    l_scr[...] = l
    acc_scr[...] = acc
    carry_scr[...] = carry

    @pl.when(s_id == pl.num_programs(1) - 1)
    def _():
        fq = carry + lfn_ref[...]
        s_new = jnp.sum(qs * kn_ref[...], axis=-1, keepdims=True)
        m_past = m + fq
        mx = jnp.maximum(m_past, s_new)
        wp = jnp.exp(m_past - mx)
        wn = jnp.exp(s_new - mx)
        o_ref[...] = (acc * wp + wn * vn_ref[...]) / (l * wp + wn)


def _fox_decode(page_table, q, k_new, v_new, lf_new, cache_k, cache_v, cache_lf, *, pp):
    nb, n_pages = page_table.shape
    b3 = lambda b, s, pt: (b, 0, 0)
    kv_spec = lambda r: pl.BlockSpec((None, PAGE_SIZE, H_C, HD_C), lambda b, s, pt: (pt[b, s * pp + r], 0, 0, 0))
    lf_spec = lambda r: pl.BlockSpec((None, H_C, PAGE_SIZE), lambda b, s, pt: (pt[b, s * pp + r], 0, 0))
    in_specs = [pl.BlockSpec((None, H_C, HD_C), b3), pl.BlockSpec((None, H_C, HD_C), b3),
                pl.BlockSpec((None, H_C, HD_C), b3), pl.BlockSpec((None, H_C, 1), b3)]
    in_specs += [kv_spec(r) for r in range(pp)] + [kv_spec(r) for r in range(pp)] + [lf_spec(r) for r in range(pp)]
    grid_spec = pltpu.PrefetchScalarGridSpec(
        num_scalar_prefetch=1, grid=(nb, n_pages // pp), in_specs=in_specs,
        out_specs=pl.BlockSpec((None, H_C, HD_C), b3),
        scratch_shapes=[pltpu.VMEM((H_C, 1), F32), pltpu.VMEM((H_C, 1), F32), pltpu.VMEM((H_C, HD_C), F32),
                        pltpu.VMEM((H_C, 1), F32), pltpu.VMEM((PAGE_SIZE, PAGE_SIZE * H_C), BF16),
                        pltpu.VMEM((PAGE_SIZE, PAGE_SIZE), BF16)],
    )
    return pl.pallas_call(
        functools.partial(_fox_decode_kernel, pp=pp),
        grid_spec=grid_spec,
        out_shape=jax.ShapeDtypeStruct((nb, H_C, HD_C), F32),
        compiler_params=_cparams(("arbitrary", "arbitrary")),
        name="fox_decode",
    )(page_table, q, k_new, v_new, lf_new, *([cache_k] * pp), *([cache_v] * pp), *([cache_lf] * pp))


def _lane_col(x, lane):
    return x[:, lane:lane + 1]


def _ssd_prompt_kernel(x_ref, bc_ref, z_ref, dt_ref, cwx_ref, cwbc_ref, cbx_ref, cbbc_ref, dtb_ref, alog_ref,
                       dskip_ref, nw_ref, od_ref, st_ref, cv_ref, xpx_scr, xpbc_scr, s_scr, y_scr):
    c = pl.program_id(1)
    nc = pl.num_programs(1)

    @pl.when(c == 0)
    def _():
        xpx_scr[0:8, :] = jnp.zeros((8, DI_D), F32)
        xpbc_scr[0:8, :] = jnp.zeros((8, 512), F32)
        s_scr[...] = jnp.zeros_like(s_scr)

    xpx_scr[8:8 + CHUNK, :] = x_ref[...]
    xpbc_scr[8:8 + CHUNK, :] = bc_ref[...]
    cx = cbx_ref[...]
    cbc = cbbc_ref[...]
    for k in range(CONV_W):
        w = CONV_W - 1 - k
        cx = cx + cwx_ref[w:w + 1, :] * xpx_scr[8 - k:8 - k + CHUNK, :]
        cbc = cbc + cwbc_ref[w:w + 1, :] * xpbc_scr[8 - k:8 - k + CHUNK, :]

    @pl.when(c == nc - 1)
    def _():
        cv_ref[:, 0:DI_D] = xpx_scr[CHUNK + 5:CHUNK + 8, :]
        cv_ref[:, DI_D:CONV_CH] = xpbc_scr[CHUNK + 5:CHUNK + 8, :]

    xpx_scr[0:8, :] = xpx_scr[CHUNK:CHUNK + 8, :]
    xpbc_scr[0:8, :] = xpbc_scr[CHUNK:CHUNK + 8, :]

    xs = _silu(cx)
    bcs = _silu(cbc)
    dt = _softplus(dt_ref[...] + dtb_ref[...])
    la = dt * (-jnp.exp(alog_ref[...]))
    cum = _exact_left01(_tri_lower(CHUNK), la)
    cum_t = jnp.transpose(cum)
    t_i = lax.broadcasted_iota(jnp.int32, (CHUNK, CHUNK), 0)
    s_i = lax.broadcasted_iota(jnp.int32, (CHUNK, CHUNK), 1)
    causal = t_i >= s_i
    lo = s_i < 64
    att_base, bt = [], []
    for g in range(G_D):
        bg = bcs[:, 128 * g:128 * (g + 1)]
        cg = bcs[:, 256 + 128 * g:256 + 128 * (g + 1)]
        att_base.append(_dot_nt(cg.astype(BF16), bg.astype(BF16)))
        bt.append(jnp.transpose(bg).astype(BF16))
    ss = jnp.zeros((CHUNK, 1), F32)
    for p in range(H_D // 2):
        g = (2 * p) // (H_D // G_D)
        sl = slice(128 * p, 128 * (p + 1))
        le, lo_ = DT_LANE + 2 * p, DT_LANE + 2 * p + 1
        dt_pair = jnp.where(lo, _lane_col(dt, le), _lane_col(dt, lo_))
        cum_pair = jnp.where(lo, _lane_col(cum, le), _lane_col(cum, lo_))
        clast = cum_pair[CHUNK - 1:CHUNK, :]
        xs_p = xs[:, sl]
        xdt = xs_p * dt_pair
        s_old = s_scr[p]
        cgb = bcs[:, 256 + 128 * g:256 + 128 * (g + 1)].astype(BF16)
        y = _dot(cgb, s_old.astype(BF16)) * jnp.exp(cum_pair)
        for e in range(2):
            ln = DT_LANE + 2 * p + e
            diff = _lane_col(cum, ln) - cum_t[ln:ln + 1, :]
            att = (att_base[g] * jnp.exp(jnp.where(causal, diff, NEG))).astype(BF16)
            xm = jnp.where(lo if e == 0 else ~lo, xdt, 0.0).astype(BF16)
            y = y + _dot(att, xm)
        y = (y + xs_p * dskip_ref[:, sl]) * _silu(z_ref[:, sl])
        y_scr[:, sl] = y
        ss = ss + jnp.sum(y * y, axis=-1, keepdims=True)
        txdt = (xdt * jnp.exp(clast - cum_pair)).astype(BF16)
        s_scr[p] = jnp.exp(clast) * s_old + _dot(bt[g], txdt)
    inv = lax.rsqrt(ss * (1.0 / DI_D) + EPS)
    od_ref[...] = (y_scr[...] * inv * nw_ref[...]).astype(BF16)

    @pl.when(c == nc - 1)
    def _():
        st_ref[...] = s_scr[...]


def _ssd_prompt(p_main, p_small, cwx, cwbc, cbx, cbbc, dtb, alog, dskip, nw, *, nb, nc):
    m = nb * nc * CHUNK
    row = lambda b, c: b * nc + c
    c2 = lambda b, c: (0, 0)
    return pl.pallas_call(
        _ssd_prompt_kernel,
        grid=(nb, nc),
        in_specs=[
            pl.BlockSpec((CHUNK, 1024), lambda b, c: (row(b, c), 4)),
            pl.BlockSpec((CHUNK, 512), lambda b, c: (row(b, c), 10)),
            pl.BlockSpec((CHUNK, 1024), lambda b, c: (row(b, c), 3)),
            pl.BlockSpec((CHUNK, ODD_SMALL), lambda b, c: (row(b, c), 0)),
            pl.BlockSpec((CONV_W, DI_D), c2), pl.BlockSpec((CONV_W, 512), c2),
            pl.BlockSpec((1, DI_D), c2), pl.BlockSpec((1, 512), c2),
            pl.BlockSpec((1, ODD_SMALL), c2), pl.BlockSpec((1, ODD_SMALL), c2),
            pl.BlockSpec((1, DI_D), c2), pl.BlockSpec((1, DI_D), c2),
        ],
        out_specs=[
            pl.BlockSpec((CHUNK, DI_D), lambda b, c: (row(b, c), 0)),
            pl.BlockSpec((None, 8, 128, 128), lambda b, c: (b, 0, 0, 0)),
            pl.BlockSpec((None, CONV_W - 1, CONV_CH), lambda b, c: (b, 0, 0)),
        ],
        out_shape=[jax.ShapeDtypeStruct((m, DI_D), BF16), jax.ShapeDtypeStruct((nb, 8, 128, 128), F32),
                   jax.ShapeDtypeStruct((nb, CONV_W - 1, CONV_CH), F32)],
        scratch_shapes=[pltpu.VMEM((CHUNK + 8, DI_D), F32), pltpu.VMEM((CHUNK + 8, 512), F32),
                        pltpu.VMEM((8, 128, 128), F32), pltpu.VMEM((CHUNK, DI_D), F32)],
        compiler_params=_cparams(("arbitrary", "arbitrary")),
        name="ssd_prompt",
    )(p_main, p_main, p_main, p_small, cwx, cwbc, cbx, cbbc, dtb, alog, dskip, nw)


def _ssd_sample_kernel(p_ref, dt_ref, cs_ref, s0_ref, cwx_ref, cwbc_ref, cbx_ref, cbbc_ref, dtb_ref, alog_ref,
                       dskip_ref, nw_ref, od_ref, st_ref, cv_ref):
    row = p_ref[...]
    xn = row[:, 4096:5120]
    bcn = row[:, 5120:5632]
    cx = cbx_ref[...] + cwx_ref[3:4, :] * xn
    cbc = cbbc_ref[...] + cwbc_ref[3:4, :] * bcn
    for w in range(CONV_W - 1):
        cx = cx + cwx_ref[w:w + 1, :] * cs_ref[w:w + 1, 0:DI_D]
        cbc = cbc + cwbc_ref[w:w + 1, :] * cs_ref[w:w + 1, DI_D:CONV_CH]
    cv_ref[0:1, :] = cs_ref[1:2, :]
    cv_ref[1:2, :] = cs_ref[2:3, :]
    cv_ref[2:3, 0:DI_D] = xn
    cv_ref[2:3, DI_D:CONV_CH] = bcn
    xs = _silu(cx)
    bcs = _silu(cbc)
    dt = _softplus(dt_ref[...] + dtb_ref[...])
    da = jnp.exp(dt * (-jnp.exp(alog_ref[...])))
    lane = lax.broadcasted_iota(jnp.int32, (1, 128), 1)
    lo = lane < 64
    bcol = [_col_bcast(bcs[:, 128 * g:128 * (g + 1)]) for g in range(G_D)]
    ccol = [_col_bcast(bcs[:, 256 + 128 * g:256 + 128 * (g + 1)]) for g in range(G_D)]
    ys = []
    ss = jnp.zeros((1, 1), F32)
    for p in range(H_D // 2):
        g = (2 * p) // (H_D // G_D)
        sl = slice(128 * p, 128 * (p + 1))
        le, lo_ = DT_LANE + 2 * p, DT_LANE + 2 * p + 1
        dt_pair = jnp.where(lo, _lane_col(dt, le), _lane_col(dt, lo_))
        da_pair = jnp.where(lo, _lane_col(da, le), _lane_col(da, lo_))
        xs_p = xs[:, sl]
        s_new = da_pair * s0_ref[p] + bcol[g] * (xs_p * dt_pair)
        st_ref[p] = s_new
        y = jnp.sum(ccol[g] * s_new, axis=0, keepdims=True)
        y = (y + xs_p * dskip_ref[:, sl]) * _silu(row[:, 3072 + 128 * p:3072 + 128 * (p + 1)])
        ys.append(y)
        ss = ss + jnp.sum(y * y, axis=-1, keepdims=True)
    inv = lax.rsqrt(ss * (1.0 / DI_D) + EPS)
    for p in range(H_D // 2):
        sl = slice(128 * p, 128 * (p + 1))
        od_ref[:, sl] = ys[p] * inv * nw_ref[:, sl]


def _ssd_sample(p_main, p_small, cs, s0, cwx, cwbc, cbx, cbbc, dtb, alog, dskip, nw):
    nb = p_main.shape[0]
    b3 = lambda b: (b, 0, 0)
    b4 = lambda b: (b, 0, 0, 0)
    c2 = lambda b: (0, 0)
    return pl.pallas_call(
        _ssd_sample_kernel,
        grid=(nb,),
        in_specs=[
            pl.BlockSpec((None, 1, ODD_MAIN), b3),
            pl.BlockSpec((None, 1, ODD_SMALL), b3),
            pl.BlockSpec((None, CONV_W - 1, CONV_CH), b3),
            pl.BlockSpec((None, 8, 128, 128), b4),
            pl.BlockSpec((CONV_W, DI_D), c2), pl.BlockSpec((CONV_W, 512), c2),
            pl.BlockSpec((1, DI_D), c2), pl.BlockSpec((1, 512), c2),
            pl.BlockSpec((1, ODD_SMALL), c2), pl.BlockSpec((1, ODD_SMALL), c2),
            pl.BlockSpec((1, DI_D), c2), pl.BlockSpec((1, DI_D), c2),
        ],
        out_specs=[
            pl.BlockSpec((None, 1, DI_D), b3),
            pl.BlockSpec((None, 8, 128, 128), b4),
            pl.BlockSpec((None, CONV_W - 1, CONV_CH), b3),
        ],
        out_shape=[jax.ShapeDtypeStruct((nb, 1, DI_D), F32), jax.ShapeDtypeStruct((nb, 8, 128, 128), F32),
                   jax.ShapeDtypeStruct((nb, CONV_W - 1, CONV_CH), F32)],
        compiler_params=_cparams(("arbitrary",)),
        name="ssd_sample",
    )(p_main.reshape(nb, 1, ODD_MAIN), p_small.reshape(nb, 1, ODD_SMALL), cs, s0,
      cwx, cwbc, cbx, cbbc, dtb, alog, dskip, nw)


def _pairs_to_heads(s):
    b, p, n, _ = s.shape
    return s.reshape(b, p, n, 2, 64).transpose(0, 1, 3, 2, 4).reshape(b, 2 * p, n, 64)


def _heads_to_pairs(s):
    b, h, n, d = s.shape
    return s.reshape(b, h // 2, 2, n, d).transpose(0, 1, 3, 2, 4).reshape(b, h // 2, n, 2 * d)


def _pad_lanes(v, start, width=ODD_SMALL):
    out = jnp.zeros((1, width), F32)
    return lax.dynamic_update_slice(out, v.reshape(1, -1).astype(F32), (0, start))


def kernel(x_prompt, x_sample, state_ret, cache_swa_k, cache_swa_v, cache_fox_k, cache_fox_v, cache_fox_logf,
           state_ssm, state_conv, page_table, norm_mix_pre, norm_mix_post, norm_mlp_pre, norm_mlp_post,
           w_in_even, w_out_even, ret_norm_w, swa_sinks, w_in_odd, w_out_odd, fox_fb, conv_w, conv_b,
           dt_bias, a_log, d_skip, ssd_norm_w, w_up, w_down):
    nb, seq = BATCH, SEQ
    nc = seq // CHUNK
    mp = nb * seq
    ms = DEC_BATCH
    xp = x_prompt.reshape(mp, D_MODEL)
    xs = x_sample.reshape(ms, D_MODEL)
    row = lambda v: v.reshape(1, -1)

    pos_p = jnp.arange(seq, dtype=jnp.int32)
    pos_s = jnp.full((1,), PAST_LEN, dtype=jnp.int32)
    ca_p, sa_p = _rope_tables(pos_p, RET_THETA, 8)
    cb_p, sb_p = _rope_tables(pos_p, ROPE_THETA_B, 2)
    ca_s, sa_s = _rope_tables(pos_s, RET_THETA, 8)
    cb_s, sb_s = _rope_tables(pos_s, ROPE_THETA_B, 2)

    we = w_in_even[0]
    we_main = we[:, :EVEN_MAIN].astype(BF16)
    we_small = we[:, EVEN_MAIN:].astype(BF16)
    wo = w_out_even[0].astype(BF16)
    wo_a, wo_b = wo[:1024], wo[1024:]
    g_pre, g_post = row(norm_mix_pre[0]), row(norm_mix_post[0])
    gm_pre, gm_post = row(norm_mlp_pre[0]), row(norm_mlp_post[0])
    gn_w = row(ret_norm_w[0])
    sinks = swa_sinks[0]

    pm, psm, sm, ssm_, wu_b, wd_b = _proj(xp, xs, g_pre, we_main, we_small, w_up, w_down, 0,
                                          odd=False, tm=256, tn=512, seq=seq)
    out_a, ret_p = _ret_prompt(pm, ca_p, sa_p, gn_w, nb=nb, nc=nc)
    out_b, swak_p, swav_p = _swa_prompt(pm, psm, cb_p, sb_p, sinks, nb=nb, nc=nc)
    mix_s, ret_s, swak_s, swav_s = _even_sample(
        sm, ssm_, state_ret[0].reshape(ms, 4, 128, 128), cache_swa_k[0].reshape(ms, WINDOW, 128),
        cache_swa_v[0].reshape(ms, WINDOW, 128), ca_s, sa_s, cb_s, sb_s, gn_w, sinks)
    mix_s = mix_s.reshape(ms, 2048).astype(BF16)
    xp, xs = _outproj(out_a, out_b, xp, mix_s[:, :1024], mix_s[:, 1024:], xs, wo_a, wo_b, g_post, tm=512)
    xp, xs = _mlp(xp, xs, gm_pre, gm_post, wu_b, wd_b, tm=512, tf=1024)

    wod = w_in_odd[0]
    wod_main = jnp.concatenate([wod[:, :3072], wod[:, 3080:5640]], axis=1).astype(BF16)
    wod_small = jnp.concatenate(
        [wod[:, 3072:3080], wod[:, 5640:5656], jnp.zeros((D_MODEL, ODD_SMALL - 24), F32)], axis=1).astype(BF16)
    wo1 = w_out_odd[0].astype(BF16)
    wo_c, wo_d = wo1[:1024], wo1[1024:]
    g_pre, g_post = row(norm_mix_pre[1]), row(norm_mix_post[1])
    gm_pre, gm_post = row(norm_mlp_pre[1]), row(norm_mlp_post[1])
    fb = _pad_lanes(fox_fb[0], 0)
    dtb = _pad_lanes(dt_bias[0], DT_LANE)
    alog = _pad_lanes(a_log[0], DT_LANE)
    cw = conv_w[0]
    cwx, cwbc = cw[:, :DI_D], cw[:, DI_D:]
    cbx, cbbc = row(conv_b[0][:DI_D]), row(conv_b[0][DI_D:])
    dskip = row(jnp.repeat(d_skip[0], HD_D))
    nw = row(ssd_norm_w[0])

    pm, psm, sm, ssm_, wu_b, wd_b, lf_p, fc_p, lf_s = _proj(
        xp, xs, g_pre, wod_main, wod_small, w_up, w_down, 1, fb, odd=True, tm=256, tn=512, seq=seq)
    fc = fc_p[:, :H_C].reshape(nb, seq, H_C).transpose(0, 2, 1)
    out_c = _fox_prompt(pm, fc[..., None], fc[:, :, None, :], nb=nb, seq=seq, tq=256)
    out_d, ssm_pairs_p, conv_p = _ssd_prompt(pm, psm, cwx, cwbc, cbx, cbbc, dtb, alog, dskip, nw, nb=nb, nc=nc)
    fox_k_p = pm[:, 1024:2048].reshape(1, nb, seq, H_C, HD_C)
    fox_v_p = pm[:, 2048:3072].reshape(1, nb, seq, H_C, HD_C)
    fox_lf_p = lf_p[:, :H_C].reshape(1, nb, seq, H_C)

    q_s = sm[:, 0:1024].reshape(ms, H_C, HD_C)
    k_s = sm[:, 1024:2048].reshape(ms, H_C, HD_C)
    v_s = sm[:, 2048:3072].reshape(ms, H_C, HD_C)
    lf_s8 = lf_s[:, :H_C]
    out_c_s = _fox_decode(page_table, q_s, k_s, v_s, lf_s8.reshape(ms, H_C, 1),
                          cache_fox_k[0], cache_fox_v[0], cache_fox_logf[0].transpose(0, 2, 1), pp=16)
    out_d_s, ssm_pairs_s, conv_s = _ssd_sample(sm, ssm_, state_conv[0], _heads_to_pairs(state_ssm[0]),
                                               cwx, cwbc, cbx, cbbc, dtb, alog, dskip, nw)
    xp, xs = _outproj(out_c, out_d, xp, out_c_s.reshape(ms, 1024).astype(BF16),
                      out_d_s.reshape(ms, DI_D).astype(BF16), xs, wo_c, wo_d, g_post, tm=512)
    xp, xs = _mlp(xp, xs, gm_pre, gm_post, wu_b, wd_b, tm=512, tf=1024)

    return (
        xp.reshape(nb, seq, D_MODEL), xs.reshape(ms, 1, D_MODEL),
        ret_p.reshape(1, nb, H_A, DK_A, DV_A), ret_s.reshape(1, ms, H_A, DK_A, DV_A),
        swak_p.reshape(1, nb, WINDOW, KV_B, HD_B), swav_p.reshape(1, nb, WINDOW, KV_B, HD_B),
        swak_s.reshape(1, ms, WINDOW, KV_B, HD_B), swav_s.reshape(1, ms, WINDOW, KV_B, HD_B),
        fox_k_p, fox_v_p, fox_lf_p,
        k_s.reshape(1, ms, 1, H_C, HD_C), v_s.reshape(1, ms, 1, H_C, HD_C), lf_s8.reshape(1, ms, 1, H_C),
        _pairs_to_heads(ssm_pairs_p)[None], _pairs_to_heads(ssm_pairs_s)[None],
        conv_p[None], conv_s[None],
    )
```

```python
import functools

import numpy as np
import jax
import jax.numpy as jnp
from jax import lax
from jax.experimental import pallas as pl
from jax.experimental.pallas import tpu as pltpu

F32 = jnp.float32
BF16 = jnp.bfloat16

D_MODEL = 2048
BATCH = 4
SEQ = 2048
DEC_BATCH = 32
PAST_LEN = 16384
PAGE_SIZE = 128
D_FF = 4 * D_MODEL
EPS = 1e-6
GN_EPS = 1e-5
CHUNK = 128

H_A, DK_A, DV_A = 8, 64, 128
RET_THETA = 10000.0
H_B, KV_B, HD_B = 16, 2, 64
WINDOW = 128
ROPE_THETA_B = 150000.0
H_C, HD_C = 8, 128
H_D, HD_D, G_D, N_D = 16, 64, 2, 128
CONV_W = 4
DI_D = H_D * HD_D
CONV_CH = DI_D + 2 * G_D * N_D

EVEN_MAIN = 4096
EVEN_SMALL = 256
ODD_MAIN = 5632
ODD_SMALL = 128
DT_LANE = 8

NEG = -1e30
VMEM_LIMIT = 56 * 1024 * 1024
MLP_VMEM_LIMIT = 60 * 1024 * 1024
MLP_SUB_ROWS = 256
MIX_NBAT = 2

_LOG_GAMMA = [float(v) for v in np.log1p(-np.exp2(-5.0 - np.arange(H_A, dtype=np.float64)))]


def _cparams(sem, vmem_limit=VMEM_LIMIT):
    return pltpu.CompilerParams(dimension_semantics=sem, vmem_limit_bytes=vmem_limit)


def _silu(x):
    return x * jax.nn.sigmoid(x)


def _softplus(x):
    return jnp.maximum(x, 0.0) + jnp.log1p(jnp.exp(-jnp.abs(x)))


def _log_sigmoid(x):
    return jnp.minimum(x, 0.0) - jnp.log1p(jnp.exp(-jnp.abs(x)))


def _rms(x, g):
    ms = jnp.mean(x * x, axis=-1, keepdims=True)
    return (x * lax.rsqrt(ms + EPS)) * g


def _dot(a, b):
    return jnp.dot(a, b, preferred_element_type=F32)


def _dot_nt(a, b):
    return lax.dot_general(a, b, (((1,), (1,)), ((), ())), preferred_element_type=F32)


def _split3(x):
    hi = x.astype(BF16)
    r = x - hi.astype(F32)
    mid = r.astype(BF16)
    lo = (r - mid.astype(F32)).astype(BF16)
    return hi, mid, lo


def _exact_left01(m01, x):
    hi, mid, lo = _split3(x)
    return _dot(m01, hi) + _dot(m01, mid) + _dot(m01, lo)


def _exact_right01(x, m01):
    hi, mid, lo = _split3(x)
    return _dot(hi, m01) + _dot(mid, m01) + _dot(lo, m01)


def _tri_lower(n):
    r = lax.broadcasted_iota(jnp.int32, (n, n), 0)
    c = lax.broadcasted_iota(jnp.int32, (n, n), 1)
    return jnp.where(r >= c, 1.0, 0.0).astype(BF16)


def _rope64(x, c, s):
    w = x.shape[-1]
    ax = x.ndim - 1
    lane = lax.broadcasted_iota(jnp.int32, x.shape, ax)
    first = (lane & 32) == 0
    left = pltpu.roll(x, w - 32, axis=ax)
    right = pltpu.roll(x, 32, axis=ax)
    return x * c + jnp.where(first, left, right) * s


def _rope_tables(pos, theta, reps):
    inv = 1.0 / (theta ** (jnp.arange(32, dtype=F32) * (2.0 / 64)))
    ang = pos.astype(F32)[:, None] * inv[None, :]
    cos, sin = jnp.cos(ang), jnp.sin(ang)
    c = jnp.concatenate([cos, cos], axis=-1)
    s = jnp.concatenate([-sin, sin], axis=-1)
    return jnp.tile(c, (1, reps)), jnp.tile(s, (1, reps))


def _proj_kernel(*refs, odd, n_w, tm, tn, seq):
    x_ref, xs_ref, g_ref = refs[0:3]
    w_refs = refs[3:3 + n_w]
    refs = refs[3 + n_w:]
    if odd:
        (wsm_ref, wu_ref, wd_ref, fb_ref,
         o_ref, osm_ref, os_ref, ossm_ref, wub_ref, wdb_ref, lf_ref, fc_ref, lfs_ref, h_scr, carry_scr) = refs
    else:
        (wsm_ref, wu_ref, wd_ref, o_ref, osm_ref, os_ref, ossm_ref, wub_ref, wdb_ref, h_scr) = refs
    i = pl.program_id(0)
    wub_ref[...] = wu_ref[...].astype(BF16)
    wdb_ref[...] = wd_ref[...].astype(BF16)

    def rows(x_r, o_r, osm_r, n):
        h_scr[0:n, :] = _rms(x_r[...], g_ref[...]).astype(BF16)
        sm = _dot(h_scr[0:n, :], wsm_ref[...])
        osm_r[...] = sm
        base = 0
        for w_ref in w_refs:
            for t in range(w_ref.shape[1] // tn):
                o_r[:, base + t * tn:base + (t + 1) * tn] = _dot(h_scr[0:n, :], w_ref[:, t * tn:(t + 1) * tn])
            base += w_ref.shape[1]
        return sm

    sm = rows(x_ref, o_ref, osm_ref, tm)
    if odd:
        lf = _log_sigmoid(sm + fb_ref[...])
        lf_ref[...] = lf

        @pl.when((i * tm) % seq == 0)
        def _():
            carry_scr[...] = jnp.zeros_like(carry_scr)

        f = _exact_left01(_tri_lower(tm), lf) + carry_scr[...]
        fc_ref[...] = f
        carry_scr[...] = f[tm - 1:tm, :]

    @pl.when(i == 0)
    def _():
        sms = rows(xs_ref, os_ref, ossm_ref, xs_ref.shape[0])
        if odd:
            lfs_ref[...] = _log_sigmoid(sms + fb_ref[...])


def _proj(x, xs, g, w_mains, w_small, w_up, w_down, layer, fb=None, *, odd, tm, tn, seq):
    m, ms = x.shape[0], xs.shape[0]
    n_main = sum(w.shape[1] for w in w_mains)
    n_small = w_small.shape[1]
    ni = m // tm
    ru, rd = D_MODEL // ni, D_FF // ni
    c2 = lambda i: (0, 0)
    r2 = lambda i: (i, 0)
    in_specs = [
        pl.BlockSpec((tm, D_MODEL), r2),
        pl.BlockSpec((ms, D_MODEL), c2),
        pl.BlockSpec((1, D_MODEL), c2),
    ] + [pl.BlockSpec((D_MODEL, w.shape[1]), c2, pipeline_mode=pl.Buffered(1)) for w in w_mains] + [
        pl.BlockSpec((D_MODEL, n_small), c2),
        pl.BlockSpec((None, ru, D_FF), lambda i: (layer, i, 0)),
        pl.BlockSpec((None, rd, D_MODEL), lambda i: (layer, i, 0)),
    ]
    args = [x, xs, g, *w_mains, w_small, w_up, w_down]
    small = jax.ShapeDtypeStruct((m, n_small), F32)
    small_s = jax.ShapeDtypeStruct((ms, n_small), F32)
    small_spec = pl.BlockSpec((tm, n_small), r2)
    small_s_spec = pl.BlockSpec((ms, n_small), c2)
    out_shape = [jax.ShapeDtypeStruct((m, n_main), F32), small,
                 jax.ShapeDtypeStruct((ms, n_main), F32), small_s,
                 jax.ShapeDtypeStruct((D_MODEL, D_FF), BF16), jax.ShapeDtypeStruct((D_FF, D_MODEL), BF16)]
    out_specs = [pl.BlockSpec((tm, n_main), r2), small_spec,
                 pl.BlockSpec((ms, n_main), c2), small_s_spec,
                 pl.BlockSpec((ru, D_FF), r2), pl.BlockSpec((rd, D_MODEL), r2)]
    scratch = [pltpu.VMEM((tm, D_MODEL), BF16)]
    if odd:
        in_specs.append(pl.BlockSpec((1, n_small), c2))
        args.append(fb)
        out_shape += [small, small, small_s]
        out_specs += [small_spec, small_spec, small_s_spec]
        scratch.append(pltpu.VMEM((1, n_small), F32))
    return pl.pallas_call(
        functools.partial(_proj_kernel, odd=odd, n_w=len(w_mains), tm=tm, tn=tn, seq=seq),
        grid=(ni,), in_specs=in_specs, out_specs=out_specs, out_shape=out_shape,
        scratch_shapes=scratch, compiler_params=_cparams(("arbitrary",)),
        name="proj_odd" if odd else "proj_even",
    )(*args)


def _outproj_kernel(a_ref, b_ref, x_ref, as_ref, bs_ref, xs_ref, wa_ref, wb_ref, g_ref, o_ref, os_ref):
    g = g_ref[...]
    n = x_ref.shape[0]
    sub = min(n, MLP_SUB_ROWS)
    for r in range(n // sub):
        rows = slice(r * sub, (r + 1) * sub)
        y = _dot(a_ref[rows, :], wa_ref[...]) + _dot(b_ref[rows, :], wb_ref[...])
        o_ref[rows, :] = x_ref[rows, :] + _rms(y, g)

    @pl.when(pl.program_id(0) == 0)
    def _():
        ys = _dot(as_ref[...], wa_ref[...]) + _dot(bs_ref[...], wb_ref[...])
        os_ref[...] = xs_ref[...] + _rms(ys, g)


def _outproj(a, b, x, a_s, b_s, xs, wa, wb, g, *, tm):
    m, ms = x.shape[0], xs.shape[0]
    ka, kb = a.shape[1], b.shape[1]
    c2 = lambda i: (0, 0)
    return pl.pallas_call(
        _outproj_kernel,
        grid=(m // tm,),
        in_specs=[
            pl.BlockSpec((tm, ka), lambda i: (i, 0)),
            pl.BlockSpec((tm, kb), lambda i: (i, 0)),
            pl.BlockSpec((tm, D_MODEL), lambda i: (i, 0)),
            pl.BlockSpec((ms, ka), c2),
            pl.BlockSpec((ms, kb), c2),
            pl.BlockSpec((ms, D_MODEL), c2),
            pl.BlockSpec((ka, D_MODEL), c2),
            pl.BlockSpec((kb, D_MODEL), c2),
            pl.BlockSpec((1, D_MODEL), c2),
        ],
        out_specs=[pl.BlockSpec((tm, D_MODEL), lambda i: (i, 0)), pl.BlockSpec((ms, D_MODEL), c2)],
        out_shape=[jax.ShapeDtypeStruct((m, D_MODEL), F32), jax.ShapeDtypeStruct((ms, D_MODEL), F32)],
        compiler_params=_cparams(("arbitrary",)),
        name="outproj",
    )(a, b, x, a_s, b_s, xs, wa, wb, g)


def _mlp_kernel(x_ref, xs_ref, gpre_ref, gpost_ref, wu_ref, wd_ref, o_ref, os_ref, h_scr, hs_scr):
    i = pl.program_id(0)
    j = pl.program_id(1)
    last = pl.num_programs(1) - 1

    def group(x_r, o_r, h_s):
        @pl.when(j == 0)
        def _():
            h_s[...] = _rms(x_r[...], gpre_ref[...]).astype(BF16)
            o_r[...] = jnp.zeros_like(o_r)

        n = x_r.shape[0]
        sub = min(n, MLP_SUB_ROWS)
        for r in range(n // sub):
            rows = slice(r * sub, (r + 1) * sub)
            u = jnp.maximum(_dot(h_s[rows, :], wu_ref[...]), 0.0)
            o_r[rows, :] += _dot((u * u).astype(BF16), wd_ref[...])

        @pl.when(j == last)
        def _():
            o_r[...] = x_r[...] + _rms(o_r[...], gpost_ref[...])

    group(x_ref, o_ref, h_scr)

    @pl.when(i == 0)
    def _():
        group(xs_ref, os_ref, hs_scr)


def _mlp(x, xs, gpre, gpost, w_up, w_down, *, tm, tf):
    m, ms = x.shape[0], xs.shape[0]
    c2 = lambda i, j: (0, 0)
    return pl.pallas_call(
        _mlp_kernel,
        grid=(m // tm, D_FF // tf),
        in_specs=[
            pl.BlockSpec((tm, D_MODEL), lambda i, j: (i, 0)),
            pl.BlockSpec((ms, D_MODEL), c2),
            pl.BlockSpec((1, D_MODEL), c2),
            pl.BlockSpec((1, D_MODEL), c2),
            pl.BlockSpec((D_MODEL, tf), lambda i, j: (0, j)),
            pl.BlockSpec((tf, D_MODEL), lambda i, j: (j, 0)),
        ],
        out_specs=[pl.BlockSpec((tm, D_MODEL), lambda i, j: (i, 0)), pl.BlockSpec((ms, D_MODEL), c2)],
        out_shape=[jax.ShapeDtypeStruct((m, D_MODEL), F32), jax.ShapeDtypeStruct((ms, D_MODEL), F32)],
        scratch_shapes=[pltpu.VMEM((tm, D_MODEL), BF16), pltpu.VMEM((ms, D_MODEL), BF16)],
        compiler_params=_cparams(("arbitrary", "arbitrary"), MLP_VMEM_LIMIT),
        name="mlp",
    )(x, xs, gpre, gpost, w_up, w_down)


def _group_norm_gate(y, gate, gw):
    mu = jnp.mean(y, axis=-1, keepdims=True)
    d = y - mu
    var = jnp.mean(d * d, axis=-1, keepdims=True)
    return _silu(gate) * (d * lax.rsqrt(var + GN_EPS) * gw)


def _ret_prompt_kernel(q_all, k_all, v_all, g_all, cos_ref, sin_ref, gn_ref, oa_all, st_all,
                       s_all, d_scr, e_scr, t_scr):
    b = pl.program_id(0)
    c = pl.program_id(1)
    t_i = lax.broadcasted_iota(jnp.int32, (CHUNK, CHUNK), 0)
    s_i = lax.broadcasted_iota(jnp.int32, (CHUNK, CHUNK), 1)
    lo = s_i < 64

    @pl.when((b == 0) & (c == 0))
    def _():
        tf = t_i.astype(F32)
        sf = s_i.astype(F32)
        for h in range(H_A):
            d_scr[h] = jnp.where(t_i >= s_i, jnp.exp((tf - sf) * _LOG_GAMMA[h]), 0.0)
            e_scr[h] = jnp.exp((tf + 1.0) * _LOG_GAMMA[h])
        for p in range(H_A // 2):
            lg = jnp.where(lo, _LOG_GAMMA[2 * p], _LOG_GAMMA[2 * p + 1])
            t_scr[p] = jnp.exp((CHUNK - 1.0 - tf) * lg)

    @pl.when(c == 0)
    def _():
        s_all[...] = jnp.zeros_like(s_all)

    cos, sin = cos_ref[...], sin_ref[...]
    top = t_i < 64
    for u in range(MIX_NBAT):
        v_ref, g_ref, oa_ref, s_scr = v_all.at[u], g_all.at[u], oa_all.at[u], s_all.at[u]
        qr = _rope64(q_all[u], cos, sin)
        kr = _rope64(k_all[u], cos, sin) * (DK_A ** -0.5)
        for p in range(H_A // 2):
            sl = slice(128 * p, 128 * (p + 1))
            qp, kp = qr[:, sl], kr[:, sl]
            kb = kp.astype(BF16)
            s_old = s_scr[p]
            s_old_b = s_old.astype(BF16)
            ktt = jnp.transpose(kp * t_scr[p]).astype(BF16)
            upd = []
            for e in range(2):
                h = 2 * p + e
                hs = slice(128 * h, 128 * (h + 1))
                qm = jnp.where(lo if e == 0 else ~lo, qp, 0.0).astype(BF16)
                vh = v_ref[:, hs].astype(BF16)
                att = (_dot_nt(qm, kb) * d_scr[h]).astype(BF16)
                y = _dot(att, vh) + _dot(qm, s_old_b) * e_scr[h]
                oa_ref[:, hs] = _group_norm_gate(y, g_ref[:, hs], gn_ref[:, hs]).astype(BF16)
                upd.append(_dot(ktt, vh))
            g128 = jnp.where(top, float(np.exp(CHUNK * _LOG_GAMMA[2 * p])),
                             float(np.exp(CHUNK * _LOG_GAMMA[2 * p + 1])))
            s_scr[p] = g128 * s_old + jnp.where(top, upd[0], upd[1])

    @pl.when(c == pl.num_programs(1) - 1)
    def _():
        st_all[...] = s_all[...]


def _batch_groups(a, nb, seq):
    return a.reshape(nb // MIX_NBAT, MIX_NBAT, seq, a.shape[-1])


def _ret_prompt(p_main, cos, sin, gn_w, *, nb, nc):
    seq = nc * CHUNK
    pg = _batch_groups(p_main, nb, seq)
    blk = lambda w, col: pl.BlockSpec((None, MIX_NBAT, CHUNK, w), lambda b, c: (b, 0, c, col))
    oa, st = pl.pallas_call(
        _ret_prompt_kernel,
        grid=(nb // MIX_NBAT, nc),
        in_specs=[
            blk(512, 0), blk(512, 1), blk(1024, 1), blk(1024, 2),
            pl.BlockSpec((CHUNK, 512), lambda b, c: (c, 0)),
            pl.BlockSpec((CHUNK, 512), lambda b, c: (c, 0)),
            pl.BlockSpec((1, 1024), lambda b, c: (0, 0)),
        ],
        out_specs=[
            blk(1024, 0),
            pl.BlockSpec((None, MIX_NBAT, 4, 128, 128), lambda b, c: (b, 0, 0, 0, 0)),
        ],
        out_shape=[jax.ShapeDtypeStruct((nb // MIX_NBAT, MIX_NBAT, seq, 1024), BF16),
                   jax.ShapeDtypeStruct((nb // MIX_NBAT, MIX_NBAT, 4, 128, 128), F32)],
        scratch_shapes=[pltpu.VMEM((MIX_NBAT, 4, 128, 128), F32), pltpu.VMEM((H_A, 128, 128), F32),
                        pltpu.VMEM((H_A, 128, 128), F32), pltpu.VMEM((4, 128, 128), F32)],
        compiler_params=_cparams(("arbitrary", "arbitrary")),
        name="ret_prompt",
    )(pg, pg, pg, pg, cos, sin, gn_w)
    return oa.reshape(nb * seq, 1024), st.reshape(nb, 4, 128, 128)


def _swa_padded(x2):
    lane = lax.broadcasted_iota(jnp.int32, x2.shape, 1)
    lo = lane < 64
    xr = pltpu.roll(x2, 64, axis=1)
    z = jnp.zeros_like(x2)
    return {
        (0, 0): jnp.where(lo, x2, z).astype(BF16), (0, 1): jnp.where(lo, z, xr).astype(BF16),
        (1, 0): jnp.where(lo, xr, z).astype(BF16), (1, 1): jnp.where(lo, z, x2).astype(BF16),
    }


def _swa_attend(q_pair_fn, kpad, vpad, valid, sink_ref, store_fn, rows):
    per_group = (H_B // KV_B) // 2
    for g in range(KV_B):
        pairs = list(range(g * per_group, (g + 1) * per_group))
        qs = jnp.concatenate([q_pair_fn(jj) for jj in pairs], axis=0).astype(BF16)
        acc = None
        for e in range(2):
            s = _dot_nt(qs, kpad[(g, e)])
            if valid is not None:
                s = jnp.where(valid, s, NEG)
            sink = jnp.concatenate([jnp.full((rows, 1), sink_ref[2 * jj + e], F32) for jj in pairs], axis=0)
            mx = jnp.maximum(jnp.max(s, axis=-1, keepdims=True), sink)
            pr = jnp.exp(s - mx)
            den = jnp.sum(pr, axis=-1, keepdims=True) + jnp.exp(sink - mx)
            o = _dot(pr.astype(BF16), vpad[(g, e)]) / den
            acc = o if acc is None else acc + o
        for t, jj in enumerate(pairs):
            store_fn(jj, acc[t * rows:(t + 1) * rows])


def _swa_prompt_kernel(sink_ref, q_all, kvc_all, kvp_all, cc_ref, sc_ref, cp_ref, sp_ref,
                       ob_all, ko_all, vo_all):
    n = pl.program_id(1)
    cc, sc = cc_ref[...], sc_ref[...]
    stacked = WINDOW * (H_B // KV_B) // 2
    i = lax.broadcasted_iota(jnp.int32, (stacked, 2 * WINDOW), 0) & (WINDOW - 1)
    j = lax.broadcasted_iota(jnp.int32, (stacked, 2 * WINDOW), 1)
    valid = (j >= i + 1) & (j <= i + WINDOW) & ((n > 0) | (j >= WINDOW))
    for u in range(MIX_NBAT):
        q_ref, ob_ref = q_all.at[u], ob_all.at[u]
        kvc, kvp = kvc_all[u], kvp_all[u]
        kc = _rope64(kvc[:, :128], cc, sc)
        kp = _rope64(kvp[:, :128], cp_ref[...], sp_ref[...])
        vc = kvc[:, 128:]
        kpad = _swa_padded(jnp.concatenate([kp, kc], axis=0))
        vpad = _swa_padded(jnp.concatenate([kvp[:, 128:], vc], axis=0))

        def q_pair(jj, q_ref=q_ref):
            return _rope64(q_ref[:, 128 * jj:128 * (jj + 1)], cc, sc) * (HD_B ** -0.5)

        def store(jj, acc, ob_ref=ob_ref):
            ob_ref[:, 128 * jj:128 * (jj + 1)] = acc.astype(BF16)

        _swa_attend(q_pair, kpad, vpad, valid, sink_ref, store, WINDOW)

        @pl.when(n == pl.num_programs(1) - 1)
        def _(u=u, kc=kc, vc=vc):
            ko_all[u] = kc
            vo_all[u] = vc


def _swa_prompt(p_main, p_small, cos, sin, sinks, *, nb, nc):
    seq = nc * CHUNK
    ng = nb // MIX_NBAT
    pg, psg = _batch_groups(p_main, nb, seq), _batch_groups(p_small, nb, seq)
    prev = lambda n: jnp.maximum(n - 1, 0)
    win = pl.BlockSpec((None, MIX_NBAT, WINDOW, 128), lambda b, n: (b, 0, 0, 0))
    ob, ko, vo = pl.pallas_call(
        _swa_prompt_kernel,
        grid=(ng, nc),
        in_specs=[
            pl.BlockSpec(memory_space=pltpu.SMEM),
            pl.BlockSpec((None, MIX_NBAT, CHUNK, 1024), lambda b, n: (b, 0, n, 3)),
            pl.BlockSpec((None, MIX_NBAT, CHUNK, 256), lambda b, n: (b, 0, n, 0)),
            pl.BlockSpec((None, MIX_NBAT, CHUNK, 256), lambda b, n: (b, 0, prev(n), 0)),
            pl.BlockSpec((CHUNK, 128), lambda b, n: (n, 0)),
            pl.BlockSpec((CHUNK, 128), lambda b, n: (n, 0)),
            pl.BlockSpec((CHUNK, 128), lambda b, n: (prev(n), 0)),
            pl.BlockSpec((CHUNK, 128), lambda b, n: (prev(n), 0)),
        ],
        out_specs=[pl.BlockSpec((None, MIX_NBAT, CHUNK, 1024), lambda b, n: (b, 0, n, 0)), win, win],
        out_shape=[jax.ShapeDtypeStruct((ng, MIX_NBAT, seq, 1024), BF16),
                   jax.ShapeDtypeStruct((ng, MIX_NBAT, WINDOW, 128), F32),
                   jax.ShapeDtypeStruct((ng, MIX_NBAT, WINDOW, 128), F32)],
        compiler_params=_cparams(("arbitrary", "arbitrary")),
        name="swa_prompt",
    )(sinks, pg, psg, psg, cos, sin, cos, sin)
    return ob.reshape(nb * seq, 1024), ko.reshape(nb, WINDOW, 128), vo.reshape(nb, WINDOW, 128)


def _col_bcast(row128):
    return jnp.transpose(jnp.broadcast_to(row128, (128, 128)))


def _even_sample_kernel(sink_ref, p_ref, ps_ref, s0_ref, kbuf_ref, vbuf_ref, ca_ref, sa_ref, cb_ref, sb_ref,
                        gn_ref, mix_ref, st_ref, ko_ref, vo_ref):
    row = p_ref[...]
    ca, sa = ca_ref[...], sa_ref[...]
    qr = _rope64(jnp.broadcast_to(row[:, 0:512], (8, 512)), ca, sa)
    kr = _rope64(jnp.broadcast_to(row[:, 512:1024], (8, 512)), ca, sa) * (DK_A ** -0.5)
    r_i = lax.broadcasted_iota(jnp.int32, (128, 128), 0)
    top = r_i < 64
    for p in range(H_A // 2):
        sl = slice(128 * p, 128 * (p + 1))
        kcol = _col_bcast(kr[0:1, sl])
        qcol = _col_bcast(qr[0:1, sl])
        he, ho = 2 * p, 2 * p + 1
        v_e = row[:, 1024 + 128 * he:1024 + 128 * (he + 1)]
        v_o = row[:, 1024 + 128 * ho:1024 + 128 * (ho + 1)]
        gam = jnp.where(top, float(np.exp(_LOG_GAMMA[he])), float(np.exp(_LOG_GAMMA[ho])))
        s_new = gam * s0_ref[p] + kcol * jnp.where(top, v_e, v_o)
        st_ref[p] = s_new
        prod = qcol * s_new
        for e, h in ((0, he), (1, ho)):
            y = jnp.sum(jnp.where(top if e == 0 else ~top, prod, 0.0), axis=0, keepdims=True)
            hs = slice(128 * h, 128 * (h + 1))
            gate = row[:, 2048 + 128 * h:2048 + 128 * (h + 1)]
            mix_ref[:, hs] = _group_norm_gate(y, gate, gn_ref[:, hs])

    cb, sb = cb_ref[...], sb_ref[...]
    ps = ps_ref[...]
    kn = _rope64(jnp.broadcast_to(ps[:, 0:128], (8, 128)), cb, sb)[0:1]
    vn = ps[:, 128:256]
    last = r_i == WINDOW - 1
    k_new = jnp.where(last, kn, pltpu.roll(kbuf_ref[...], WINDOW - 1, axis=0))
    v_new = jnp.where(last, vn, pltpu.roll(vbuf_ref[...], WINDOW - 1, axis=0))
    ko_ref[...] = k_new
    vo_ref[...] = v_new
    kpad = _swa_padded(k_new)
    vpad = _swa_padded(v_new)

    def q_pair(jj):
        q8 = jnp.broadcast_to(row[:, 3072 + 128 * jj:3072 + 128 * (jj + 1)], (8, 128))
        return _rope64(q8, cb, sb) * (HD_B ** -0.5)

    def store(jj, acc):
        mix_ref[:, 1024 + 128 * jj:1024 + 128 * (jj + 1)] = acc[0:1]

    _swa_attend(q_pair, kpad, vpad, None, sink_ref, store, 8)


def _even_sample(p_main, p_small, s0, kbuf, vbuf, ca, sa, cb, sb, gn_w, sinks):
    nb = p_main.shape[0]
    b3 = lambda b: (b, 0, 0)
    b4 = lambda b: (b, 0, 0, 0)
    c2 = lambda b: (0, 0)
    return pl.pallas_call(
        _even_sample_kernel,
        grid=(nb,),
        in_specs=[
            pl.BlockSpec(memory_space=pltpu.SMEM),
            pl.BlockSpec((None, 1, EVEN_MAIN), b3),
            pl.BlockSpec((None, 1, EVEN_SMALL), b3),
            pl.BlockSpec((None, 4, 128, 128), b4),
            pl.BlockSpec((None, WINDOW, 128), b3),
            pl.BlockSpec((None, WINDOW, 128), b3),
            pl.BlockSpec((1, 512), c2), pl.BlockSpec((1, 512), c2),
            pl.BlockSpec((1, 128), c2), pl.BlockSpec((1, 128), c2),
            pl.BlockSpec((1, 1024), c2),
        ],
        out_specs=[
            pl.BlockSpec((None, 1, 2048), b3),
            pl.BlockSpec((None, 4, 128, 128), b4),
            pl.BlockSpec((None, WINDOW, 128), b3),
            pl.BlockSpec((None, WINDOW, 128), b3),
        ],
        out_shape=[jax.ShapeDtypeStruct((nb, 1, 2048), F32), jax.ShapeDtypeStruct((nb, 4, 128, 128), F32),
                   jax.ShapeDtypeStruct((nb, WINDOW, 128), F32), jax.ShapeDtypeStruct((nb, WINDOW, 128), F32)],
        compiler_params=_cparams(("arbitrary",)),
        name="even_sample",
    )(sinks, p_main.reshape(nb, 1, EVEN_MAIN), p_small.reshape(nb, 1, EVEN_SMALL), s0, kbuf, vbuf,
      ca, sa, cb, sb, gn_w)


def _fox_q_tile(qi, q_ref, fcol_ref, frow_ref, kb_scr, vb_scr, o_ref, tq):
    r_i = lax.broadcasted_iota(jnp.int32, (tq, tq), 0)
    c_i = lax.broadcasted_iota(jnp.int32, (tq, tq), 1)
    q = (q_ref[...] * (HD_C ** -0.5)).astype(BF16)
    fq = fcol_ref[...]
    m = l = acc = None
    for j in range(qi + 1):
        cols = slice(j * tq, (j + 1) * tq)
        s = _dot_nt(q, kb_scr[cols, :]) + (fq - frow_ref[:, cols])
        if j == qi:
            s = jnp.where(c_i <= r_i, s, NEG)
        mx = jnp.max(s, axis=-1, keepdims=True)
        if j == 0:
            m = mx
            p = jnp.exp(s - m)
            l = jnp.sum(p, axis=-1, keepdims=True)
            acc = _dot(p.astype(BF16), vb_scr[cols, :])
        else:
            m_new = jnp.maximum(m, mx)
            a = jnp.exp(m - m_new)
            p = jnp.exp(s - m_new)
            l = a * l + jnp.sum(p, axis=-1, keepdims=True)
            acc = a * acc + _dot(p.astype(BF16), vb_scr[cols, :])
            m = m_new
    o_ref[...] = (acc / l).astype(BF16)


def _fox_kernel(*refs, pp, tq):
    pt_ref = refs[0]
    q_ref, kn_ref, vn_ref, lfn_ref = refs[1:5]
    k_refs = refs[5:5 + pp]
    v_refs = refs[5 + pp:5 + 2 * pp]
    lf_refs = refs[5 + 2 * pp:5 + 3 * pp]
    pq_ref, pk_ref, pv_ref, fcol_ref, frow_ref = refs[5 + 3 * pp:10 + 3 * pp]
    o_ref, po_ref = refs[10 + 3 * pp:12 + 3 * pp]
    m_scr, l_scr, acc_scr, carry_scr, rexp_scr, tri_scr, kb_scr, vb_scr = refs[12 + 3 * pp:]
    del pt_ref
    b = pl.program_id(0)
    s_id = pl.program_id(1)
    rows = PAGE_SIZE * H_C

    @pl.when(s_id == 0)
    def _():
        kb_scr[...] = pk_ref[...].astype(BF16)
        vb_scr[...] = pv_ref[...].astype(BF16)

    for qi in range(kb_scr.shape[0] // tq):
        @pl.when(s_id == qi)
        def _(qi=qi):
            _fox_q_tile(qi, pq_ref, fcol_ref, frow_ref, kb_scr, vb_scr, po_ref, tq)

    @pl.when((b == 0) & (s_id == 0))
    def _():
        r = lax.broadcasted_iota(jnp.int32, (PAGE_SIZE, rows), 0)
        c = lax.broadcasted_iota(jnp.int32, (PAGE_SIZE, rows), 1)
        rexp_scr[...] = jnp.where((c >> 3) == r, 1.0, 0.0).astype(BF16)
        r2 = lax.broadcasted_iota(jnp.int32, (PAGE_SIZE, PAGE_SIZE), 0)
        c2 = lax.broadcasted_iota(jnp.int32, (PAGE_SIZE, PAGE_SIZE), 1)
        tri_scr[...] = jnp.where(r2 <= c2, 1.0, 0.0).astype(BF16)

    @pl.when(s_id == 0)
    def _():
        m_scr[...] = jnp.full_like(m_scr, NEG)
        l_scr[...] = jnp.zeros_like(l_scr)
        acc_scr[...] = jnp.zeros_like(acc_scr)
        carry_scr[...] = jnp.zeros_like(carry_scr)

    qs = q_ref[...] * (HD_C ** -0.5)
    qb = qs.astype(BF16)
    h_i = lax.broadcasted_iota(jnp.int32, (H_C, rows), 0)
    c_i = lax.broadcasted_iota(jnp.int32, (H_C, rows), 1)
    diag = (c_i & (H_C - 1)) == h_i
    m, l, acc, carry = m_scr[...], l_scr[...], acc_scr[...], carry_scr[...]

    lf_all = jnp.concatenate([lf_refs[r][...] for r in range(pp)], axis=0)
    cum_in = _exact_right01(lf_all, tri_scr[...])
    offs = []
    for r in range(pp):
        offs.append(carry)
        carry = carry + cum_in[H_C * r:H_C * (r + 1), PAGE_SIZE - 1:PAGE_SIZE]
    gexp = _exact_right01(cum_in + jnp.concatenate(offs, axis=0), rexp_scr[...])

    logits = []
    for r in range(pp):
        k2 = k_refs[r][...].reshape(rows, HD_C).astype(BF16)
        logits.append(jnp.where(diag, _dot_nt(qb, k2) - gexp[H_C * r:H_C * (r + 1)], NEG))
    mx = logits[0]
    for r in range(1, pp):
        mx = jnp.maximum(mx, logits[r])
    m_new = jnp.maximum(m, jnp.max(mx, axis=-1, keepdims=True))
    a = jnp.exp(m - m_new)
    psum = pv = None
    for r in range(pp):
        p = jnp.exp(logits[r] - m_new)
        o = _dot(p.astype(BF16), v_refs[r][...].reshape(rows, HD_C).astype(BF16))
        psum = p if psum is None else psum + p
        pv = o if pv is None else pv + o
    l = a * l + jnp.sum(psum, axis=-1, keepdims=True)
    acc = a * acc + pv
    m = m_new
    m_scr[...] = m
    l_scr[...] = l
    acc_scr[...] = acc
    carry_scr[...] = carry

    @pl.when(s_id == pl.num_programs(1) - 1)
    def _():
        fq = carry + lfn_ref[...]
        s_new = jnp.sum(qs * kn_ref[...], axis=-1, keepdims=True)
        m_past = m + fq
        mx = jnp.maximum(m_past, s_new)
        wp = jnp.exp(m_past - mx)
        wn = jnp.exp(s_new - mx)
        o_ref[...] = (acc * wp + wn * vn_ref[...]) / (l * wp + wn)


def _fox(page_table, q, k_new, v_new, lf_new, cache_k, cache_v, cache_lf, p_main, fcol, frow, *, pp, nb_p, seq):
    nb, n_pages = page_table.shape
    ns = n_pages // pp
    tq = seq // ns
    assert nb == nb_p * H_C and ns * tq == seq and ns * pp == n_pages
    b3 = lambda b, s, pt: (b, 0, 0)
    kv_spec = lambda r: pl.BlockSpec((None, PAGE_SIZE, H_C, HD_C), lambda b, s, pt: (pt[b, s * pp + r], 0, 0, 0))
    lf_spec = lambda r: pl.BlockSpec((None, H_C, PAGE_SIZE), lambda b, s, pt: (pt[b, s * pp + r], 0, 0))
    in_specs = [pl.BlockSpec((None, H_C, HD_C), b3), pl.BlockSpec((None, H_C, HD_C), b3),
                pl.BlockSpec((None, H_C, HD_C), b3), pl.BlockSpec((None, H_C, 1), b3)]
    in_specs += [kv_spec(r) for r in range(pp)] + [kv_spec(r) for r in range(pp)] + [lf_spec(r) for r in range(pp)]
    in_specs += [
        pl.BlockSpec((tq, HD_C), lambda b, s, pt: ((b // H_C) * ns + s, b % H_C)),
        pl.BlockSpec((seq, HD_C), lambda b, s, pt: (b // H_C, H_C + b % H_C)),
        pl.BlockSpec((seq, HD_C), lambda b, s, pt: (b // H_C, 2 * H_C + b % H_C)),
        pl.BlockSpec((None, None, tq, 1), lambda b, s, pt: (b // H_C, b % H_C, s, 0)),
        pl.BlockSpec((None, None, 1, seq), lambda b, s, pt: (b // H_C, b % H_C, 0, 0)),
    ]
    grid_spec = pltpu.PrefetchScalarGridSpec(
        num_scalar_prefetch=1, grid=(nb, ns), in_specs=in_specs,
        out_specs=[pl.BlockSpec((None, H_C, HD_C), b3),
                   pl.BlockSpec((tq, HD_C), lambda b, s, pt: ((b // H_C) * ns + s, b % H_C))],
        scratch_shapes=[pltpu.VMEM((H_C, 1), F32), pltpu.VMEM((H_C, 1), F32), pltpu.VMEM((H_C, HD_C), F32),
                        pltpu.VMEM((H_C, 1), F32), pltpu.VMEM((PAGE_SIZE, PAGE_SIZE * H_C), BF16),
                        pltpu.VMEM((PAGE_SIZE, PAGE_SIZE), BF16),
                        pltpu.VMEM((seq, HD_C), BF16), pltpu.VMEM((seq, HD_C), BF16)],
    )
    return pl.pallas_call(
        functools.partial(_fox_kernel, pp=pp, tq=tq),
        grid_spec=grid_spec,
        out_shape=[jax.ShapeDtypeStruct((nb, H_C, HD_C), F32),
                   jax.ShapeDtypeStruct((nb_p * seq, H_C * HD_C), BF16)],
        compiler_params=_cparams(("arbitrary", "arbitrary")),
        name="fox",
    )(page_table, q, k_new, v_new, lf_new, *([cache_k] * pp), *([cache_v] * pp), *([cache_lf] * pp),
      p_main, p_main, p_main, fcol, frow)


def _lane_col(x, lane):
    return x[:, lane:lane + 1]


def _ssd_prompt_kernel(x_all, bc_all, z_all, dt_all, cwx_ref, cwbc_ref, cbx_ref, cbbc_ref, dtb_ref, alog_ref,
                       dskip_ref, nw_ref, od_all, st_all, cv_all, xpx_all, xpbc_all, s_all, y_all):
    c = pl.program_id(1)
    nc = pl.num_programs(1)

    @pl.when(c == 0)
    def _():
        xpx_all[:, 0:8, :] = jnp.zeros((MIX_NBAT, 8, DI_D), F32)
        xpbc_all[:, 0:8, :] = jnp.zeros((MIX_NBAT, 8, 512), F32)
        s_all[...] = jnp.zeros_like(s_all)

    for u in range(MIX_NBAT):
        _ssd_chunk(x_all.at[u], bc_all.at[u], z_all.at[u], dt_all.at[u], cwx_ref, cwbc_ref, cbx_ref, cbbc_ref,
                   dtb_ref, alog_ref, dskip_ref, nw_ref, od_all.at[u], cv_all.at[u],
                   xpx_all.at[u], xpbc_all.at[u], s_all.at[u], y_all.at[u])

    @pl.when(c == nc - 1)
    def _():
        st_all[...] = s_all[...]


def _ssd_chunk(x_ref, bc_ref, z_ref, dt_ref, cwx_ref, cwbc_ref, cbx_ref, cbbc_ref, dtb_ref, alog_ref,
               dskip_ref, nw_ref, od_ref, cv_ref, xpx_scr, xpbc_scr, s_scr, y_scr):
    xpx_scr[8:8 + CHUNK, :] = x_ref[...]
    xpbc_scr[8:8 + CHUNK, :] = bc_ref[...]
    cx = cbx_ref[...]
    cbc = cbbc_ref[...]
    for k in range(CONV_W):
        w = CONV_W - 1 - k
        cx = cx + cwx_ref[w:w + 1, :] * xpx_scr[8 - k:8 - k + CHUNK, :]
        cbc = cbc + cwbc_ref[w:w + 1, :] * xpbc_scr[8 - k:8 - k + CHUNK, :]

    cv_ref[:, 0:DI_D] = xpx_scr[CHUNK + 5:CHUNK + 8, :]
    cv_ref[:, DI_D:CONV_CH] = xpbc_scr[CHUNK + 5:CHUNK + 8, :]
    xpx_scr[0:8, :] = xpx_scr[CHUNK:CHUNK + 8, :]
    xpbc_scr[0:8, :] = xpbc_scr[CHUNK:CHUNK + 8, :]

    xs = _silu(cx)
    bcs = _silu(cbc)
    dt = _softplus(dt_ref[...] + dtb_ref[...])
    la = dt * (-jnp.exp(alog_ref[...]))
    cum = _exact_left01(_tri_lower(CHUNK), la)
    cum_t = jnp.transpose(cum)
    t_i = lax.broadcasted_iota(jnp.int32, (CHUNK, CHUNK), 0)
    s_i = lax.broadcasted_iota(jnp.int32, (CHUNK, CHUNK), 1)
    causal = t_i >= s_i
    lo = s_i < 64
    att_base, bt = [], []
    for g in range(G_D):
        bg = bcs[:, 128 * g:128 * (g + 1)]
        cg = bcs[:, 256 + 128 * g:256 + 128 * (g + 1)]
        att_base.append(_dot_nt(cg.astype(BF16), bg.astype(BF16)))
        bt.append(jnp.transpose(bg).astype(BF16))
    ss = jnp.zeros((CHUNK, 1), F32)
    for p in range(H_D // 2):
        g = (2 * p) // (H_D // G_D)
        sl = slice(128 * p, 128 * (p + 1))
        le, lo_ = DT_LANE + 2 * p, DT_LANE + 2 * p + 1
        dt_pair = jnp.where(lo, _lane_col(dt, le), _lane_col(dt, lo_))
        cum_pair = jnp.where(lo, _lane_col(cum, le), _lane_col(cum, lo_))
        clast = cum_pair[CHUNK - 1:CHUNK, :]
        xs_p = xs[:, sl]
        xdt = xs_p * dt_pair
        s_old = s_scr[p]
        cgb = bcs[:, 256 + 128 * g:256 + 128 * (g + 1)].astype(BF16)
        y = _dot(cgb, s_old.astype(BF16)) * jnp.exp(cum_pair)
        for e in range(2):
            ln = DT_LANE + 2 * p + e
            diff = _lane_col(cum, ln) - cum_t[ln:ln + 1, :]
            att = (att_base[g] * jnp.exp(jnp.where(causal, diff, NEG))).astype(BF16)
            xm = jnp.where(lo if e == 0 else ~lo, xdt, 0.0).astype(BF16)
            y = y + _dot(att, xm)
        y = (y + xs_p * dskip_ref[:, sl]) * _silu(z_ref[:, sl])
        y_scr[:, sl] = y
        ss = ss + jnp.sum(y * y, axis=-1, keepdims=True)
        txdt = (xdt * jnp.exp(clast - cum_pair)).astype(BF16)
        s_scr[p] = jnp.exp(clast) * s_old + _dot(bt[g], txdt)
    inv = lax.rsqrt(ss * (1.0 / DI_D) + EPS)
    od_ref[...] = (y_scr[...] * inv * nw_ref[...]).astype(BF16)


def _ssd_prompt(p_main, p_small, cwx, cwbc, cbx, cbbc, dtb, alog, dskip, nw, *, nb, nc):
    seq = nc * CHUNK
    ng = nb // MIX_NBAT
    pg, psg = _batch_groups(p_main, nb, seq), _batch_groups(p_small, nb, seq)
    blk = lambda w, col: pl.BlockSpec((None, MIX_NBAT, CHUNK, w), lambda b, c: (b, 0, c, col))
    c2 = lambda b, c: (0, 0)
    od, st, cv = pl.pallas_call(
        _ssd_prompt_kernel,
        grid=(ng, nc),
        in_specs=[
            blk(1024, 4), blk(512, 10), blk(1024, 3), blk(ODD_SMALL, 0),
            pl.BlockSpec((CONV_W, DI_D), c2), pl.BlockSpec((CONV_W, 512), c2),
            pl.BlockSpec((1, DI_D), c2), pl.BlockSpec((1, 512), c2),
            pl.BlockSpec((1, ODD_SMALL), c2), pl.BlockSpec((1, ODD_SMALL), c2),
            pl.BlockSpec((1, DI_D), c2), pl.BlockSpec((1, DI_D), c2),
        ],
        out_specs=[
            blk(DI_D, 0),
            pl.BlockSpec((None, MIX_NBAT, 8, 128, 128), lambda b, c: (b, 0, 0, 0, 0)),
            pl.BlockSpec((None, MIX_NBAT, CONV_W - 1, CONV_CH), lambda b, c: (b, 0, 0, 0)),
        ],
        out_shape=[jax.ShapeDtypeStruct((ng, MIX_NBAT, seq, DI_D), BF16),
                   jax.ShapeDtypeStruct((ng, MIX_NBAT, 8, 128, 128), F32),
                   jax.ShapeDtypeStruct((ng, MIX_NBAT, CONV_W - 1, CONV_CH), F32)],
        scratch_shapes=[pltpu.VMEM((MIX_NBAT, CHUNK + 8, DI_D), F32), pltpu.VMEM((MIX_NBAT, CHUNK + 8, 512), F32),
                        pltpu.VMEM((MIX_NBAT, 8, 128, 128), F32), pltpu.VMEM((MIX_NBAT, CHUNK, DI_D), F32)],
        compiler_params=_cparams(("arbitrary", "arbitrary")),
        name="ssd_prompt",
    )(pg, pg, pg, psg, cwx, cwbc, cbx, cbbc, dtb, alog, dskip, nw)
    return od.reshape(nb * seq, DI_D), st.reshape(nb, 8, 128, 128), cv.reshape(nb, CONV_W - 1, CONV_CH)


def _ssd_sample_kernel(p_ref, dt_ref, cs_ref, s0_ref, cwx_ref, cwbc_ref, cbx_ref, cbbc_ref, dtb_ref, alog_ref,
                       dskip_ref, nw_ref, od_ref, st_ref, cv_ref):
    row = p_ref[...]
    xn = row[:, 4096:5120]
    bcn = row[:, 5120:5632]
    cx = cbx_ref[...] + cwx_ref[3:4, :] * xn
    cbc = cbbc_ref[...] + cwbc_ref[3:4, :] * bcn
    for w in range(CONV_W - 1):
        cx = cx + cwx_ref[w:w + 1, :] * cs_ref[w:w + 1, 0:DI_D]
        cbc = cbc + cwbc_ref[w:w + 1, :] * cs_ref[w:w + 1, DI_D:CONV_CH]
    cv_ref[0:1, :] = cs_ref[1:2, :]
    cv_ref[1:2, :] = cs_ref[2:3, :]
    cv_ref[2:3, 0:DI_D] = xn
    cv_ref[2:3, DI_D:CONV_CH] = bcn
    xs = _silu(cx)
    bcs = _silu(cbc)
    dt = _softplus(dt_ref[...] + dtb_ref[...])
    da = jnp.exp(dt * (-jnp.exp(alog_ref[...])))
    lane = lax.broadcasted_iota(jnp.int32, (1, 128), 1)
    lo = lane < 64
    bcol = [_col_bcast(bcs[:, 128 * g:128 * (g + 1)]) for g in range(G_D)]
    ccol = [_col_bcast(bcs[:, 256 + 128 * g:256 + 128 * (g + 1)]) for g in range(G_D)]
    ys = []
    ss = jnp.zeros((1, 1), F32)
    for p in range(H_D // 2):
        g = (2 * p) // (H_D // G_D)
        sl = slice(128 * p, 128 * (p + 1))
        le, lo_ = DT_LANE + 2 * p, DT_LANE + 2 * p + 1
        dt_pair = jnp.where(lo, _lane_col(dt, le), _lane_col(dt, lo_))
        da_pair = jnp.where(lo, _lane_col(da, le), _lane_col(da, lo_))
        xs_p = xs[:, sl]
        s_new = da_pair * s0_ref[p] + bcol[g] * (xs_p * dt_pair)
        st_ref[p] = s_new
        y = jnp.sum(ccol[g] * s_new, axis=0, keepdims=True)
        y = (y + xs_p * dskip_ref[:, sl]) * _silu(row[:, 3072 + 128 * p:3072 + 128 * (p + 1)])
        ys.append(y)
        ss = ss + jnp.sum(y * y, axis=-1, keepdims=True)
    inv = lax.rsqrt(ss * (1.0 / DI_D) + EPS)
    for p in range(H_D // 2):
        sl = slice(128 * p, 128 * (p + 1))
        od_ref[:, sl] = ys[p] * inv * nw_ref[:, sl]


def _ssd_sample(p_main, p_small, cs, s0, cwx, cwbc, cbx, cbbc, dtb, alog, dskip, nw):
    nb = p_main.shape[0]
    b3 = lambda b: (b, 0, 0)
    b4 = lambda b: (b, 0, 0, 0)
    c2 = lambda b: (0, 0)
    return pl.pallas_call(
        _ssd_sample_kernel,
        grid=(nb,),
        in_specs=[
            pl.BlockSpec((None, 1, ODD_MAIN), b3),
            pl.BlockSpec((None, 1, ODD_SMALL), b3),
            pl.BlockSpec((None, CONV_W - 1, CONV_CH), b3),
            pl.BlockSpec((None, 8, 128, 128), b4),
            pl.BlockSpec((CONV_W, DI_D), c2), pl.BlockSpec((CONV_W, 512), c2),
            pl.BlockSpec((1, DI_D), c2), pl.BlockSpec((1, 512), c2),
            pl.BlockSpec((1, ODD_SMALL), c2), pl.BlockSpec((1, ODD_SMALL), c2),
            pl.BlockSpec((1, DI_D), c2), pl.BlockSpec((1, DI_D), c2),
        ],
        out_specs=[
            pl.BlockSpec((None, 1, DI_D), b3),
            pl.BlockSpec((None, 8, 128, 128), b4),
            pl.BlockSpec((None, CONV_W - 1, CONV_CH), b3),
        ],
        out_shape=[jax.ShapeDtypeStruct((nb, 1, DI_D), F32), jax.ShapeDtypeStruct((nb, 8, 128, 128), F32),
                   jax.ShapeDtypeStruct((nb, CONV_W - 1, CONV_CH), F32)],
        compiler_params=_cparams(("arbitrary",)),
        name="ssd_sample",
    )(p_main.reshape(nb, 1, ODD_MAIN), p_small.reshape(nb, 1, ODD_SMALL), cs, s0,
      cwx, cwbc, cbx, cbbc, dtb, alog, dskip, nw)


def _pairs_to_heads(s):
    b, p, n, _ = s.shape
    return s.reshape(b, p, n, 2, 64).transpose(0, 1, 3, 2, 4).reshape(b, 2 * p, n, 64)


def _heads_to_pairs(s):
    b, h, n, d = s.shape
    return s.reshape(b, h // 2, 2, n, d).transpose(0, 1, 3, 2, 4).reshape(b, h // 2, n, 2 * d)


def _pad_lanes(v, start, width=ODD_SMALL):
    out = jnp.zeros((1, width), F32)
    return lax.dynamic_update_slice(out, v.reshape(1, -1).astype(F32), (0, start))


def kernel(x_prompt, x_sample, state_ret, cache_swa_k, cache_swa_v, cache_fox_k, cache_fox_v, cache_fox_logf,
           state_ssm, state_conv, page_table, norm_mix_pre, norm_mix_post, norm_mlp_pre, norm_mlp_post,
           w_in_even, w_out_even, ret_norm_w, swa_sinks, w_in_odd, w_out_odd, fox_fb, conv_w, conv_b,
           dt_bias, a_log, d_skip, ssd_norm_w, w_up, w_down):
    nb, seq = BATCH, SEQ
    nc = seq // CHUNK
    mp = nb * seq
    ms = DEC_BATCH
    xp = x_prompt.reshape(mp, D_MODEL)
    xs = x_sample.reshape(ms, D_MODEL)
    row = lambda v: v.reshape(1, -1)

    pos_p = jnp.arange(seq, dtype=jnp.int32)
    pos_s = jnp.full((1,), PAST_LEN, dtype=jnp.int32)
    ca_p, sa_p = _rope_tables(pos_p, RET_THETA, 8)
    cb_p, sb_p = _rope_tables(pos_p, ROPE_THETA_B, 2)
    ca_s, sa_s = _rope_tables(pos_s, RET_THETA, 8)
    cb_s, sb_s = _rope_tables(pos_s, ROPE_THETA_B, 2)

    we = w_in_even[0]
    we_main = we[:, :EVEN_MAIN].astype(BF16)
    we_small = we[:, EVEN_MAIN:].astype(BF16)
    wo = w_out_even[0].astype(BF16)
    wo_a, wo_b = wo[:1024], wo[1024:]
    g_pre, g_post = row(norm_mix_pre[0]), row(norm_mix_post[0])
    gm_pre, gm_post = row(norm_mlp_pre[0]), row(norm_mlp_post[0])
    gn_w = row(ret_norm_w[0])
    sinks = swa_sinks[0]

    pm, psm, sm, ssm_, wu_b, wd_b = _proj(xp, xs, g_pre, (we_main,), we_small, w_up, w_down, 0,
                                          odd=False, tm=256, tn=512, seq=seq)
    out_a, ret_p = _ret_prompt(pm, ca_p, sa_p, gn_w, nb=nb, nc=nc)
    out_b, swak_p, swav_p = _swa_prompt(pm, psm, cb_p, sb_p, sinks, nb=nb, nc=nc)
    mix_s, ret_s, swak_s, swav_s = _even_sample(
        sm, ssm_, state_ret[0].reshape(ms, 4, 128, 128), cache_swa_k[0].reshape(ms, WINDOW, 128),
        cache_swa_v[0].reshape(ms, WINDOW, 128), ca_s, sa_s, cb_s, sb_s, gn_w, sinks)
    mix_s = mix_s.reshape(ms, 2048).astype(BF16)
    xp, xs = _outproj(out_a, out_b, xp, mix_s[:, :1024], mix_s[:, 1024:], xs, wo_a, wo_b, g_post, tm=512)
    xp, xs = _mlp(xp, xs, gm_pre, gm_post, wu_b, wd_b, tm=512, tf=1024)

    wod = w_in_odd[0]
    wod_qkv = wod[:, :3072].astype(BF16)
    wod_zx = wod[:, 3080:5640].astype(BF16)
    wod_small = jnp.concatenate(
        [wod[:, 3072:3080], wod[:, 5640:5656], jnp.zeros((D_MODEL, ODD_SMALL - 24), F32)], axis=1).astype(BF16)
    wo1 = w_out_odd[0].astype(BF16)
    wo_c, wo_d = wo1[:1024], wo1[1024:]
    g_pre, g_post = row(norm_mix_pre[1]), row(norm_mix_post[1])
    gm_pre, gm_post = row(norm_mlp_pre[1]), row(norm_mlp_post[1])
    fb = _pad_lanes(fox_fb[0], 0)
    dtb = _pad_lanes(dt_bias[0], DT_LANE)
    alog = _pad_lanes(a_log[0], DT_LANE)
    cw = conv_w[0]
    cwx, cwbc = cw[:, :DI_D], cw[:, DI_D:]
    cbx, cbbc = row(conv_b[0][:DI_D]), row(conv_b[0][DI_D:])
    dskip = row(jnp.repeat(d_skip[0], HD_D))
    nw = row(ssd_norm_w[0])

    pm, psm, sm, ssm_, wu_b, wd_b, lf_p, fc_p, lf_s = _proj(
        xp, xs, g_pre, (wod_qkv, wod_zx), wod_small, w_up, w_down, 1, fb, odd=True, tm=256, tn=512, seq=seq)
    fc = fc_p[:, :H_C].reshape(nb, seq, H_C).transpose(0, 2, 1)
    out_d, ssm_pairs_p, conv_p = _ssd_prompt(pm, psm, cwx, cwbc, cbx, cbbc, dtb, alog, dskip, nw, nb=nb, nc=nc)
    fox_k_p = pm[:, 1024:2048].reshape(1, nb, seq, H_C, HD_C)
    fox_v_p = pm[:, 2048:3072].reshape(1, nb, seq, H_C, HD_C)
    fox_lf_p = lf_p[:, :H_C].reshape(1, nb, seq, H_C)

    q_s = sm[:, 0:1024].reshape(ms, H_C, HD_C)
    k_s = sm[:, 1024:2048].reshape(ms, H_C, HD_C)
    v_s = sm[:, 2048:3072].reshape(ms, H_C, HD_C)
    lf_s8 = lf_s[:, :H_C]
    out_c_s, out_c = _fox(page_table, q_s, k_s, v_s, lf_s8.reshape(ms, H_C, 1),
                          cache_fox_k[0], cache_fox_v[0], cache_fox_logf[0].transpose(0, 2, 1),
                          pm, fc[..., None], fc[:, :, None, :], pp=16, nb_p=nb, seq=seq)
    out_d_s, ssm_pairs_s, conv_s = _ssd_sample(sm, ssm_, state_conv[0], _heads_to_pairs(state_ssm[0]),
                                               cwx, cwbc, cbx, cbbc, dtb, alog, dskip, nw)
    xp, xs = _outproj(out_c, out_d, xp, out_c_s.reshape(ms, 1024).astype(BF16),
                      out_d_s.reshape(ms, DI_D).astype(BF16), xs, wo_c, wo_d, g_post, tm=512)
    xp, xs = _mlp(xp, xs, gm_pre, gm_post, wu_b, wd_b, tm=512, tf=1024)

    return (
        xp.reshape(nb, seq, D_MODEL), xs.reshape(ms, 1, D_MODEL),
        ret_p.reshape(1, nb, H_A, DK_A, DV_A), ret_s.reshape(1, ms, H_A, DK_A, DV_A),
        swak_p.reshape(1, nb, WINDOW, KV_B, HD_B), swav_p.reshape(1, nb, WINDOW, KV_B, HD_B),
        swak_s.reshape(1, ms, WINDOW, KV_B, HD_B), swav_s.reshape(1, ms, WINDOW, KV_B, HD_B),
        fox_k_p, fox_v_p, fox_lf_p,
        k_s.reshape(1, ms, 1, H_C, HD_C), v_s.reshape(1, ms, 1, H_C, HD_C), lf_s8.reshape(1, ms, 1, H_C),
        _pairs_to_heads(ssm_pairs_p)[None], _pairs_to_heads(ssm_pairs_s)[None],
        conv_p[None], conv_s[None],
    )
```

```python
import functools

import numpy as np
import jax
import jax.numpy as jnp
from jax import lax
from jax.experimental import pallas as pl
from jax.experimental.pallas import tpu as pltpu

F32 = jnp.float32
BF16 = jnp.bfloat16

D_MODEL = 2048
BATCH = 4
SEQ = 2048
DEC_BATCH = 32
PAST_LEN = 16384
PAGE_SIZE = 128
D_FF = 4 * D_MODEL
EPS = 1e-6
GN_EPS = 1e-5
CHUNK = 128

H_A, DK_A, DV_A = 8, 64, 128
RET_THETA = 10000.0
H_B, KV_B, HD_B = 16, 2, 64
WINDOW = 128
ROPE_THETA_B = 150000.0
H_C, HD_C = 8, 128
H_D, HD_D, G_D, N_D = 16, 64, 2, 128
CONV_W = 4
DI_D = H_D * HD_D
CONV_CH = DI_D + 2 * G_D * N_D

EVEN_MAIN = 4096
EVEN_SMALL = 256
ODD_MAIN = 5632
ODD_SMALL = 128
DT_LANE = 8

NEG = -1e30
VMEM_LIMIT = 56 * 1024 * 1024
MLP_VMEM_LIMIT = 60 * 1024 * 1024
MLP_SUB_ROWS = 256
MIX_NBAT = 2
SAMPLE_NSEQ = 4

_LOG_GAMMA = [float(v) for v in np.log1p(-np.exp2(-5.0 - np.arange(H_A, dtype=np.float64)))]


def _cparams(sem, vmem_limit=VMEM_LIMIT):
    return pltpu.CompilerParams(dimension_semantics=sem, vmem_limit_bytes=vmem_limit)


def _silu(x):
    return x * jax.nn.sigmoid(x)


def _softplus(x):
    return jnp.maximum(x, 0.0) + jnp.log1p(jnp.exp(-jnp.abs(x)))


def _log_sigmoid(x):
    return jnp.minimum(x, 0.0) - jnp.log1p(jnp.exp(-jnp.abs(x)))


def _rms(x, g):
    ms = jnp.mean(x * x, axis=-1, keepdims=True)
    return (x * lax.rsqrt(ms + EPS)) * g


def _dot(a, b):
    return jnp.dot(a, b, preferred_element_type=F32)


def _dot_nt(a, b):
    return lax.dot_general(a, b, (((1,), (1,)), ((), ())), preferred_element_type=F32)


def _split3(x):
    hi = x.astype(BF16)
    r = x - hi.astype(F32)
    mid = r.astype(BF16)
    lo = (r - mid.astype(F32)).astype(BF16)
    return hi, mid, lo


def _exact_left01(m01, x):
    hi, mid, lo = _split3(x)
    return _dot(m01, hi) + _dot(m01, mid) + _dot(m01, lo)


def _exact_right01(x, m01):
    hi, mid, lo = _split3(x)
    return _dot(hi, m01) + _dot(mid, m01) + _dot(lo, m01)


def _tri_lower(n):
    r = lax.broadcasted_iota(jnp.int32, (n, n), 0)
    c = lax.broadcasted_iota(jnp.int32, (n, n), 1)
    return jnp.where(r >= c, 1.0, 0.0).astype(BF16)


def _rope64(x, c, s):
    w = x.shape[-1]
    ax = x.ndim - 1
    lane = lax.broadcasted_iota(jnp.int32, x.shape, ax)
    first = (lane & 32) == 0
    left = pltpu.roll(x, w - 32, axis=ax)
    right = pltpu.roll(x, 32, axis=ax)
    return x * c + jnp.where(first, left, right) * s


def _rope_tables(pos, theta, reps):
    inv = 1.0 / (theta ** (jnp.arange(32, dtype=F32) * (2.0 / 64)))
    ang = pos.astype(F32)[:, None] * inv[None, :]
    cos, sin = jnp.cos(ang), jnp.sin(ang)
    c = jnp.concatenate([cos, cos], axis=-1)
    s = jnp.concatenate([-sin, sin], axis=-1)
    return jnp.tile(c, (1, reps)), jnp.tile(s, (1, reps))


def _proj_kernel(*refs, odd, n_w, tm, tn, seq):
    x_ref, xs_ref, g_ref = refs[0:3]
    w_refs = refs[3:3 + n_w]
    refs = refs[3 + n_w:]
    kv_refs = None
    if odd:
        (wsm_ref, wu_ref, wd_ref, fb_ref, o_ref, osm_ref, os_ref, ossm_ref, wub_ref, wdb_ref,
         lf_ref, fc_ref, lfs_ref, k3_ref, v3_ref, h_scr, carry_scr) = refs
        kv_refs = (k3_ref, v3_ref)
    else:
        (wsm_ref, wu_ref, wd_ref, o_ref, osm_ref, os_ref, ossm_ref, wub_ref, wdb_ref, h_scr) = refs
    i = pl.program_id(0)
    wub_ref[...] = wu_ref[...].astype(BF16)
    wdb_ref[...] = wd_ref[...].astype(BF16)
    width = H_C * HD_C

    def rows(x_r, o_r, osm_r, n, kv=None):
        h_scr[0:n, :] = _rms(x_r[...], g_ref[...]).astype(BF16)
        sm = _dot(h_scr[0:n, :], wsm_ref[...])
        osm_r[...] = sm
        base = 0
        for w_ref in w_refs:
            for t in range(w_ref.shape[1] // tn):
                c0 = base + t * tn
                val = _dot(h_scr[0:n, :], w_ref[:, t * tn:(t + 1) * tn])
                o_r[:, c0:c0 + tn] = val
                if kv is not None and width <= c0 < 3 * width:
                    dst = kv[(c0 - width) // width]
                    h0 = ((c0 - width) % width) // HD_C
                    for hh in range(tn // HD_C):
                        dst[:, h0 + hh, :] = val[:, HD_C * hh:HD_C * (hh + 1)]
            base += w_ref.shape[1]
        return sm

    sm = rows(x_ref, o_ref, osm_ref, tm, kv_refs)
    if odd:
        lf = _log_sigmoid(sm + fb_ref[...])
        lf_ref[...] = lf

        @pl.when((i * tm) % seq == 0)
        def _():
            carry_scr[...] = jnp.zeros_like(carry_scr)

        f = _exact_left01(_tri_lower(tm), lf) + carry_scr[...]
        fc_ref[...] = f
        carry_scr[...] = f[tm - 1:tm, :]

    @pl.when(i == 0)
    def _():
        sms = rows(xs_ref, os_ref, ossm_ref, xs_ref.shape[0])
        if odd:
            lfs_ref[...] = _log_sigmoid(sms + fb_ref[...])


def _proj(x, xs, g, w_mains, w_small, w_up, w_down, layer, fb=None, *, odd, tm, tn, seq):
    m, ms = x.shape[0], xs.shape[0]
    n_main = sum(w.shape[1] for w in w_mains)
    n_small = w_small.shape[1]
    ni = m // tm
    ru, rd = D_MODEL // ni, D_FF // ni
    c2 = lambda i: (0, 0)
    r2 = lambda i: (i, 0)
    in_specs = [
        pl.BlockSpec((tm, D_MODEL), r2),
        pl.BlockSpec((ms, D_MODEL), c2),
        pl.BlockSpec((1, D_MODEL), c2),
    ] + [pl.BlockSpec((D_MODEL, w.shape[1]), c2, pipeline_mode=pl.Buffered(1)) for w in w_mains] + [
        pl.BlockSpec((D_MODEL, n_small), c2),
        pl.BlockSpec((None, ru, D_FF), lambda i: (layer, i, 0)),
        pl.BlockSpec((None, rd, D_MODEL), lambda i: (layer, i, 0)),
    ]
    args = [x, xs, g, *w_mains, w_small, w_up, w_down]
    small = jax.ShapeDtypeStruct((m, n_small), F32)
    small_s = jax.ShapeDtypeStruct((ms, n_small), F32)
    small_spec = pl.BlockSpec((tm, n_small), r2)
    small_s_spec = pl.BlockSpec((ms, n_small), c2)
    out_shape = [jax.ShapeDtypeStruct((m, n_main), F32), small,
                 jax.ShapeDtypeStruct((ms, n_main), F32), small_s,
                 jax.ShapeDtypeStruct((D_MODEL, D_FF), BF16), jax.ShapeDtypeStruct((D_FF, D_MODEL), BF16)]
    out_specs = [pl.BlockSpec((tm, n_main), r2), small_spec,
                 pl.BlockSpec((ms, n_main), c2), small_s_spec,
                 pl.BlockSpec((ru, D_FF), r2), pl.BlockSpec((rd, D_MODEL), r2)]
    scratch = [pltpu.VMEM((tm, D_MODEL), BF16)]
    if odd:
        in_specs.append(pl.BlockSpec((1, n_small), c2))
        args.append(fb)
        rows3 = jax.ShapeDtypeStruct((m, H_C, HD_C), F32)
        rows3_spec = pl.BlockSpec((tm, H_C, HD_C), lambda i: (i, 0, 0))
        out_shape += [small, small, small_s, rows3, rows3]
        out_specs += [small_spec, small_spec, small_s_spec, rows3_spec, rows3_spec]
        scratch.append(pltpu.VMEM((1, n_small), F32))
    return pl.pallas_call(
        functools.partial(_proj_kernel, odd=odd, n_w=len(w_mains), tm=tm, tn=tn, seq=seq),
        grid=(ni,), in_specs=in_specs, out_specs=out_specs, out_shape=out_shape,
        scratch_shapes=scratch, compiler_params=_cparams(("arbitrary",), MLP_VMEM_LIMIT),
        name="proj_odd" if odd else "proj_even",
    )(*args)


def _outproj_kernel(a_ref, b_ref, x_ref, as_ref, bs_ref, xs_ref, wa_ref, wb_ref, g_ref, o_ref, os_ref):
    g = g_ref[...]
    n = x_ref.shape[0]
    sub = min(n, MLP_SUB_ROWS)
    for r in range(n // sub):
        rows = slice(r * sub, (r + 1) * sub)
        y = _dot(a_ref[rows, :], wa_ref[...]) + _dot(b_ref[rows, :], wb_ref[...])
        o_ref[rows, :] = x_ref[rows, :] + _rms(y, g)

    @pl.when(pl.program_id(0) == 0)
    def _():
        ys = _dot(as_ref[...], wa_ref[...]) + _dot(bs_ref[...], wb_ref[...])
        os_ref[...] = xs_ref[...] + _rms(ys, g)


def _outproj(a, b, x, a_s, b_s, xs, wa, wb, g, *, tm):
    m, ms = x.shape[0], xs.shape[0]
    ka, kb = a.shape[1], b.shape[1]
    c2 = lambda i: (0, 0)
    return pl.pallas_call(
        _outproj_kernel,
        grid=(m // tm,),
        in_specs=[
            pl.BlockSpec((tm, ka), lambda i: (i, 0)),
            pl.BlockSpec((tm, kb), lambda i: (i, 0)),
            pl.BlockSpec((tm, D_MODEL), lambda i: (i, 0)),
            pl.BlockSpec((ms, ka), c2),
            pl.BlockSpec((ms, kb), c2),
            pl.BlockSpec((ms, D_MODEL), c2),
            pl.BlockSpec((ka, D_MODEL), c2),
            pl.BlockSpec((kb, D_MODEL), c2),
            pl.BlockSpec((1, D_MODEL), c2),
        ],
        out_specs=[pl.BlockSpec((tm, D_MODEL), lambda i: (i, 0)), pl.BlockSpec((ms, D_MODEL), c2)],
        out_shape=[jax.ShapeDtypeStruct((m, D_MODEL), F32), jax.ShapeDtypeStruct((ms, D_MODEL), F32)],
        compiler_params=_cparams(("arbitrary",)),
        name="outproj",
    )(a, b, x, a_s, b_s, xs, wa, wb, g)


def _mlp_kernel(x_ref, xs_ref, gpre_ref, gpost_ref, wu_ref, wd_ref, o_ref, os_ref, h_scr, hs_scr):
    i = pl.program_id(0)
    j = pl.program_id(1)
    last = pl.num_programs(1) - 1

    def group(x_r, o_r, h_s):
        @pl.when(j == 0)
        def _():
            h_s[...] = _rms(x_r[...], gpre_ref[...]).astype(BF16)
            o_r[...] = jnp.zeros_like(o_r)

        n = x_r.shape[0]
        sub = min(n, MLP_SUB_ROWS)
        for r in range(n // sub):
            rows = slice(r * sub, (r + 1) * sub)
            u = jnp.maximum(_dot(h_s[rows, :], wu_ref[...]), 0.0)
            o_r[rows, :] += _dot((u * u).astype(BF16), wd_ref[...])

        @pl.when(j == last)
        def _():
            o_r[...] = x_r[...] + _rms(o_r[...], gpost_ref[...])

    group(x_ref, o_ref, h_scr)

    @pl.when(i == 0)
    def _():
        group(xs_ref, os_ref, hs_scr)


def _mlp(x, xs, gpre, gpost, w_up, w_down, *, tm, tf):
    m, ms = x.shape[0], xs.shape[0]
    c2 = lambda i, j: (0, 0)
    return pl.pallas_call(
        _mlp_kernel,
        grid=(m // tm, D_FF // tf),
        in_specs=[
            pl.BlockSpec((tm, D_MODEL), lambda i, j: (i, 0)),
            pl.BlockSpec((ms, D_MODEL), c2),
            pl.BlockSpec((1, D_MODEL), c2),
            pl.BlockSpec((1, D_MODEL), c2),
            pl.BlockSpec((D_MODEL, tf), lambda i, j: (0, j)),
            pl.BlockSpec((tf, D_MODEL), lambda i, j: (j, 0)),
        ],
        out_specs=[pl.BlockSpec((tm, D_MODEL), lambda i, j: (i, 0)), pl.BlockSpec((ms, D_MODEL), c2)],
        out_shape=[jax.ShapeDtypeStruct((m, D_MODEL), F32), jax.ShapeDtypeStruct((ms, D_MODEL), F32)],
        scratch_shapes=[pltpu.VMEM((tm, D_MODEL), BF16), pltpu.VMEM((ms, D_MODEL), BF16)],
        compiler_params=_cparams(("arbitrary", "arbitrary"), MLP_VMEM_LIMIT),
        name="mlp",
    )(x, xs, gpre, gpost, w_up, w_down)


def _group_norm_gate(y, gate, gw):
    mu = jnp.mean(y, axis=-1, keepdims=True)
    d = y - mu
    var = jnp.mean(d * d, axis=-1, keepdims=True)
    return _silu(gate) * (d * lax.rsqrt(var + GN_EPS) * gw)


def _ret_prompt_kernel(q_all, k_all, v_all, g_all, cos_ref, sin_ref, gn_ref, oa_all, st_all,
                       s_all, d_scr, e_scr, t_scr):
    b = pl.program_id(0)
    c = pl.program_id(1)
    t_i = lax.broadcasted_iota(jnp.int32, (CHUNK, CHUNK), 0)
    s_i = lax.broadcasted_iota(jnp.int32, (CHUNK, CHUNK), 1)
    lo = s_i < 64

    @pl.when((b == 0) & (c == 0))
    def _():
        tf = t_i.astype(F32)
        sf = s_i.astype(F32)
        for h in range(H_A):
            d_scr[h] = jnp.where(t_i >= s_i, jnp.exp((tf - sf) * _LOG_GAMMA[h]), 0.0)
            e_scr[h] = jnp.exp((tf + 1.0) * _LOG_GAMMA[h])
        for p in range(H_A // 2):
            lg = jnp.where(lo, _LOG_GAMMA[2 * p], _LOG_GAMMA[2 * p + 1])
            t_scr[p] = jnp.exp((CHUNK - 1.0 - tf) * lg)

    @pl.when(c == 0)
    def _():
        s_all[...] = jnp.zeros_like(s_all)

    cos, sin = cos_ref[...], sin_ref[...]
    top = t_i < 64
    for u in range(MIX_NBAT):
        v_ref, g_ref, oa_ref, s_scr = v_all.at[u], g_all.at[u], oa_all.at[u], s_all.at[u]
        qr = _rope64(q_all[u], cos, sin)
        kr = _rope64(k_all[u], cos, sin) * (DK_A ** -0.5)
        for p in range(H_A // 2):
            sl = slice(128 * p, 128 * (p + 1))
            qp, kp = qr[:, sl], kr[:, sl]
            kb = kp.astype(BF16)
            s_old = s_scr[p]
            s_old_b = s_old.astype(BF16)
            ktt = jnp.transpose(kp * t_scr[p]).astype(BF16)
            upd = []
            for e in range(2):
                h = 2 * p + e
                hs = slice(128 * h, 128 * (h + 1))
                qm = jnp.where(lo if e == 0 else ~lo, qp, 0.0).astype(BF16)
                vh = v_ref[:, hs].astype(BF16)
                att = (_dot_nt(qm, kb) * d_scr[h]).astype(BF16)
                y = _dot(att, vh) + _dot(qm, s_old_b) * e_scr[h]
                oa_ref[:, hs] = _group_norm_gate(y, g_ref[:, hs], gn_ref[:, hs]).astype(BF16)
                upd.append(_dot(ktt, vh))
            g128 = jnp.where(top, float(np.exp(CHUNK * _LOG_GAMMA[2 * p])),
                             float(np.exp(CHUNK * _LOG_GAMMA[2 * p + 1])))
            s_scr[p] = g128 * s_old + jnp.where(top, upd[0], upd[1])

    @pl.when(c == pl.num_programs(1) - 1)
    def _():
        st_all[...] = s_all[...]


def _batch_groups(a, nb, seq):
    return a.reshape(nb // MIX_NBAT, MIX_NBAT, seq, a.shape[-1])


def _ret_prompt(p_main, cos, sin, gn_w, *, nb, nc):
    seq = nc * CHUNK
    pg = _batch_groups(p_main, nb, seq)
    blk = lambda w, col: pl.BlockSpec((None, MIX_NBAT, CHUNK, w), lambda b, c: (b, 0, c, col))
    oa, st = pl.pallas_call(
        _ret_prompt_kernel,
        grid=(nb // MIX_NBAT, nc),
        in_specs=[
            blk(512, 0), blk(512, 1), blk(1024, 1), blk(1024, 2),
            pl.BlockSpec((CHUNK, 512), lambda b, c: (c, 0)),
            pl.BlockSpec((CHUNK, 512), lambda b, c: (c, 0)),
            pl.BlockSpec((1, 1024), lambda b, c: (0, 0)),
        ],
        out_specs=[
            blk(1024, 0),
            pl.BlockSpec((None, MIX_NBAT, 4, 128, 128), lambda b, c: (b, 0, 0, 0, 0)),
        ],
        out_shape=[jax.ShapeDtypeStruct((nb // MIX_NBAT, MIX_NBAT, seq, 1024), BF16),
                   jax.ShapeDtypeStruct((nb // MIX_NBAT, MIX_NBAT, 4, 128, 128), F32)],
        scratch_shapes=[pltpu.VMEM((MIX_NBAT, 4, 128, 128), F32), pltpu.VMEM((H_A, 128, 128), F32),
                        pltpu.VMEM((H_A, 128, 128), F32), pltpu.VMEM((4, 128, 128), F32)],
        compiler_params=_cparams(("arbitrary", "arbitrary")),
        name="ret_prompt",
    )(pg, pg, pg, pg, cos, sin, gn_w)
    return oa.reshape(nb * seq, 1024), st.reshape(nb, 4, 128, 128)


def _swa_padded(x2):
    lane = lax.broadcasted_iota(jnp.int32, x2.shape, 1)
    lo = lane < 64
    xr = pltpu.roll(x2, 64, axis=1)
    z = jnp.zeros_like(x2)
    return {
        (0, 0): jnp.where(lo, x2, z).astype(BF16), (0, 1): jnp.where(lo, z, xr).astype(BF16),
        (1, 0): jnp.where(lo, xr, z).astype(BF16), (1, 1): jnp.where(lo, z, x2).astype(BF16),
    }


def _swa_attend(q_pair_fn, kpad, vpad, valid, sink_ref, store_fn, rows):
    per_group = (H_B // KV_B) // 2
    for g in range(KV_B):
        pairs = list(range(g * per_group, (g + 1) * per_group))
        qs = jnp.concatenate([q_pair_fn(jj) for jj in pairs], axis=0).astype(BF16)
        acc = None
        for e in range(2):
            s = _dot_nt(qs, kpad[(g, e)])
            if valid is not None:
                s = jnp.where(valid, s, NEG)
            sink = jnp.concatenate([jnp.full((rows, 1), sink_ref[2 * jj + e], F32) for jj in pairs], axis=0)
            mx = jnp.maximum(jnp.max(s, axis=-1, keepdims=True), sink)
            pr = jnp.exp(s - mx)
            den = jnp.sum(pr, axis=-1, keepdims=True) + jnp.exp(sink - mx)
            o = _dot(pr.astype(BF16), vpad[(g, e)]) / den
            acc = o if acc is None else acc + o
        for t, jj in enumerate(pairs):
            store_fn(jj, acc[t * rows:(t + 1) * rows])


def _swa_prompt_kernel(sink_ref, q_all, kvc_all, kvp_all, cc_ref, sc_ref, cp_ref, sp_ref,
                       ob_all, ko_all, vo_all):
    n = pl.program_id(1)
    cc, sc = cc_ref[...], sc_ref[...]
    stacked = WINDOW * (H_B // KV_B) // 2
    i = lax.broadcasted_iota(jnp.int32, (stacked, 2 * WINDOW), 0) & (WINDOW - 1)
    j = lax.broadcasted_iota(jnp.int32, (stacked, 2 * WINDOW), 1)
    valid = (j >= i + 1) & (j <= i + WINDOW) & ((n > 0) | (j >= WINDOW))
    for u in range(MIX_NBAT):
        q_ref, ob_ref = q_all.at[u], ob_all.at[u]
        kvc, kvp = kvc_all[u], kvp_all[u]
        kc = _rope64(kvc[:, :128], cc, sc)
        kp = _rope64(kvp[:, :128], cp_ref[...], sp_ref[...])
        vc = kvc[:, 128:]
        kpad = _swa_padded(jnp.concatenate([kp, kc], axis=0))
        vpad = _swa_padded(jnp.concatenate([kvp[:, 128:], vc], axis=0))

        def q_pair(jj, q_ref=q_ref):
            return _rope64(q_ref[:, 128 * jj:128 * (jj + 1)], cc, sc) * (HD_B ** -0.5)

        def store(jj, acc, ob_ref=ob_ref):
            ob_ref[:, 128 * jj:128 * (jj + 1)] = acc.astype(BF16)

        _swa_attend(q_pair, kpad, vpad, valid, sink_ref, store, WINDOW)

        @pl.when(n == pl.num_programs(1) - 1)
        def _(u=u, kc=kc, vc=vc):
            ko_all[u] = kc
            vo_all[u] = vc


def _swa_prompt(p_main, p_small, cos, sin, sinks, *, nb, nc):
    seq = nc * CHUNK
    ng = nb // MIX_NBAT
    pg, psg = _batch_groups(p_main, nb, seq), _batch_groups(p_small, nb, seq)
    prev = lambda n: jnp.maximum(n - 1, 0)
    win = pl.BlockSpec((None, MIX_NBAT, WINDOW, 128), lambda b, n: (b, 0, 0, 0))
    ob, ko, vo = pl.pallas_call(
        _swa_prompt_kernel,
        grid=(ng, nc),
        in_specs=[
            pl.BlockSpec(memory_space=pltpu.SMEM),
            pl.BlockSpec((None, MIX_NBAT, CHUNK, 1024), lambda b, n: (b, 0, n, 3)),
            pl.BlockSpec((None, MIX_NBAT, CHUNK, 256), lambda b, n: (b, 0, n, 0)),
            pl.BlockSpec((None, MIX_NBAT, CHUNK, 256), lambda b, n: (b, 0, prev(n), 0)),
            pl.BlockSpec((CHUNK, 128), lambda b, n: (n, 0)),
            pl.BlockSpec((CHUNK, 128), lambda b, n: (n, 0)),
            pl.BlockSpec((CHUNK, 128), lambda b, n: (prev(n), 0)),
            pl.BlockSpec((CHUNK, 128), lambda b, n: (prev(n), 0)),
        ],
        out_specs=[pl.BlockSpec((None, MIX_NBAT, CHUNK, 1024), lambda b, n: (b, 0, n, 0)), win, win],
        out_shape=[jax.ShapeDtypeStruct((ng, MIX_NBAT, seq, 1024), BF16),
                   jax.ShapeDtypeStruct((ng, MIX_NBAT, WINDOW, 128), F32),
                   jax.ShapeDtypeStruct((ng, MIX_NBAT, WINDOW, 128), F32)],
        compiler_params=_cparams(("arbitrary", "arbitrary")),
        name="swa_prompt",
    )(sinks, pg, psg, psg, cos, sin, cos, sin)
    return ob.reshape(nb * seq, 1024), ko.reshape(nb, WINDOW, 128), vo.reshape(nb, WINDOW, 128)


def _col_bcast(row128):
    return jnp.transpose(jnp.broadcast_to(row128, (128, 128)))


def _even_sample_kernel(sink_ref, p_all, ps_all, s0_all, kbuf_all, vbuf_all, ca_ref, sa_ref, cb_ref, sb_ref,
                        gn_ref, mix_all, st_all, ko_all, vo_all):
    for u in range(SAMPLE_NSEQ):
        _even_sample_one(sink_ref, p_all.at[u], ps_all.at[u], s0_all.at[u], kbuf_all.at[u], vbuf_all.at[u],
                         ca_ref, sa_ref, cb_ref, sb_ref, gn_ref,
                         mix_all.at[u], st_all.at[u], ko_all.at[u], vo_all.at[u])


def _even_sample_one(sink_ref, p_ref, ps_ref, s0_ref, kbuf_ref, vbuf_ref, ca_ref, sa_ref, cb_ref, sb_ref,
                     gn_ref, mix_ref, st_ref, ko_ref, vo_ref):
    row = p_ref[...]
    ca, sa = ca_ref[...], sa_ref[...]
    qr = _rope64(jnp.broadcast_to(row[:, 0:512], (8, 512)), ca, sa)
    kr = _rope64(jnp.broadcast_to(row[:, 512:1024], (8, 512)), ca, sa) * (DK_A ** -0.5)
    r_i = lax.broadcasted_iota(jnp.int32, (128, 128), 0)
    top = r_i < 64
    for p in range(H_A // 2):
        sl = slice(128 * p, 128 * (p + 1))
        kcol = _col_bcast(kr[0:1, sl])
        qcol = _col_bcast(qr[0:1, sl])
        he, ho = 2 * p, 2 * p + 1
        v_e = row[:, 1024 + 128 * he:1024 + 128 * (he + 1)]
        v_o = row[:, 1024 + 128 * ho:1024 + 128 * (ho + 1)]
        gam = jnp.where(top, float(np.exp(_LOG_GAMMA[he])), float(np.exp(_LOG_GAMMA[ho])))
        s_new = gam * s0_ref[p] + kcol * jnp.where(top, v_e, v_o)
        st_ref[p] = s_new
        prod = qcol * s_new
        for e, h in ((0, he), (1, ho)):
            y = jnp.sum(jnp.where(top if e == 0 else ~top, prod, 0.0), axis=0, keepdims=True)
            hs = slice(128 * h, 128 * (h + 1))
            gate = row[:, 2048 + 128 * h:2048 + 128 * (h + 1)]
            mix_ref[:, hs] = _group_norm_gate(y, gate, gn_ref[:, hs])

    cb, sb = cb_ref[...], sb_ref[...]
    ps = ps_ref[...]
    kn = _rope64(jnp.broadcast_to(ps[:, 0:128], (8, 128)), cb, sb)[0:1]
    vn = ps[:, 128:256]
    last = r_i == WINDOW - 1
    k_new = jnp.where(last, kn, pltpu.roll(kbuf_ref[...], WINDOW - 1, axis=0))
    v_new = jnp.where(last, vn, pltpu.roll(vbuf_ref[...], WINDOW - 1, axis=0))
    ko_ref[...] = k_new
    vo_ref[...] = v_new
    kpad = _swa_padded(k_new)
    vpad = _swa_padded(v_new)

    def q_pair(jj):
        q8 = jnp.broadcast_to(row[:, 3072 + 128 * jj:3072 + 128 * (jj + 1)], (8, 128))
        return _rope64(q8, cb, sb) * (HD_B ** -0.5)

    def store(jj, acc):
        mix_ref[:, 1024 + 128 * jj:1024 + 128 * (jj + 1)] = acc[0:1]

    _swa_attend(q_pair, kpad, vpad, None, sink_ref, store, 8)


def _even_sample(p_main, p_small, s0, kbuf, vbuf, ca, sa, cb, sb, gn_w, sinks):
    nb = p_main.shape[0]
    b3 = lambda b: (b, 0, 0)
    b4 = lambda b: (b, 0, 0, 0)
    c2 = lambda b: (0, 0)
    ns = SAMPLE_NSEQ
    return pl.pallas_call(
        _even_sample_kernel,
        grid=(nb // ns,),
        in_specs=[
            pl.BlockSpec(memory_space=pltpu.SMEM),
            pl.BlockSpec((ns, 1, EVEN_MAIN), b3),
            pl.BlockSpec((ns, 1, EVEN_SMALL), b3),
            pl.BlockSpec((ns, 4, 128, 128), b4),
            pl.BlockSpec((ns, WINDOW, 128), b3),
            pl.BlockSpec((ns, WINDOW, 128), b3),
            pl.BlockSpec((1, 512), c2), pl.BlockSpec((1, 512), c2),
            pl.BlockSpec((1, 128), c2), pl.BlockSpec((1, 128), c2),
            pl.BlockSpec((1, 1024), c2),
        ],
        out_specs=[
            pl.BlockSpec((ns, 1, 2048), b3),
            pl.BlockSpec((ns, 4, 128, 128), b4),
            pl.BlockSpec((ns, WINDOW, 128), b3),
            pl.BlockSpec((ns, WINDOW, 128), b3),
        ],
        out_shape=[jax.ShapeDtypeStruct((nb, 1, 2048), F32), jax.ShapeDtypeStruct((nb, 4, 128, 128), F32),
                   jax.ShapeDtypeStruct((nb, WINDOW, 128), F32), jax.ShapeDtypeStruct((nb, WINDOW, 128), F32)],
        compiler_params=_cparams(("arbitrary",)),
        name="even_sample",
    )(sinks, p_main.reshape(nb, 1, EVEN_MAIN), p_small.reshape(nb, 1, EVEN_SMALL), s0, kbuf, vbuf,
      ca, sa, cb, sb, gn_w)


def _fox_q_tile(qi, q_ref, fcol_ref, frow_ref, kb_scr, vb_scr, o_ref, tq):
    r_i = lax.broadcasted_iota(jnp.int32, (tq, tq), 0)
    c_i = lax.broadcasted_iota(jnp.int32, (tq, tq), 1)
    q = (q_ref[...] * (HD_C ** -0.5)).astype(BF16)
    fq = fcol_ref[...]
    diag = slice(qi * tq, (qi + 1) * tq)
    s_d = jnp.where(c_i <= r_i, _dot_nt(q, kb_scr[diag, :]) + (fq - frow_ref[:, diag]), NEG)
    m = jnp.max(s_d, axis=-1, keepdims=True)
    if qi > 0:
        past = slice(0, qi * tq)
        s_p = _dot_nt(q, kb_scr[past, :]) + (fq - frow_ref[:, past])
        m = jnp.maximum(m, jnp.max(s_p, axis=-1, keepdims=True))
    p_d = jnp.exp(s_d - m)
    l = jnp.sum(p_d, axis=-1, keepdims=True)
    acc = _dot(p_d.astype(BF16), vb_scr[diag, :])
    if qi > 0:
        p_p = jnp.exp(s_p - m)
        l = l + jnp.sum(p_p, axis=-1, keepdims=True)
        acc = acc + _dot(p_p.astype(BF16), vb_scr[past, :])
    o_ref[...] = (acc / l).astype(BF16)


def _fox_kernel(*refs, pp, tq):
    pt_ref = refs[0]
    q_ref, kn_ref, vn_ref, lfn_ref = refs[1:5]
    k_refs = refs[5:5 + pp]
    v_refs = refs[5 + pp:5 + 2 * pp]
    lf_refs = refs[5 + 2 * pp:5 + 3 * pp]
    pq_ref, pk_ref, pv_ref, fcol_ref, frow_ref = refs[5 + 3 * pp:10 + 3 * pp]
    o_ref, po_ref = refs[10 + 3 * pp:12 + 3 * pp]
    m_scr, l_scr, acc_scr, carry_scr, rexp_scr, tri_scr, kb_scr, vb_scr = refs[12 + 3 * pp:]
    del pt_ref
    b = pl.program_id(0)
    s_id = pl.program_id(1)
    rows = PAGE_SIZE * H_C

    @pl.when(s_id == 0)
    def _():
        kb_scr[...] = pk_ref[...].astype(BF16)
        vb_scr[...] = pv_ref[...].astype(BF16)

    for qi in range(kb_scr.shape[0] // tq):
        @pl.when(s_id == qi)
        def _(qi=qi):
            _fox_q_tile(qi, pq_ref, fcol_ref, frow_ref, kb_scr, vb_scr, po_ref, tq)

    @pl.when((b == 0) & (s_id == 0))
    def _():
        r = lax.broadcasted_iota(jnp.int32, (PAGE_SIZE, rows), 0)
        c = lax.broadcasted_iota(jnp.int32, (PAGE_SIZE, rows), 1)
        rexp_scr[...] = jnp.where((c >> 3) == r, 1.0, 0.0).astype(BF16)
        r2 = lax.broadcasted_iota(jnp.int32, (PAGE_SIZE, PAGE_SIZE), 0)
        c2 = lax.broadcasted_iota(jnp.int32, (PAGE_SIZE, PAGE_SIZE), 1)
        tri_scr[...] = jnp.where(r2 <= c2, 1.0, 0.0).astype(BF16)

    @pl.when(s_id == 0)
    def _():
        m_scr[...] = jnp.full_like(m_scr, NEG)
        l_scr[...] = jnp.zeros_like(l_scr)
        acc_scr[...] = jnp.zeros_like(acc_scr)
        carry_scr[...] = jnp.zeros_like(carry_scr)

    qs = q_ref[...] * (HD_C ** -0.5)
    qb = qs.astype(BF16)
    h_i = lax.broadcasted_iota(jnp.int32, (H_C, rows), 0)
    c_i = lax.broadcasted_iota(jnp.int32, (H_C, rows), 1)
    diag = (c_i & (H_C - 1)) == h_i
    m, l, acc, carry = m_scr[...], l_scr[...], acc_scr[...], carry_scr[...]

    lf_all = jnp.concatenate([lf_refs[r][...] for r in range(pp)], axis=0)
    cum_in = _exact_right01(lf_all, tri_scr[...])
    offs = []
    for r in range(pp):
        offs.append(carry)
        carry = carry + cum_in[H_C * r:H_C * (r + 1), PAGE_SIZE - 1:PAGE_SIZE]
    gexp = _exact_right01(cum_in + jnp.concatenate(offs, axis=0), rexp_scr[...])

    logits = []
    for r in range(pp):
        k2 = k_refs[r][...].reshape(rows, HD_C).astype(BF16)
        logits.append(jnp.where(diag, _dot_nt(qb, k2) - gexp[H_C * r:H_C * (r + 1)], NEG))
    mx = logits[0]
    for r in range(1, pp):
        mx = jnp.maximum(mx, logits[r])
    m_new = jnp.maximum(m, jnp.max(mx, axis=-1, keepdims=True))
    a = jnp.exp(m - m_new)
    psum = pv = None
    for r in range(pp):
        p = jnp.exp(logits[r] - m_new)
        o = _dot(p.astype(BF16), v_refs[r][...].reshape(rows, HD_C).astype(BF16))
        psum = p if psum is None else psum + p
        pv = o if pv is None else pv + o
    l = a * l + jnp.sum(psum, axis=-1, keepdims=True)
    acc = a * acc + pv
    m = m_new
    m_scr[...] = m
    l_scr[...] = l
    acc_scr[...] = acc
    carry_scr[...] = carry

    @pl.when(s_id == pl.num_programs(1) - 1)
    def _():
        fq = carry + lfn_ref[...]
        s_new = jnp.sum(qs * kn_ref[...], axis=-1, keepdims=True)
        m_past = m + fq
        mx = jnp.maximum(m_past, s_new)
        wp = jnp.exp(m_past - mx)
        wn = jnp.exp(s_new - mx)
        o_ref[...] = (acc * wp + wn * vn_ref[...]) / (l * wp + wn)


def _fox(page_table, q, k_new, v_new, lf_new, cache_k, cache_v, cache_lf, p_main, fcol, frow, *, pp, nb_p, seq):
    nb, n_pages = page_table.shape
    ns = n_pages // pp
    tq = seq // ns
    assert nb == nb_p * H_C and ns * tq == seq and ns * pp == n_pages
    b3 = lambda b, s, pt: (b, 0, 0)
    kv_spec = lambda r: pl.BlockSpec((None, PAGE_SIZE, H_C, HD_C), lambda b, s, pt: (pt[b, s * pp + r], 0, 0, 0))
    lf_spec = lambda r: pl.BlockSpec((None, H_C, PAGE_SIZE), lambda b, s, pt: (pt[b, s * pp + r], 0, 0))
    in_specs = [pl.BlockSpec((None, H_C, HD_C), b3), pl.BlockSpec((None, H_C, HD_C), b3),
                pl.BlockSpec((None, H_C, HD_C), b3), pl.BlockSpec((None, H_C, 1), b3)]
    in_specs += [kv_spec(r) for r in range(pp)] + [kv_spec(r) for r in range(pp)] + [lf_spec(r) for r in range(pp)]
    in_specs += [
        pl.BlockSpec((tq, HD_C), lambda b, s, pt: ((b // H_C) * ns + s, b % H_C)),
        pl.BlockSpec((seq, HD_C), lambda b, s, pt: (b // H_C, H_C + b % H_C)),
        pl.BlockSpec((seq, HD_C), lambda b, s, pt: (b // H_C, 2 * H_C + b % H_C)),
        pl.BlockSpec((None, None, tq, 1), lambda b, s, pt: (b // H_C, b % H_C, s, 0)),
        pl.BlockSpec((None, None, 1, seq), lambda b, s, pt: (b // H_C, b % H_C, 0, 0)),
    ]
    grid_spec = pltpu.PrefetchScalarGridSpec(
        num_scalar_prefetch=1, grid=(nb, ns), in_specs=in_specs,
        out_specs=[pl.BlockSpec((None, H_C, HD_C), b3),
                   pl.BlockSpec((tq, HD_C), lambda b, s, pt: ((b // H_C) * ns + s, b % H_C))],
        scratch_shapes=[pltpu.VMEM((H_C, 1), F32), pltpu.VMEM((H_C, 1), F32), pltpu.VMEM((H_C, HD_C), F32),
                        pltpu.VMEM((H_C, 1), F32), pltpu.VMEM((PAGE_SIZE, PAGE_SIZE * H_C), BF16),
                        pltpu.VMEM((PAGE_SIZE, PAGE_SIZE), BF16),
                        pltpu.VMEM((seq, HD_C), BF16), pltpu.VMEM((seq, HD_C), BF16)],
    )
    return pl.pallas_call(
        functools.partial(_fox_kernel, pp=pp, tq=tq),
        grid_spec=grid_spec,
        out_shape=[jax.ShapeDtypeStruct((nb, H_C, HD_C), F32),
                   jax.ShapeDtypeStruct((nb_p * seq, H_C * HD_C), BF16)],
        compiler_params=_cparams(("arbitrary", "arbitrary")),
        name="fox",
    )(page_table, q, k_new, v_new, lf_new, *([cache_k] * pp), *([cache_v] * pp), *([cache_lf] * pp),
      p_main, p_main, p_main, fcol, frow)


def _lane_col(x, lane):
    return x[:, lane:lane + 1]


def _ssd_prompt_kernel(x_all, bc_all, z_all, dt_all, cwx_ref, cwbc_ref, cbx_ref, cbbc_ref, dtb_ref, alog_ref,
                       dskip_ref, nw_ref, od_all, st_all, cv_all, xpx_all, xpbc_all, s_all, y_all):
    c = pl.program_id(1)
    nc = pl.num_programs(1)

    @pl.when(c == 0)
    def _():
        xpx_all[:, 0:8, :] = jnp.zeros((MIX_NBAT, 8, DI_D), F32)
        xpbc_all[:, 0:8, :] = jnp.zeros((MIX_NBAT, 8, 512), F32)
        s_all[...] = jnp.zeros_like(s_all)

    for u in range(MIX_NBAT):
        _ssd_chunk(x_all.at[u], bc_all.at[u], z_all.at[u], dt_all.at[u], cwx_ref, cwbc_ref, cbx_ref, cbbc_ref,
                   dtb_ref, alog_ref, dskip_ref, nw_ref, od_all.at[u], cv_all.at[u],
                   xpx_all.at[u], xpbc_all.at[u], s_all.at[u], y_all.at[u])

    @pl.when(c == nc - 1)
    def _():
        st_all[...] = s_all[...]


def _ssd_chunk(x_ref, bc_ref, z_ref, dt_ref, cwx_ref, cwbc_ref, cbx_ref, cbbc_ref, dtb_ref, alog_ref,
               dskip_ref, nw_ref, od_ref, cv_ref, xpx_scr, xpbc_scr, s_scr, y_scr):
    xpx_scr[8:8 + CHUNK, :] = x_ref[...]
    xpbc_scr[8:8 + CHUNK, :] = bc_ref[...]
    cx = cbx_ref[...]
    cbc = cbbc_ref[...]
    for k in range(CONV_W):
        w = CONV_W - 1 - k
        cx = cx + cwx_ref[w:w + 1, :] * xpx_scr[8 - k:8 - k + CHUNK, :]
        cbc = cbc + cwbc_ref[w:w + 1, :] * xpbc_scr[8 - k:8 - k + CHUNK, :]

    cv_ref[:, 0:DI_D] = xpx_scr[CHUNK + 5:CHUNK + 8, :]
    cv_ref[:, DI_D:CONV_CH] = xpbc_scr[CHUNK + 5:CHUNK + 8, :]
    xpx_scr[0:8, :] = xpx_scr[CHUNK:CHUNK + 8, :]
    xpbc_scr[0:8, :] = xpbc_scr[CHUNK:CHUNK + 8, :]

    xs = _silu(cx)
    bcs = _silu(cbc)
    dt = _softplus(dt_ref[...] + dtb_ref[...])
    la = dt * (-jnp.exp(alog_ref[...]))
    cum = _exact_left01(_tri_lower(CHUNK), la)
    cum_t = jnp.transpose(cum)
    t_i = lax.broadcasted_iota(jnp.int32, (CHUNK, CHUNK), 0)
    s_i = lax.broadcasted_iota(jnp.int32, (CHUNK, CHUNK), 1)
    causal = t_i >= s_i
    lo = s_i < 64
    att_base, bt = [], []
    for g in range(G_D):
        bg = bcs[:, 128 * g:128 * (g + 1)]
        cg = bcs[:, 256 + 128 * g:256 + 128 * (g + 1)]
        att_base.append(_dot_nt(cg.astype(BF16), bg.astype(BF16)))
        bt.append(jnp.transpose(bg).astype(BF16))
    ss = jnp.zeros((CHUNK, 1), F32)
    for p in range(H_D // 2):
        g = (2 * p) // (H_D // G_D)
        sl = slice(128 * p, 128 * (p + 1))
        le, lo_ = DT_LANE + 2 * p, DT_LANE + 2 * p + 1
        dt_pair = jnp.where(lo, _lane_col(dt, le), _lane_col(dt, lo_))
        cum_pair = jnp.where(lo, _lane_col(cum, le), _lane_col(cum, lo_))
        clast = cum_pair[CHUNK - 1:CHUNK, :]
        xs_p = xs[:, sl]
        xdt = xs_p * dt_pair
        s_old = s_scr[p]
        cgb = bcs[:, 256 + 128 * g:256 + 128 * (g + 1)].astype(BF16)
        y = _dot(cgb, s_old.astype(BF16)) * jnp.exp(cum_pair)
        for e in range(2):
            ln = DT_LANE + 2 * p + e
            diff = _lane_col(cum, ln) - cum_t[ln:ln + 1, :]
            att = (att_base[g] * jnp.exp(jnp.where(causal, diff, NEG))).astype(BF16)
            xm = jnp.where(lo if e == 0 else ~lo, xdt, 0.0).astype(BF16)
            y = y + _dot(att, xm)
        y = (y + xs_p * dskip_ref[:, sl]) * _silu(z_ref[:, sl])
        y_scr[:, sl] = y
        ss = ss + jnp.sum(y * y, axis=-1, keepdims=True)
        txdt = (xdt * jnp.exp(clast - cum_pair)).astype(BF16)
        s_scr[p] = jnp.exp(clast) * s_old + _dot(bt[g], txdt)
    inv = lax.rsqrt(ss * (1.0 / DI_D) + EPS)
    od_ref[...] = (y_scr[...] * inv * nw_ref[...]).astype(BF16)


def _ssd_prompt(p_main, p_small, cwx, cwbc, cbx, cbbc, dtb, alog, dskip, nw, *, nb, nc):
    seq = nc * CHUNK
    ng = nb // MIX_NBAT
    pg, psg = _batch_groups(p_main, nb, seq), _batch_groups(p_small, nb, seq)
    blk = lambda w, col: pl.BlockSpec((None, MIX_NBAT, CHUNK, w), lambda b, c: (b, 0, c, col))
    c2 = lambda b, c: (0, 0)
    od, st, cv = pl.pallas_call(
        _ssd_prompt_kernel,
        grid=(ng, nc),
        in_specs=[
            blk(1024, 4), blk(512, 10), blk(1024, 3), blk(ODD_SMALL, 0),
            pl.BlockSpec((CONV_W, DI_D), c2), pl.BlockSpec((CONV_W, 512), c2),
            pl.BlockSpec((1, DI_D), c2), pl.BlockSpec((1, 512), c2),
            pl.BlockSpec((1, ODD_SMALL), c2), pl.BlockSpec((1, ODD_SMALL), c2),
            pl.BlockSpec((1, DI_D), c2), pl.BlockSpec((1, DI_D), c2),
        ],
        out_specs=[
            blk(DI_D, 0),
            pl.BlockSpec((None, MIX_NBAT, 8, 128, 128), lambda b, c: (b, 0, 0, 0, 0)),
            pl.BlockSpec((None, MIX_NBAT, CONV_W - 1, CONV_CH), lambda b, c: (b, 0, 0, 0)),
        ],
        out_shape=[jax.ShapeDtypeStruct((ng, MIX_NBAT, seq, DI_D), BF16),
                   jax.ShapeDtypeStruct((ng, MIX_NBAT, 8, 128, 128), F32),
                   jax.ShapeDtypeStruct((ng, MIX_NBAT, CONV_W - 1, CONV_CH), F32)],
        scratch_shapes=[pltpu.VMEM((MIX_NBAT, CHUNK + 8, DI_D), F32), pltpu.VMEM((MIX_NBAT, CHUNK + 8, 512), F32),
                        pltpu.VMEM((MIX_NBAT, 8, 128, 128), F32), pltpu.VMEM((MIX_NBAT, CHUNK, DI_D), F32)],
        compiler_params=_cparams(("arbitrary", "arbitrary")),
        name="ssd_prompt",
    )(pg, pg, pg, psg, cwx, cwbc, cbx, cbbc, dtb, alog, dskip, nw)
    return od.reshape(nb * seq, DI_D), st.reshape(nb, 8, 128, 128), cv.reshape(nb, CONV_W - 1, CONV_CH)


def _ssd_sample_kernel(p_all, dt_all, cs_all, s0_all, cwx_ref, cwbc_ref, cbx_ref, cbbc_ref, dtb_ref, alog_ref,
                       dskip_ref, nw_ref, od_all, st_all, cv_all):
    for u in range(SAMPLE_NSEQ):
        _ssd_sample_one(p_all.at[u], dt_all.at[u], cs_all.at[u], s0_all.at[u], cwx_ref, cwbc_ref, cbx_ref,
                        cbbc_ref, dtb_ref, alog_ref, dskip_ref, nw_ref, od_all.at[u], st_all.at[u], cv_all.at[u])


def _ssd_sample_one(p_ref, dt_ref, cs_ref, s0_ref, cwx_ref, cwbc_ref, cbx_ref, cbbc_ref, dtb_ref, alog_ref,
                    dskip_ref, nw_ref, od_ref, st_ref, cv_ref):
    row = p_ref[...]
    xn = row[:, 4096:5120]
    bcn = row[:, 5120:5632]
    cx = cbx_ref[...] + cwx_ref[3:4, :] * xn
    cbc = cbbc_ref[...] + cwbc_ref[3:4, :] * bcn
    for w in range(CONV_W - 1):
        cx = cx + cwx_ref[w:w + 1, :] * cs_ref[w:w + 1, 0:DI_D]
        cbc = cbc + cwbc_ref[w:w + 1, :] * cs_ref[w:w + 1, DI_D:CONV_CH]
    cv_ref[0:1, :] = cs_ref[1:2, :]
    cv_ref[1:2, :] = cs_ref[2:3, :]
    cv_ref[2:3, 0:DI_D] = xn
    cv_ref[2:3, DI_D:CONV_CH] = bcn
    xs = _silu(cx)
    bcs = _silu(cbc)
    dt = _softplus(dt_ref[...] + dtb_ref[...])
    da = jnp.exp(dt * (-jnp.exp(alog_ref[...])))
    lane = lax.broadcasted_iota(jnp.int32, (1, 128), 1)
    lo = lane < 64
    bcol = [_col_bcast(bcs[:, 128 * g:128 * (g + 1)]) for g in range(G_D)]
    ccol = [_col_bcast(bcs[:, 256 + 128 * g:256 + 128 * (g + 1)]) for g in range(G_D)]
    ys = []
    ss = jnp.zeros((1, 1), F32)
    for p in range(H_D // 2):
        g = (2 * p) // (H_D // G_D)
        sl = slice(128 * p, 128 * (p + 1))
        le, lo_ = DT_LANE + 2 * p, DT_LANE + 2 * p + 1
        dt_pair = jnp.where(lo, _lane_col(dt, le), _lane_col(dt, lo_))
        da_pair = jnp.where(lo, _lane_col(da, le), _lane_col(da, lo_))
        xs_p = xs[:, sl]
        s_new = da_pair * s0_ref[p] + bcol[g] * (xs_p * dt_pair)
        st_ref[p] = s_new
        y = jnp.sum(ccol[g] * s_new, axis=0, keepdims=True)
        y = (y + xs_p * dskip_ref[:, sl]) * _silu(row[:, 3072 + 128 * p:3072 + 128 * (p + 1)])
        ys.append(y)
        ss = ss + jnp.sum(y * y, axis=-1, keepdims=True)
    inv = lax.rsqrt(ss * (1.0 / DI_D) + EPS)
    for p in range(H_D // 2):
        sl = slice(128 * p, 128 * (p + 1))
        od_ref[:, sl] = ys[p] * inv * nw_ref[:, sl]


def _ssd_sample(p_main, p_small, cs, s0, cwx, cwbc, cbx, cbbc, dtb, alog, dskip, nw):
    nb = p_main.shape[0]
    b3 = lambda b: (b, 0, 0)
    b4 = lambda b: (b, 0, 0, 0)
    c2 = lambda b: (0, 0)
    ns = SAMPLE_NSEQ
    return pl.pallas_call(
        _ssd_sample_kernel,
        grid=(nb // ns,),
        in_specs=[
            pl.BlockSpec((ns, 1, ODD_MAIN), b3),
            pl.BlockSpec((ns, 1, ODD_SMALL), b3),
            pl.BlockSpec((ns, CONV_W - 1, CONV_CH), b3),
            pl.BlockSpec((ns, 8, 128, 128), b4),
            pl.BlockSpec((CONV_W, DI_D), c2), pl.BlockSpec((CONV_W, 512), c2),
            pl.BlockSpec((1, DI_D), c2), pl.BlockSpec((1, 512), c2),
            pl.BlockSpec((1, ODD_SMALL), c2), pl.BlockSpec((1, ODD_SMALL), c2),
            pl.BlockSpec((1, DI_D), c2), pl.BlockSpec((1, DI_D), c2),
        ],
        out_specs=[
            pl.BlockSpec((ns, 1, DI_D), b3),
            pl.BlockSpec((ns, 8, 128, 128), b4),
            pl.BlockSpec((ns, CONV_W - 1, CONV_CH), b3),
        ],
        out_shape=[jax.ShapeDtypeStruct((nb, 1, DI_D), F32), jax.ShapeDtypeStruct((nb, 8, 128, 128), F32),
                   jax.ShapeDtypeStruct((nb, CONV_W - 1, CONV_CH), F32)],
        compiler_params=_cparams(("arbitrary",)),
        name="ssd_sample",
    )(p_main.reshape(nb, 1, ODD_MAIN), p_small.reshape(nb, 1, ODD_SMALL), cs, s0,
      cwx, cwbc, cbx, cbbc, dtb, alog, dskip, nw)


def _pairs_to_heads(s):
    b, p, n, _ = s.shape
    return s.reshape(b, p, n, 2, 64).transpose(0, 1, 3, 2, 4).reshape(b, 2 * p, n, 64)


def _heads_to_pairs(s):
    b, h, n, d = s.shape
    return s.reshape(b, h // 2, 2, n, d).transpose(0, 1, 3, 2, 4).reshape(b, h // 2, n, 2 * d)


def _pad_lanes(v, start, width=ODD_SMALL):
    out = jnp.zeros((1, width), F32)
    return lax.dynamic_update_slice(out, v.reshape(1, -1).astype(F32), (0, start))


def kernel(x_prompt, x_sample, state_ret, cache_swa_k, cache_swa_v, cache_fox_k, cache_fox_v, cache_fox_logf,
           state_ssm, state_conv, page_table, norm_mix_pre, norm_mix_post, norm_mlp_pre, norm_mlp_post,
           w_in_even, w_out_even, ret_norm_w, swa_sinks, w_in_odd, w_out_odd, fox_fb, conv_w, conv_b,
           dt_bias, a_log, d_skip, ssd_norm_w, w_up, w_down):
    nb, seq = BATCH, SEQ
    nc = seq // CHUNK
    mp = nb * seq
    ms = DEC_BATCH
    xp = x_prompt.reshape(mp, D_MODEL)
    xs = x_sample.reshape(ms, D_MODEL)
    row = lambda v: v.reshape(1, -1)

    pos_p = jnp.arange(seq, dtype=jnp.int32)
    pos_s = jnp.full((1,), PAST_LEN, dtype=jnp.int32)
    ca_p, sa_p = _rope_tables(pos_p, RET_THETA, 8)
    cb_p, sb_p = _rope_tables(pos_p, ROPE_THETA_B, 2)
    ca_s, sa_s = _rope_tables(pos_s, RET_THETA, 8)
    cb_s, sb_s = _rope_tables(pos_s, ROPE_THETA_B, 2)

    we = w_in_even[0]
    we_main = we[:, :EVEN_MAIN].astype(BF16)
    we_small = we[:, EVEN_MAIN:].astype(BF16)
    wo = w_out_even[0].astype(BF16)
    wo_a, wo_b = wo[:1024], wo[1024:]
    g_pre, g_post = row(norm_mix_pre[0]), row(norm_mix_post[0])
    gm_pre, gm_post = row(norm_mlp_pre[0]), row(norm_mlp_post[0])
    gn_w = row(ret_norm_w[0])
    sinks = swa_sinks[0]

    pm, psm, sm, ssm_, wu_b, wd_b = _proj(xp, xs, g_pre, (we_main,), we_small, w_up, w_down, 0,
                                          odd=False, tm=256, tn=512, seq=seq)
    out_a, ret_p = _ret_prompt(pm, ca_p, sa_p, gn_w, nb=nb, nc=nc)
    out_b, swak_p, swav_p = _swa_prompt(pm, psm, cb_p, sb_p, sinks, nb=nb, nc=nc)
    mix_s, ret_s, swak_s, swav_s = _even_sample(
        sm, ssm_, state_ret[0].reshape(ms, 4, 128, 128), cache_swa_k[0].reshape(ms, WINDOW, 128),
        cache_swa_v[0].reshape(ms, WINDOW, 128), ca_s, sa_s, cb_s, sb_s, gn_w, sinks)
    mix_s = mix_s.reshape(ms, 2048).astype(BF16)
    xp, xs = _outproj(out_a, out_b, xp, mix_s[:, :1024], mix_s[:, 1024:], xs, wo_a, wo_b, g_post, tm=512)
    xp, xs = _mlp(xp, xs, gm_pre, gm_post, wu_b, wd_b, tm=512, tf=1024)

    wod = w_in_odd[0]
    wod_qkv = wod[:, :3072].astype(BF16)
    wod_zx = wod[:, 3080:5640].astype(BF16)
    wod_small = jnp.concatenate(
        [wod[:, 3072:3080], wod[:, 5640:5656], jnp.zeros((D_MODEL, ODD_SMALL - 24), F32)], axis=1).astype(BF16)
    wo1 = w_out_odd[0].astype(BF16)
    wo_c, wo_d = wo1[:1024], wo1[1024:]
    g_pre, g_post = row(norm_mix_pre[1]), row(norm_mix_post[1])
    gm_pre, gm_post = row(norm_mlp_pre[1]), row(norm_mlp_post[1])
    fb = _pad_lanes(fox_fb[0], 0)
    dtb = _pad_lanes(dt_bias[0], DT_LANE)
    alog = _pad_lanes(a_log[0], DT_LANE)
    cw = conv_w[0]
    cwx, cwbc = cw[:, :DI_D], cw[:, DI_D:]
    cbx, cbbc = row(conv_b[0][:DI_D]), row(conv_b[0][DI_D:])
    dskip = row(jnp.repeat(d_skip[0], HD_D))
    nw = row(ssd_norm_w[0])

    pm, psm, sm, ssm_, wu_b, wd_b, lf_p, fc_p, lf_s, k3_p, v3_p = _proj(
        xp, xs, g_pre, (wod_qkv, wod_zx), wod_small, w_up, w_down, 1, fb, odd=True, tm=256, tn=512, seq=seq)
    fc = fc_p[:, :H_C].reshape(nb, seq, H_C).transpose(0, 2, 1)
    out_d, ssm_pairs_p, conv_p = _ssd_prompt(pm, psm, cwx, cwbc, cbx, cbbc, dtb, alog, dskip, nw, nb=nb, nc=nc)
    fox_k_p = k3_p.reshape(1, nb, seq, H_C, HD_C)
    fox_v_p = v3_p.reshape(1, nb, seq, H_C, HD_C)
    fox_lf_p = lf_p[:, :H_C].reshape(1, nb, seq, H_C)

    q_s = sm[:, 0:1024].reshape(ms, H_C, HD_C)
    k_s = sm[:, 1024:2048].reshape(ms, H_C, HD_C)
    v_s = sm[:, 2048:3072].reshape(ms, H_C, HD_C)
    lf_s8 = lf_s[:, :H_C]
    out_c_s, out_c = _fox(page_table, q_s, k_s, v_s, lf_s8.reshape(ms, H_C, 1),
                          cache_fox_k[0], cache_fox_v[0], cache_fox_logf[0].transpose(0, 2, 1),
                          pm, fc[..., None], fc[:, :, None, :], pp=16, nb_p=nb, seq=seq)
    out_d_s, ssm_pairs_s, conv_s = _ssd_sample(sm, ssm_, state_conv[0], _heads_to_pairs(state_ssm[0]),
                                               cwx, cwbc, cbx, cbbc, dtb, alog, dskip, nw)
    xp, xs = _outproj(out_c, out_d, xp, out_c_s.reshape(ms, 1024).astype(BF16),
                      out_d_s.reshape(ms, DI_D).astype(BF16), xs, wo_c, wo_d, g_post, tm=512)
    xp, xs = _mlp(xp, xs, gm_pre, gm_post, wu_b, wd_b, tm=512, tf=1024)

    return (
        xp.reshape(nb, seq, D_MODEL), xs.reshape(ms, 1, D_MODEL),
        ret_p.reshape(1, nb, H_A, DK_A, DV_A), ret_s.reshape(1, ms, H_A, DK_A, DV_A),
        swak_p.reshape(1, nb, WINDOW, KV_B, HD_B), swav_p.reshape(1, nb, WINDOW, KV_B, HD_B),
        swak_s.reshape(1, ms, WINDOW, KV_B, HD_B), swav_s.reshape(1, ms, WINDOW, KV_B, HD_B),
        fox_k_p, fox_v_p, fox_lf_p,
        k_s.reshape(1, ms, 1, H_C, HD_C), v_s.reshape(1, ms, 1, H_C, HD_C), lf_s8.reshape(1, ms, 1, H_C),
        _pairs_to_heads(ssm_pairs_p)[None], _pairs_to_heads(ssm_pairs_s)[None],
        conv_p[None], conv_s[None],
    )
```

```python
import functools

import numpy as np
import jax
import jax.numpy as jnp
from jax import lax
from jax.experimental import pallas as pl
from jax.experimental.pallas import tpu as pltpu

F32 = jnp.float32
BF16 = jnp.bfloat16

D_MODEL = 2048
BATCH = 4
SEQ = 2048
DEC_BATCH = 32
PAST_LEN = 16384
PAGE_SIZE = 128
D_FF = 4 * D_MODEL
EPS = 1e-6
GN_EPS = 1e-5
CHUNK = 128

H_A, DK_A, DV_A = 8, 64, 128
RET_THETA = 10000.0
H_B, KV_B, HD_B = 16, 2, 64
WINDOW = 128
ROPE_THETA_B = 150000.0
H_C, HD_C = 8, 128
H_D, HD_D, G_D, N_D = 16, 64, 2, 128
CONV_W = 4
DI_D = H_D * HD_D
CONV_CH = DI_D + 2 * G_D * N_D

EVEN_MAIN = 4096
EVEN_SMALL = 256
ODD_MAIN = 5632
ODD_SMALL = 128
DT_LANE = 8

NEG = -1e30
VMEM_LIMIT = 56 * 1024 * 1024
MLP_VMEM_LIMIT = 60 * 1024 * 1024
MLP_SUB_ROWS = 256
MIX_NBAT = 2
SAMPLE_NSEQ = 4

_LOG_GAMMA = [float(v) for v in np.log1p(-np.exp2(-5.0 - np.arange(H_A, dtype=np.float64)))]


def _cparams(sem, vmem_limit=VMEM_LIMIT):
    return pltpu.CompilerParams(dimension_semantics=sem, vmem_limit_bytes=vmem_limit)


def _silu(x):
    return x * jax.nn.sigmoid(x)


def _softplus(x):
    return jnp.maximum(x, 0.0) + jnp.log1p(jnp.exp(-jnp.abs(x)))


def _log_sigmoid(x):
    return jnp.minimum(x, 0.0) - jnp.log1p(jnp.exp(-jnp.abs(x)))


def _rms(x, g):
    ms = jnp.mean(x * x, axis=-1, keepdims=True)
    return (x * lax.rsqrt(ms + EPS)) * g


def _dot(a, b):
    return jnp.dot(a, b, preferred_element_type=F32)


def _dot_nt(a, b):
    return lax.dot_general(a, b, (((1,), (1,)), ((), ())), preferred_element_type=F32)


def _split3(x):
    hi = x.astype(BF16)
    r = x - hi.astype(F32)
    mid = r.astype(BF16)
    lo = (r - mid.astype(F32)).astype(BF16)
    return hi, mid, lo


def _exact_left01(m01, x):
    hi, mid, lo = _split3(x)
    return _dot(m01, hi) + _dot(m01, mid) + _dot(m01, lo)


def _exact_right01(x, m01):
    hi, mid, lo = _split3(x)
    return _dot(hi, m01) + _dot(mid, m01) + _dot(lo, m01)


def _tri_lower(n):
    r = lax.broadcasted_iota(jnp.int32, (n, n), 0)
    c = lax.broadcasted_iota(jnp.int32, (n, n), 1)
    return jnp.where(r >= c, 1.0, 0.0).astype(BF16)


def _rope64(x, c, s):
    w = x.shape[-1]
    ax = x.ndim - 1
    lane = lax.broadcasted_iota(jnp.int32, x.shape, ax)
    first = (lane & 32) == 0
    left = pltpu.roll(x, w - 32, axis=ax)
    right = pltpu.roll(x, 32, axis=ax)
    return x * c + jnp.where(first, left, right) * s


def _rope_tables(pos, theta, reps):
    inv = 1.0 / (theta ** (jnp.arange(32, dtype=F32) * (2.0 / 64)))
    ang = pos.astype(F32)[:, None] * inv[None, :]
    cos, sin = jnp.cos(ang), jnp.sin(ang)
    c = jnp.concatenate([cos, cos], axis=-1)
    s = jnp.concatenate([-sin, sin], axis=-1)
    return jnp.tile(c, (1, reps)), jnp.tile(s, (1, reps))


def _proj_kernel(*refs, odd, n_w, n_c, tm, tn, seq):
    x_ref, xs_ref, g_ref = refs[0:3]
    w_refs = refs[3:3 + n_w]
    wsm_ref = refs[3 + n_w]
    cast_src = refs[4 + n_w:4 + n_w + n_c]
    refs = refs[4 + n_w + n_c:]
    if odd:
        fb_ref, refs = refs[0], refs[1:]
    o_ref, osm_ref, os_ref, ossm_ref = refs[0:4]
    cast_dst = refs[4:4 + n_c]
    refs = refs[4 + n_c:]
    kv_refs = None
    if odd:
        lf_ref, fc_ref, lfs_ref, k3_ref, v3_ref, h_scr, carry_scr = refs
        kv_refs = (k3_ref, v3_ref)
    else:
        (h_scr,) = refs
    i = pl.program_id(0)
    for src, dst in zip(cast_src, cast_dst):
        dst[...] = src[...].astype(BF16)
    width = H_C * HD_C

    def rows(x_r, o_r, osm_r, n, kv=None):
        h_scr[0:n, :] = _rms(x_r[...], g_ref[...]).astype(BF16)
        sm = _dot(h_scr[0:n, :], wsm_ref[...].astype(BF16))
        osm_r[...] = sm
        base = 0
        for w_ref in w_refs:
            for t in range(w_ref.shape[1] // tn):
                c0 = base + t * tn
                val = _dot(h_scr[0:n, :], w_ref[:, t * tn:(t + 1) * tn])
                o_r[:, c0:c0 + tn] = val
                if kv is not None and width <= c0 < 3 * width:
                    dst = kv[(c0 - width) // width]
                    h0 = ((c0 - width) % width) // HD_C
                    for hh in range(tn // HD_C):
                        dst[:, h0 + hh, :] = val[:, HD_C * hh:HD_C * (hh + 1)]
            base += w_ref.shape[1]
        return sm

    sm = rows(x_ref, o_ref, osm_ref, tm, kv_refs)
    if odd:
        lf = _log_sigmoid(sm + fb_ref[...])
        lf_ref[...] = lf

        @pl.when((i * tm) % seq == 0)
        def _():
            carry_scr[...] = jnp.zeros_like(carry_scr)

        f = _exact_left01(_tri_lower(tm), lf) + carry_scr[...]
        fc_ref[...] = f
        carry_scr[...] = f[tm - 1:tm, :]

    @pl.when(i == 0)
    def _():
        sms = rows(xs_ref, os_ref, ossm_ref, xs_ref.shape[0])
        if odd:
            lfs_ref[...] = _log_sigmoid(sms + fb_ref[...])


def _proj(x, xs, g, w_mains, w_small, casts, fb=None, *, odd, tm, tn, seq):
    m, ms = x.shape[0], xs.shape[0]
    n_main = sum(w.shape[1] for w in w_mains)
    n_small = w_small.shape[1]
    ni = m // tm
    c2 = lambda i: (0, 0)
    r2 = lambda i: (i, 0)
    in_specs = [
        pl.BlockSpec((tm, D_MODEL), r2),
        pl.BlockSpec((ms, D_MODEL), c2),
        pl.BlockSpec((1, D_MODEL), c2),
    ] + [pl.BlockSpec((D_MODEL, w.shape[1]), c2, pipeline_mode=pl.Buffered(1)) for w in w_mains] + [
        pl.BlockSpec((D_MODEL, n_small), c2),
    ]
    args = [x, xs, g, *w_mains, w_small]
    small = jax.ShapeDtypeStruct((m, n_small), F32)
    small_s = jax.ShapeDtypeStruct((ms, n_small), F32)
    small_spec = pl.BlockSpec((tm, n_small), r2)
    small_s_spec = pl.BlockSpec((ms, n_small), c2)
    out_shape = [jax.ShapeDtypeStruct((m, n_main), F32), small, jax.ShapeDtypeStruct((ms, n_main), F32), small_s]
    out_specs = [pl.BlockSpec((tm, n_main), r2), small_spec, pl.BlockSpec((ms, n_main), c2), small_s_spec]
    for src, lead, nrows, ncols in casts:
        rb = nrows // ni
        if lead is None:
            in_specs.append(pl.BlockSpec((rb, ncols), r2))
        else:
            in_specs.append(pl.BlockSpec((None, rb, ncols), lambda i, lead=lead: (lead, i, 0)))
        args.append(src)
        out_shape.append(jax.ShapeDtypeStruct((nrows, ncols), BF16))
        out_specs.append(pl.BlockSpec((rb, ncols), r2))
    scratch = [pltpu.VMEM((tm, D_MODEL), BF16)]
    if odd:
        in_specs.append(pl.BlockSpec((1, n_small), c2))
        args.append(fb)
        rows3 = jax.ShapeDtypeStruct((m, H_C, HD_C), F32)
        rows3_spec = pl.BlockSpec((tm, H_C, HD_C), lambda i: (i, 0, 0))
        out_shape += [small, small, small_s, rows3, rows3]
        out_specs += [small_spec, small_spec, small_s_spec, rows3_spec, rows3_spec]
        scratch.append(pltpu.VMEM((1, n_small), F32))
    return pl.pallas_call(
        functools.partial(_proj_kernel, odd=odd, n_w=len(w_mains), n_c=len(casts), tm=tm, tn=tn, seq=seq),
        grid=(ni,), in_specs=in_specs, out_specs=out_specs, out_shape=out_shape,
        scratch_shapes=scratch, compiler_params=_cparams(("arbitrary",), MLP_VMEM_LIMIT),
        name="proj_odd" if odd else "proj_even",
    )(*args)


def _outproj_kernel(a_ref, b_ref, x_ref, as_ref, bs_ref, xs_ref, wa_ref, wb_ref, g_ref, o_ref, os_ref):
    g = g_ref[...]
    n = x_ref.shape[0]
    sub = min(n, MLP_SUB_ROWS)
    for r in range(n // sub):
        rows = slice(r * sub, (r + 1) * sub)
        y = _dot(a_ref[rows, :], wa_ref[...]) + _dot(b_ref[rows, :], wb_ref[...])
        o_ref[rows, :] = x_ref[rows, :] + _rms(y, g)

    @pl.when(pl.program_id(0) == 0)
    def _():
        ys = _dot(as_ref[...], wa_ref[...]) + _dot(bs_ref[...], wb_ref[...])
        os_ref[...] = xs_ref[...] + _rms(ys, g)


def _outproj(a, b, x, a_s, b_s, xs, wa, wb, g, *, tm):
    m, ms = x.shape[0], xs.shape[0]
    ka, kb = a.shape[1], b.shape[1]
    c2 = lambda i: (0, 0)
    return pl.pallas_call(
        _outproj_kernel,
        grid=(m // tm,),
        in_specs=[
            pl.BlockSpec((tm, ka), lambda i: (i, 0)),
            pl.BlockSpec((tm, kb), lambda i: (i, 0)),
            pl.BlockSpec((tm, D_MODEL), lambda i: (i, 0)),
            pl.BlockSpec((ms, ka), c2),
            pl.BlockSpec((ms, kb), c2),
            pl.BlockSpec((ms, D_MODEL), c2),
            pl.BlockSpec((ka, D_MODEL), c2),
            pl.BlockSpec((kb, D_MODEL), c2),
            pl.BlockSpec((1, D_MODEL), c2),
        ],
        out_specs=[pl.BlockSpec((tm, D_MODEL), lambda i: (i, 0)), pl.BlockSpec((ms, D_MODEL), c2)],
        out_shape=[jax.ShapeDtypeStruct((m, D_MODEL), F32), jax.ShapeDtypeStruct((ms, D_MODEL), F32)],
        compiler_params=_cparams(("arbitrary",)),
        name="outproj",
    )(a, b, x, a_s, b_s, xs, wa, wb, g)


def _mlp_kernel(x_ref, xs_ref, gpre_ref, gpost_ref, wu_ref, wd_ref, o_ref, os_ref, h_scr, hs_scr):
    i = pl.program_id(0)
    j = pl.program_id(1)
    last = pl.num_programs(1) - 1

    def group(x_r, o_r, h_s):
        @pl.when(j == 0)
        def _():
            h_s[...] = _rms(x_r[...], gpre_ref[...]).astype(BF16)
            o_r[...] = jnp.zeros_like(o_r)

        n = x_r.shape[0]
        sub = min(n, MLP_SUB_ROWS)
        for r in range(n // sub):
            rows = slice(r * sub, (r + 1) * sub)
            u = jnp.maximum(_dot(h_s[rows, :], wu_ref[...]), 0.0)
            o_r[rows, :] += _dot((u * u).astype(BF16), wd_ref[...])

        @pl.when(j == last)
        def _():
            o_r[...] = x_r[...] + _rms(o_r[...], gpost_ref[...])

    group(x_ref, o_ref, h_scr)

    @pl.when(i == 0)
    def _():
        group(xs_ref, os_ref, hs_scr)


def _mlp(x, xs, gpre, gpost, w_up, w_down, *, tm, tf):
    m, ms = x.shape[0], xs.shape[0]
    c2 = lambda i, j: (0, 0)
    return pl.pallas_call(
        _mlp_kernel,
        grid=(m // tm, D_FF // tf),
        in_specs=[
            pl.BlockSpec((tm, D_MODEL), lambda i, j: (i, 0)),
            pl.BlockSpec((ms, D_MODEL), c2),
            pl.BlockSpec((1, D_MODEL), c2),
            pl.BlockSpec((1, D_MODEL), c2),
            pl.BlockSpec((D_MODEL, tf), lambda i, j: (0, j)),
            pl.BlockSpec((tf, D_MODEL), lambda i, j: (j, 0)),
        ],
        out_specs=[pl.BlockSpec((tm, D_MODEL), lambda i, j: (i, 0)), pl.BlockSpec((ms, D_MODEL), c2)],
        out_shape=[jax.ShapeDtypeStruct((m, D_MODEL), F32), jax.ShapeDtypeStruct((ms, D_MODEL), F32)],
        scratch_shapes=[pltpu.VMEM((tm, D_MODEL), BF16), pltpu.VMEM((ms, D_MODEL), BF16)],
        compiler_params=_cparams(("arbitrary", "arbitrary"), MLP_VMEM_LIMIT),
        name="mlp",
    )(x, xs, gpre, gpost, w_up, w_down)


def _group_norm_gate(y, gate, gw):
    mu = jnp.mean(y, axis=-1, keepdims=True)
    d = y - mu
    var = jnp.mean(d * d, axis=-1, keepdims=True)
    return _silu(gate) * (d * lax.rsqrt(var + GN_EPS) * gw)


def _ret_prompt_kernel(q_all, k_all, v_all, g_all, cos_ref, sin_ref, gn_ref, oa_all, st_all,
                       s_all, d_scr, e_scr, t_scr):
    b = pl.program_id(0)
    c = pl.program_id(1)
    t_i = lax.broadcasted_iota(jnp.int32, (CHUNK, CHUNK), 0)
    s_i = lax.broadcasted_iota(jnp.int32, (CHUNK, CHUNK), 1)
    lo = s_i < 64

    @pl.when((b == 0) & (c == 0))
    def _():
        tf = t_i.astype(F32)
        sf = s_i.astype(F32)
        for h in range(H_A):
            d_scr[h] = jnp.where(t_i >= s_i, jnp.exp((tf - sf) * _LOG_GAMMA[h]), 0.0)
            e_scr[h] = jnp.exp((tf + 1.0) * _LOG_GAMMA[h])
        for p in range(H_A // 2):
            lg = jnp.where(lo, _LOG_GAMMA[2 * p], _LOG_GAMMA[2 * p + 1])
            t_scr[p] = jnp.exp((CHUNK - 1.0 - tf) * lg)

    @pl.when(c == 0)
    def _():
        s_all[...] = jnp.zeros_like(s_all)

    cos, sin = cos_ref[...], sin_ref[...]
    top = t_i < 64
    for u in range(MIX_NBAT):
        v_ref, g_ref, oa_ref, s_scr = v_all.at[u], g_all.at[u], oa_all.at[u], s_all.at[u]
        qr = _rope64(q_all[u], cos, sin)
        kr = _rope64(k_all[u], cos, sin) * (DK_A ** -0.5)
        for p in range(H_A // 2):
            sl = slice(128 * p, 128 * (p + 1))
            qp, kp = qr[:, sl], kr[:, sl]
            kb = kp.astype(BF16)
            s_old = s_scr[p]
            s_old_b = s_old.astype(BF16)
            ktt = jnp.transpose(kp * t_scr[p]).astype(BF16)
            upd = []
            for e in range(2):
                h = 2 * p + e
                hs = slice(128 * h, 128 * (h + 1))
                qm = jnp.where(lo if e == 0 else ~lo, qp, 0.0).astype(BF16)
                vh = v_ref[:, hs].astype(BF16)
                att = (_dot_nt(qm, kb) * d_scr[h]).astype(BF16)
                y = _dot(att, vh) + _dot(qm, s_old_b) * e_scr[h]
                oa_ref[:, hs] = _group_norm_gate(y, g_ref[:, hs], gn_ref[:, hs]).astype(BF16)
                upd.append(_dot(ktt, vh))
            g128 = jnp.where(top, float(np.exp(CHUNK * _LOG_GAMMA[2 * p])),
                             float(np.exp(CHUNK * _LOG_GAMMA[2 * p + 1])))
            s_scr[p] = g128 * s_old + jnp.where(top, upd[0], upd[1])

    @pl.when(c == pl.num_programs(1) - 1)
    def _():
        st_all[...] = s_all[...]


def _batch_groups(a, nb, seq):
    return a.reshape(nb // MIX_NBAT, MIX_NBAT, seq, a.shape[-1])


def _ret_prompt(p_main, cos, sin, gn_w, *, nb, nc):
    seq = nc * CHUNK
    pg = _batch_groups(p_main, nb, seq)
    blk = lambda w, col: pl.BlockSpec((None, MIX_NBAT, CHUNK, w), lambda b, c: (b, 0, c, col))
    oa, st = pl.pallas_call(
        _ret_prompt_kernel,
        grid=(nb // MIX_NBAT, nc),
        in_specs=[
            blk(512, 0), blk(512, 1), blk(1024, 1), blk(1024, 2),
            pl.BlockSpec((CHUNK, 512), lambda b, c: (c, 0)),
            pl.BlockSpec((CHUNK, 512), lambda b, c: (c, 0)),
            pl.BlockSpec((1, 1024), lambda b, c: (0, 0)),
        ],
        out_specs=[
            blk(1024, 0),
            pl.BlockSpec((None, MIX_NBAT, 4, 128, 128), lambda b, c: (b, 0, 0, 0, 0)),
        ],
        out_shape=[jax.ShapeDtypeStruct((nb // MIX_NBAT, MIX_NBAT, seq, 1024), BF16),
                   jax.ShapeDtypeStruct((nb // MIX_NBAT, MIX_NBAT, 4, 128, 128), F32)],
        scratch_shapes=[pltpu.VMEM((MIX_NBAT, 4, 128, 128), F32), pltpu.VMEM((H_A, 128, 128), F32),
                        pltpu.VMEM((H_A, 128, 128), F32), pltpu.VMEM((4, 128, 128), F32)],
        compiler_params=_cparams(("arbitrary", "arbitrary")),
        name="ret_prompt",
    )(pg, pg, pg, pg, cos, sin, gn_w)
    return oa.reshape(nb * seq, 1024), st.reshape(nb, 4, 128, 128)


def _swa_padded(x2):
    lane = lax.broadcasted_iota(jnp.int32, x2.shape, 1)
    lo = lane < 64
    xr = pltpu.roll(x2, 64, axis=1)
    z = jnp.zeros_like(x2)
    return {
        (0, 0): jnp.where(lo, x2, z).astype(BF16), (0, 1): jnp.where(lo, z, xr).astype(BF16),
        (1, 0): jnp.where(lo, xr, z).astype(BF16), (1, 1): jnp.where(lo, z, x2).astype(BF16),
    }


def _swa_attend(q_pair_fn, kpad, vpad, valid, sink_ref, store_fn, rows):
    per_group = (H_B // KV_B) // 2
    for g in range(KV_B):
        pairs = list(range(g * per_group, (g + 1) * per_group))
        qs = jnp.concatenate([q_pair_fn(jj) for jj in pairs], axis=0).astype(BF16)
        acc = None
        for e in range(2):
            s = _dot_nt(qs, kpad[(g, e)])
            if valid is not None:
                s = jnp.where(valid, s, NEG)
            sink = jnp.concatenate([jnp.full((rows, 1), sink_ref[2 * jj + e], F32) for jj in pairs], axis=0)
            mx = jnp.maximum(jnp.max(s, axis=-1, keepdims=True), sink)
            pr = jnp.exp(s - mx)
            den = jnp.sum(pr, axis=-1, keepdims=True) + jnp.exp(sink - mx)
            o = _dot(pr.astype(BF16), vpad[(g, e)]) / den
            acc = o if acc is None else acc + o
        for t, jj in enumerate(pairs):
            store_fn(jj, acc[t * rows:(t + 1) * rows])


def _swa_prompt_kernel(sink_ref, q_all, kvc_all, kvp_all, cc_ref, sc_ref, cp_ref, sp_ref,
                       ob_all, ko_all, vo_all):
    n = pl.program_id(1)
    cc, sc = cc_ref[...], sc_ref[...]
    stacked = WINDOW * (H_B // KV_B) // 2
    i = lax.broadcasted_iota(jnp.int32, (stacked, 2 * WINDOW), 0) & (WINDOW - 1)
    j = lax.broadcasted_iota(jnp.int32, (stacked, 2 * WINDOW), 1)
    valid = (j >= i + 1) & (j <= i + WINDOW) & ((n > 0) | (j >= WINDOW))
    for u in range(MIX_NBAT):
        q_ref, ob_ref = q_all.at[u], ob_all.at[u]
        kvc, kvp = kvc_all[u], kvp_all[u]
        kc = _rope64(kvc[:, :128], cc, sc)
        kp = _rope64(kvp[:, :128], cp_ref[...], sp_ref[...])
        vc = kvc[:, 128:]
        kpad = _swa_padded(jnp.concatenate([kp, kc], axis=0))
        vpad = _swa_padded(jnp.concatenate([kvp[:, 128:], vc], axis=0))

        def q_pair(jj, q_ref=q_ref):
            return _rope64(q_ref[:, 128 * jj:128 * (jj + 1)], cc, sc) * (HD_B ** -0.5)

        def store(jj, acc, ob_ref=ob_ref):
            ob_ref[:, 128 * jj:128 * (jj + 1)] = acc.astype(BF16)

        _swa_attend(q_pair, kpad, vpad, valid, sink_ref, store, WINDOW)

        @pl.when(n == pl.num_programs(1) - 1)
        def _(u=u, kc=kc, vc=vc):
            ko_all[u] = kc
            vo_all[u] = vc


def _swa_prompt(p_main, p_small, cos, sin, sinks, *, nb, nc):
    seq = nc * CHUNK
    ng = nb // MIX_NBAT
    pg, psg = _batch_groups(p_main, nb, seq), _batch_groups(p_small, nb, seq)
    prev = lambda n: jnp.maximum(n - 1, 0)
    win = pl.BlockSpec((None, MIX_NBAT, WINDOW, 128), lambda b, n: (b, 0, 0, 0))
    ob, ko, vo = pl.pallas_call(
        _swa_prompt_kernel,
        grid=(ng, nc),
        in_specs=[
            pl.BlockSpec(memory_space=pltpu.SMEM),
            pl.BlockSpec((None, MIX_NBAT, CHUNK, 1024), lambda b, n: (b, 0, n, 3)),
            pl.BlockSpec((None, MIX_NBAT, CHUNK, 256), lambda b, n: (b, 0, n, 0)),
            pl.BlockSpec((None, MIX_NBAT, CHUNK, 256), lambda b, n: (b, 0, prev(n), 0)),
            pl.BlockSpec((CHUNK, 128), lambda b, n: (n, 0)),
            pl.BlockSpec((CHUNK, 128), lambda b, n: (n, 0)),
            pl.BlockSpec((CHUNK, 128), lambda b, n: (prev(n), 0)),
            pl.BlockSpec((CHUNK, 128), lambda b, n: (prev(n), 0)),
        ],
        out_specs=[pl.BlockSpec((None, MIX_NBAT, CHUNK, 1024), lambda b, n: (b, 0, n, 0)), win, win],
        out_shape=[jax.ShapeDtypeStruct((ng, MIX_NBAT, seq, 1024), BF16),
                   jax.ShapeDtypeStruct((ng, MIX_NBAT, WINDOW, 128), F32),
                   jax.ShapeDtypeStruct((ng, MIX_NBAT, WINDOW, 128), F32)],
        compiler_params=_cparams(("arbitrary", "arbitrary")),
        name="swa_prompt",
    )(sinks, pg, psg, psg, cos, sin, cos, sin)
    return ob.reshape(nb * seq, 1024), ko.reshape(nb, WINDOW, 128), vo.reshape(nb, WINDOW, 128)


def _col_bcast(row128):
    return jnp.transpose(jnp.broadcast_to(row128, (128, 128)))


def _even_sample_kernel(sink_ref, p_all, ps_all, s0_all, kbuf_all, vbuf_all, ca_ref, sa_ref, cb_ref, sb_ref,
                        gn_ref, mix_all, st_all, ko_all, vo_all):
    for u in range(SAMPLE_NSEQ):
        _even_sample_one(sink_ref, p_all.at[u], ps_all.at[u], s0_all.at[u], kbuf_all.at[u], vbuf_all.at[u],
                         ca_ref, sa_ref, cb_ref, sb_ref, gn_ref,
                         mix_all.at[u], st_all.at[u], ko_all.at[u], vo_all.at[u])


def _even_sample_one(sink_ref, p_ref, ps_ref, s0_ref, kbuf_ref, vbuf_ref, ca_ref, sa_ref, cb_ref, sb_ref,
                     gn_ref, mix_ref, st_ref, ko_ref, vo_ref):
    row = p_ref[...]
    ca, sa = ca_ref[...], sa_ref[...]
    qr = _rope64(jnp.broadcast_to(row[:, 0:512], (8, 512)), ca, sa)
    kr = _rope64(jnp.broadcast_to(row[:, 512:1024], (8, 512)), ca, sa) * (DK_A ** -0.5)
    r_i = lax.broadcasted_iota(jnp.int32, (128, 128), 0)
    top = r_i < 64
    for p in range(H_A // 2):
        sl = slice(128 * p, 128 * (p + 1))
        kcol = _col_bcast(kr[0:1, sl])
        qcol = _col_bcast(qr[0:1, sl])
        he, ho = 2 * p, 2 * p + 1
        v_e = row[:, 1024 + 128 * he:1024 + 128 * (he + 1)]
        v_o = row[:, 1024 + 128 * ho:1024 + 128 * (ho + 1)]
        gam = jnp.where(top, float(np.exp(_LOG_GAMMA[he])), float(np.exp(_LOG_GAMMA[ho])))
        s_new = gam * s0_ref[p] + kcol * jnp.where(top, v_e, v_o)
        st_ref[p] = s_new
        prod = qcol * s_new
        for e, h in ((0, he), (1, ho)):
            y = jnp.sum(jnp.where(top if e == 0 else ~top, prod, 0.0), axis=0, keepdims=True)
            hs = slice(128 * h, 128 * (h + 1))
            gate = row[:, 2048 + 128 * h:2048 + 128 * (h + 1)]
            mix_ref[:, hs] = _group_norm_gate(y, gate, gn_ref[:, hs])

    cb, sb = cb_ref[...], sb_ref[...]
    ps = ps_ref[...]
    kn = _rope64(jnp.broadcast_to(ps[:, 0:128], (8, 128)), cb, sb)[0:1]
    vn = ps[:, 128:256]
    last = r_i == WINDOW - 1
    k_new = jnp.where(last, kn, pltpu.roll(kbuf_ref[...], WINDOW - 1, axis=0))
    v_new = jnp.where(last, vn, pltpu.roll(vbuf_ref[...], WINDOW - 1, axis=0))
    ko_ref[...] = k_new
    vo_ref[...] = v_new
    kpad = _swa_padded(k_new)
    vpad = _swa_padded(v_new)

    def q_pair(jj):
        q8 = jnp.broadcast_to(row[:, 3072 + 128 * jj:3072 + 128 * (jj + 1)], (8, 128))
        return _rope64(q8, cb, sb) * (HD_B ** -0.5)

    def store(jj, acc):
        mix_ref[:, 1024 + 128 * jj:1024 + 128 * (jj + 1)] = acc[0:1]

    _swa_attend(q_pair, kpad, vpad, None, sink_ref, store, 8)


def _even_sample(p_main, p_small, s0, kbuf, vbuf, ca, sa, cb, sb, gn_w, sinks):
    nb = p_main.shape[0]
    b3 = lambda b: (b, 0, 0)
    b4 = lambda b: (b, 0, 0, 0)
    c2 = lambda b: (0, 0)
    ns = SAMPLE_NSEQ
    return pl.pallas_call(
        _even_sample_kernel,
        grid=(nb // ns,),
        in_specs=[
            pl.BlockSpec(memory_space=pltpu.SMEM),
            pl.BlockSpec((ns, 1, EVEN_MAIN), b3),
            pl.BlockSpec((ns, 1, EVEN_SMALL), b3),
            pl.BlockSpec((ns, 4, 128, 128), b4),
            pl.BlockSpec((ns, WINDOW, 128), b3),
            pl.BlockSpec((ns, WINDOW, 128), b3),
            pl.BlockSpec((1, 512), c2), pl.BlockSpec((1, 512), c2),
            pl.BlockSpec((1, 128), c2), pl.BlockSpec((1, 128), c2),
            pl.BlockSpec((1, 1024), c2),
        ],
        out_specs=[
            pl.BlockSpec((ns, 1, 2048), b3),
            pl.BlockSpec((ns, 4, 128, 128), b4),
            pl.BlockSpec((ns, WINDOW, 128), b3),
            pl.BlockSpec((ns, WINDOW, 128), b3),
        ],
        out_shape=[jax.ShapeDtypeStruct((nb, 1, 2048), F32), jax.ShapeDtypeStruct((nb, 4, 128, 128), F32),
                   jax.ShapeDtypeStruct((nb, WINDOW, 128), F32), jax.ShapeDtypeStruct((nb, WINDOW, 128), F32)],
        compiler_params=_cparams(("arbitrary",)),
        name="even_sample",
    )(sinks, p_main.reshape(nb, 1, EVEN_MAIN), p_small.reshape(nb, 1, EVEN_SMALL), s0, kbuf, vbuf,
      ca, sa, cb, sb, gn_w)


def _fox_q_tile(qi, q_ref, fcol_ref, frow_ref, kb_scr, vb_scr, o_ref, tq):
    r_i = lax.broadcasted_iota(jnp.int32, (tq, tq), 0)
    c_i = lax.broadcasted_iota(jnp.int32, (tq, tq), 1)
    q = (q_ref[...] * (HD_C ** -0.5)).astype(BF16)
    fq = fcol_ref[...]
    diag = slice(qi * tq, (qi + 1) * tq)
    s_d = jnp.where(c_i <= r_i, _dot_nt(q, kb_scr[diag, :]) + (fq - frow_ref[:, diag]), NEG)
    m = jnp.max(s_d, axis=-1, keepdims=True)
    if qi > 0:
        past = slice(0, qi * tq)
        s_p = _dot_nt(q, kb_scr[past, :]) + (fq - frow_ref[:, past])
        m = jnp.maximum(m, jnp.max(s_p, axis=-1, keepdims=True))
    p_d = jnp.exp(s_d - m)
    l = jnp.sum(p_d, axis=-1, keepdims=True)
    acc = _dot(p_d.astype(BF16), vb_scr[diag, :])
    if qi > 0:
        p_p = jnp.exp(s_p - m)
        l = l + jnp.sum(p_p, axis=-1, keepdims=True)
        acc = acc + _dot(p_p.astype(BF16), vb_scr[past, :])
    o_ref[...] = (acc / l).astype(BF16)


def _fox_kernel(*refs, pp, tq):
    pt_ref = refs[0]
    q_ref, kn_ref, vn_ref, lfn_ref = refs[1:5]
    k_refs = refs[5:5 + pp]
    v_refs = refs[5 + pp:5 + 2 * pp]
    lf_pool = refs[5 + 2 * pp]
    pq_ref, pk_ref, pv_ref, fcol_ref, frow_ref = refs[6 + 2 * pp:11 + 2 * pp]
    o_ref, po_ref = refs[11 + 2 * pp:13 + 2 * pp]
    m_scr, l_scr, acc_scr, carry_scr, rexp_scr, tri_scr, kb_scr, vb_scr = refs[13 + 2 * pp:]
    b = pl.program_id(0)
    s_id = pl.program_id(1)
    rows = PAGE_SIZE * H_C

    @pl.when(s_id == 0)
    def _():
        kb_scr[...] = pk_ref[...].astype(BF16)
        vb_scr[...] = pv_ref[...].astype(BF16)

    for qi in range(kb_scr.shape[0] // tq):
        @pl.when(s_id == qi)
        def _(qi=qi):
            _fox_q_tile(qi, pq_ref, fcol_ref, frow_ref, kb_scr, vb_scr, po_ref, tq)

    @pl.when((b == 0) & (s_id == 0))
    def _():
        r = lax.broadcasted_iota(jnp.int32, (PAGE_SIZE, rows), 0)
        c = lax.broadcasted_iota(jnp.int32, (PAGE_SIZE, rows), 1)
        rexp_scr[...] = jnp.where((c >> 3) == r, 1.0, 0.0).astype(BF16)
        r2 = lax.broadcasted_iota(jnp.int32, (PAGE_SIZE, PAGE_SIZE), 0)
        c2 = lax.broadcasted_iota(jnp.int32, (PAGE_SIZE, PAGE_SIZE), 1)
        tri_scr[...] = jnp.where(r2 <= c2, 1.0, 0.0).astype(BF16)

    @pl.when(s_id == 0)
    def _():
        m_scr[...] = jnp.full_like(m_scr, NEG)
        l_scr[...] = jnp.zeros_like(l_scr)
        acc_scr[...] = jnp.zeros_like(acc_scr)
        carry_scr[...] = jnp.zeros_like(carry_scr)

    qs = q_ref[...] * (HD_C ** -0.5)
    qb = qs.astype(BF16)
    h_i = lax.broadcasted_iota(jnp.int32, (H_C, rows), 0)
    c_i = lax.broadcasted_iota(jnp.int32, (H_C, rows), 1)
    diag = (c_i & (H_C - 1)) == h_i
    m, l, acc, carry = m_scr[...], l_scr[...], acc_scr[...], carry_scr[...]

    lf_all = jnp.concatenate([lf_pool[pt_ref[b, s_id * pp + r]] for r in range(pp)], axis=0)
    cum_in = _exact_right01(lf_all, tri_scr[...])
    offs = []
    for r in range(pp):
        offs.append(carry)
        carry = carry + cum_in[H_C * r:H_C * (r + 1), PAGE_SIZE - 1:PAGE_SIZE]
    gexp = _exact_right01(cum_in + jnp.concatenate(offs, axis=0), rexp_scr[...])

    logits = []
    for r in range(pp):
        k2t = jnp.transpose(k_refs[r][...].reshape(rows, HD_C)).astype(BF16)
        logits.append(jnp.where(diag, _dot(qb, k2t) - gexp[H_C * r:H_C * (r + 1)], NEG))
    mx = logits[0]
    for r in range(1, pp):
        mx = jnp.maximum(mx, logits[r])
    m_new = jnp.maximum(m, jnp.max(mx, axis=-1, keepdims=True))
    a = jnp.exp(m - m_new)
    psum = pv = None
    for r in range(pp):
        p = jnp.exp(logits[r] - m_new)
        o = _dot(p.astype(BF16), v_refs[r][...].reshape(rows, HD_C).astype(BF16))
        psum = p if psum is None else psum + p
        pv = o if pv is None else pv + o
    l = a * l + jnp.sum(psum, axis=-1, keepdims=True)
    acc = a * acc + pv
    m = m_new
    m_scr[...] = m
    l_scr[...] = l
    acc_scr[...] = acc
    carry_scr[...] = carry

    @pl.when(s_id == pl.num_programs(1) - 1)
    def _():
        fq = carry + lfn_ref[...]
        s_new = jnp.sum(qs * kn_ref[...], axis=-1, keepdims=True)
        m_past = m + fq
        mx = jnp.maximum(m_past, s_new)
        wp = jnp.exp(m_past - mx)
        wn = jnp.exp(s_new - mx)
        o_ref[...] = (acc * wp + wn * vn_ref[...]) / (l * wp + wn)


def _fox(page_table, q, k_new, v_new, lf_new, cache_k, cache_v, cache_lf, p_main, fcol, frow, *, pp, nb_p, seq):
    nb, n_pages = page_table.shape
    ns = n_pages // pp
    tq = seq // ns
    assert nb == nb_p * H_C and ns * tq == seq and ns * pp == n_pages
    b3 = lambda b, s, pt: (b, 0, 0)
    kv_spec = lambda r: pl.BlockSpec((None, PAGE_SIZE, H_C, HD_C), lambda b, s, pt: (pt[b, s * pp + r], 0, 0, 0))
    in_specs = [pl.BlockSpec((None, H_C, HD_C), b3), pl.BlockSpec((None, H_C, HD_C), b3),
                pl.BlockSpec((None, H_C, HD_C), b3), pl.BlockSpec((None, H_C, 1), b3)]
    in_specs += [kv_spec(r) for r in range(pp)] + [kv_spec(r) for r in range(pp)]
    in_specs += [
        pl.BlockSpec(cache_lf.shape, lambda b, s, pt: (0, 0, 0), pipeline_mode=pl.Buffered(1)),
        pl.BlockSpec((tq, HD_C), lambda b, s, pt: ((b // H_C) * ns + s, b % H_C)),
        pl.BlockSpec((seq, HD_C), lambda b, s, pt: (b // H_C, H_C + b % H_C), pipeline_mode=pl.Buffered(1)),
        pl.BlockSpec((seq, HD_C), lambda b, s, pt: (b // H_C, 2 * H_C + b % H_C), pipeline_mode=pl.Buffered(1)),
        pl.BlockSpec((None, None, tq, 1), lambda b, s, pt: (b // H_C, b % H_C, s, 0)),
        pl.BlockSpec((None, None, 1, seq), lambda b, s, pt: (b // H_C, b % H_C, 0, 0)),
    ]
    grid_spec = pltpu.PrefetchScalarGridSpec(
        num_scalar_prefetch=1, grid=(nb, ns), in_specs=in_specs,
        out_specs=[pl.BlockSpec((None, H_C, HD_C), b3),
                   pl.BlockSpec((tq, HD_C), lambda b, s, pt: ((b // H_C) * ns + s, b % H_C))],
        scratch_shapes=[pltpu.VMEM((H_C, 1), F32), pltpu.VMEM((H_C, 1), F32), pltpu.VMEM((H_C, HD_C), F32),
                        pltpu.VMEM((H_C, 1), F32), pltpu.VMEM((PAGE_SIZE, PAGE_SIZE * H_C), BF16),
                        pltpu.VMEM((PAGE_SIZE, PAGE_SIZE), BF16),
                        pltpu.VMEM((seq, HD_C), BF16), pltpu.VMEM((seq, HD_C), BF16)],
    )
    return pl.pallas_call(
        functools.partial(_fox_kernel, pp=pp, tq=tq),
        grid_spec=grid_spec,
        out_shape=[jax.ShapeDtypeStruct((nb, H_C, HD_C), F32),
                   jax.ShapeDtypeStruct((nb_p * seq, H_C * HD_C), BF16)],
        compiler_params=_cparams(("arbitrary", "arbitrary"), MLP_VMEM_LIMIT),
        name="fox",
    )(page_table, q, k_new, v_new, lf_new, *([cache_k] * pp), *([cache_v] * pp), cache_lf,
      p_main, p_main, p_main, fcol, frow)


def _lane_col(x, lane):
    return x[:, lane:lane + 1]


def _ssd_prompt_kernel(x_all, bc_all, z_all, dt_all, cwx_ref, cwbc_ref, cbx_ref, cbbc_ref, dtb_ref, alog_ref,
                       dskip_ref, nw_ref, od_all, st_all, cv_all, xpx_all, xpbc_all, s_all, y_all):
    c = pl.program_id(1)
    nc = pl.num_programs(1)

    @pl.when(c == 0)
    def _():
        xpx_all[:, 0:8, :] = jnp.zeros((MIX_NBAT, 8, DI_D), F32)
        xpbc_all[:, 0:8, :] = jnp.zeros((MIX_NBAT, 8, 512), F32)
        s_all[...] = jnp.zeros_like(s_all)

    for u in range(MIX_NBAT):
        _ssd_chunk(x_all.at[u], bc_all.at[u], z_all.at[u], dt_all.at[u], cwx_ref, cwbc_ref, cbx_ref, cbbc_ref,
                   dtb_ref, alog_ref, dskip_ref, nw_ref, od_all.at[u], cv_all.at[u],
                   xpx_all.at[u], xpbc_all.at[u], s_all.at[u], y_all.at[u])

    @pl.when(c == nc - 1)
    def _():
        st_all[...] = s_all[...]


def _ssd_chunk(x_ref, bc_ref, z_ref, dt_ref, cwx_ref, cwbc_ref, cbx_ref, cbbc_ref, dtb_ref, alog_ref,
               dskip_ref, nw_ref, od_ref, cv_ref, xpx_scr, xpbc_scr, s_scr, y_scr):
    xpx_scr[8:8 + CHUNK, :] = x_ref[...]
    xpbc_scr[8:8 + CHUNK, :] = bc_ref[...]
    cx = cbx_ref[...]
    cbc = cbbc_ref[...]
    for k in range(CONV_W):
        w = CONV_W - 1 - k
        cx = cx + cwx_ref[w:w + 1, :] * xpx_scr[8 - k:8 - k + CHUNK, :]
        cbc = cbc + cwbc_ref[w:w + 1, :] * xpbc_scr[8 - k:8 - k + CHUNK, :]

    cv_ref[:, 0:DI_D] = xpx_scr[CHUNK + 5:CHUNK + 8, :]
    cv_ref[:, DI_D:CONV_CH] = xpbc_scr[CHUNK + 5:CHUNK + 8, :]
    xpx_scr[0:8, :] = xpx_scr[CHUNK:CHUNK + 8, :]
    xpbc_scr[0:8, :] = xpbc_scr[CHUNK:CHUNK + 8, :]

    xs = _silu(cx)
    bcs = _silu(cbc)
    dt = _softplus(dt_ref[...] + dtb_ref[...])
    la = dt * (-jnp.exp(alog_ref[...]))
    cum = _exact_left01(_tri_lower(CHUNK), la)
    cum_t = jnp.transpose(cum)
    t_i = lax.broadcasted_iota(jnp.int32, (CHUNK, CHUNK), 0)
    s_i = lax.broadcasted_iota(jnp.int32, (CHUNK, CHUNK), 1)
    causal = t_i >= s_i
    lo = s_i < 64
    att_base, bt = [], []
    for g in range(G_D):
        bg = bcs[:, 128 * g:128 * (g + 1)]
        cg = bcs[:, 256 + 128 * g:256 + 128 * (g + 1)]
        att_base.append(_dot_nt(cg.astype(BF16), bg.astype(BF16)))
        bt.append(jnp.transpose(bg).astype(BF16))
    ss = jnp.zeros((CHUNK, 1), F32)
    for p in range(H_D // 2):
        g = (2 * p) // (H_D // G_D)
        sl = slice(128 * p, 128 * (p + 1))
        le, lo_ = DT_LANE + 2 * p, DT_LANE + 2 * p + 1
        dt_pair = jnp.where(lo, _lane_col(dt, le), _lane_col(dt, lo_))
        cum_pair = jnp.where(lo, _lane_col(cum, le), _lane_col(cum, lo_))
        clast = cum_pair[CHUNK - 1:CHUNK, :]
        xs_p = xs[:, sl]
        xdt = xs_p * dt_pair
        s_old = s_scr[p]
        cgb = bcs[:, 256 + 128 * g:256 + 128 * (g + 1)].astype(BF16)
        y = _dot(cgb, s_old.astype(BF16)) * jnp.exp(cum_pair)
        for e in range(2):
            ln = DT_LANE + 2 * p + e
            diff = _lane_col(cum, ln) - cum_t[ln:ln + 1, :]
            att = (att_base[g] * jnp.exp(jnp.where(causal, diff, NEG))).astype(BF16)
            xm = jnp.where(lo if e == 0 else ~lo, xdt, 0.0).astype(BF16)
            y = y + _dot(att, xm)
        y = (y + xs_p * dskip_ref[:, sl]) * _silu(z_ref[:, sl])
        y_scr[:, sl] = y
        ss = ss + jnp.sum(y * y, axis=-1, keepdims=True)
        txdt = (xdt * jnp.exp(clast - cum_pair)).astype(BF16)
        s_scr[p] = jnp.exp(clast) * s_old + _dot(bt[g], txdt)
    inv = lax.rsqrt(ss * (1.0 / DI_D) + EPS)
    od_ref[...] = (y_scr[...] * inv * nw_ref[...]).astype(BF16)


def _ssd_prompt(p_main, p_small, cwx, cwbc, cbx, cbbc, dtb, alog, dskip, nw, *, nb, nc):
    seq = nc * CHUNK
    ng = nb // MIX_NBAT
    pg, psg = _batch_groups(p_main, nb, seq), _batch_groups(p_small, nb, seq)
    blk = lambda w, col: pl.BlockSpec((None, MIX_NBAT, CHUNK, w), lambda b, c: (b, 0, c, col))
    c2 = lambda b, c: (0, 0)
    od, st, cv = pl.pallas_call(
        _ssd_prompt_kernel,
        grid=(ng, nc),
        in_specs=[
            blk(1024, 4), blk(512, 10), blk(1024, 3), blk(ODD_SMALL, 0),
            pl.BlockSpec((CONV_W, DI_D), c2), pl.BlockSpec((CONV_W, 512), c2),
            pl.BlockSpec((1, DI_D), c2), pl.BlockSpec((1, 512), c2),
            pl.BlockSpec((1, ODD_SMALL), c2), pl.BlockSpec((1, ODD_SMALL), c2),
            pl.BlockSpec((1, DI_D), c2), pl.BlockSpec((1, DI_D), c2),
        ],
        out_specs=[
            blk(DI_D, 0),
            pl.BlockSpec((None, MIX_NBAT, 8, 128, 128), lambda b, c: (b, 0, 0, 0, 0)),
            pl.BlockSpec((None, MIX_NBAT, CONV_W - 1, CONV_CH), lambda b, c: (b, 0, 0, 0)),
        ],
        out_shape=[jax.ShapeDtypeStruct((ng, MIX_NBAT, seq, DI_D), BF16),
                   jax.ShapeDtypeStruct((ng, MIX_NBAT, 8, 128, 128), F32),
                   jax.ShapeDtypeStruct((ng, MIX_NBAT, CONV_W - 1, CONV_CH), F32)],
        scratch_shapes=[pltpu.VMEM((MIX_NBAT, CHUNK + 8, DI_D), F32), pltpu.VMEM((MIX_NBAT, CHUNK + 8, 512), F32),
                        pltpu.VMEM((MIX_NBAT, 8, 128, 128), F32), pltpu.VMEM((MIX_NBAT, CHUNK, DI_D), F32)],
        compiler_params=_cparams(("arbitrary", "arbitrary")),
        name="ssd_prompt",
    )(pg, pg, pg, psg, cwx, cwbc, cbx, cbbc, dtb, alog, dskip, nw)
    return od.reshape(nb * seq, DI_D), st.reshape(nb, 8, 128, 128), cv.reshape(nb, CONV_W - 1, CONV_CH)


def _ssd_sample_kernel(p_all, dt_all, cs_all, s0_all, cwx_ref, cwbc_ref, cbx_ref, cbbc_ref, dtb_ref, alog_ref,
                       dskip_ref, nw_ref, od_all, st_all, cv_all):
    for u in range(SAMPLE_NSEQ):
        _ssd_sample_one(p_all.at[u], dt_all.at[u], cs_all.at[u], s0_all.at[u], cwx_ref, cwbc_ref, cbx_ref,
                        cbbc_ref, dtb_ref, alog_ref, dskip_ref, nw_ref, od_all.at[u], st_all.at[u], cv_all.at[u])


def _ssd_sample_one(p_ref, dt_ref, cs_ref, s0_ref, cwx_ref, cwbc_ref, cbx_ref, cbbc_ref, dtb_ref, alog_ref,
                    dskip_ref, nw_ref, od_ref, st_ref, cv_ref):
    row = p_ref[...]
    xn = row[:, 4096:5120]
    bcn = row[:, 5120:5632]
    cx = cbx_ref[...] + cwx_ref[3:4, :] * xn
    cbc = cbbc_ref[...] + cwbc_ref[3:4, :] * bcn
    for w in range(CONV_W - 1):
        cx = cx + cwx_ref[w:w + 1, :] * cs_ref[w:w + 1, 0:DI_D]
        cbc = cbc + cwbc_ref[w:w + 1, :] * cs_ref[w:w + 1, DI_D:CONV_CH]
    cv_ref[0:1, :] = cs_ref[1:2, :]
    cv_ref[1:2, :] = cs_ref[2:3, :]
    cv_ref[2:3, 0:DI_D] = xn
    cv_ref[2:3, DI_D:CONV_CH] = bcn
    xs = _silu(cx)
    bcs = _silu(cbc)
    dt = _softplus(dt_ref[...] + dtb_ref[...])
    da = jnp.exp(dt * (-jnp.exp(alog_ref[...])))
    lane = lax.broadcasted_iota(jnp.int32, (1, 128), 1)
    lo = lane < 64
    bcol = [_col_bcast(bcs[:, 128 * g:128 * (g + 1)]) for g in range(G_D)]
    ccol = [_col_bcast(bcs[:, 256 + 128 * g:256 + 128 * (g + 1)]) for g in range(G_D)]
    ys = []
    ss = jnp.zeros((1, 1), F32)
    for p in range(H_D // 2):
        g = (2 * p) // (H_D // G_D)
        sl = slice(128 * p, 128 * (p + 1))
        le, lo_ = DT_LANE + 2 * p, DT_LANE + 2 * p + 1
        dt_pair = jnp.where(lo, _lane_col(dt, le), _lane_col(dt, lo_))
        da_pair = jnp.where(lo, _lane_col(da, le), _lane_col(da, lo_))
        xs_p = xs[:, sl]
        s_new = da_pair * s0_ref[p] + bcol[g] * (xs_p * dt_pair)
        st_ref[p] = s_new
        y = jnp.sum(ccol[g] * s_new, axis=0, keepdims=True)
        y = (y + xs_p * dskip_ref[:, sl]) * _silu(row[:, 3072 + 128 * p:3072 + 128 * (p + 1)])
        ys.append(y)
        ss = ss + jnp.sum(y * y, axis=-1, keepdims=True)
    inv = lax.rsqrt(ss * (1.0 / DI_D) + EPS)
    for p in range(H_D // 2):
        sl = slice(128 * p, 128 * (p + 1))
        od_ref[:, sl] = ys[p] * inv * nw_ref[:, sl]


def _ssd_sample(p_main, p_small, cs, s0, cwx, cwbc, cbx, cbbc, dtb, alog, dskip, nw):
    nb = p_main.shape[0]
    b3 = lambda b: (b, 0, 0)
    b4 = lambda b: (b, 0, 0, 0)
    c2 = lambda b: (0, 0)
    ns = SAMPLE_NSEQ
    return pl.pallas_call(
        _ssd_sample_kernel,
        grid=(nb // ns,),
        in_specs=[
            pl.BlockSpec((ns, 1, ODD_MAIN), b3),
            pl.BlockSpec((ns, 1, ODD_SMALL), b3),
            pl.BlockSpec((ns, CONV_W - 1, CONV_CH), b3),
            pl.BlockSpec((ns, 8, 128, 128), b4),
            pl.BlockSpec((CONV_W, DI_D), c2), pl.BlockSpec((CONV_W, 512), c2),
            pl.BlockSpec((1, DI_D), c2), pl.BlockSpec((1, 512), c2),
            pl.BlockSpec((1, ODD_SMALL), c2), pl.BlockSpec((1, ODD_SMALL), c2),
            pl.BlockSpec((1, DI_D), c2), pl.BlockSpec((1, DI_D), c2),
        ],
        out_specs=[
            pl.BlockSpec((ns, 1, DI_D), b3),
            pl.BlockSpec((ns, 8, 128, 128), b4),
            pl.BlockSpec((ns, CONV_W - 1, CONV_CH), b3),
        ],
        out_shape=[jax.ShapeDtypeStruct((nb, 1, DI_D), F32), jax.ShapeDtypeStruct((nb, 8, 128, 128), F32),
                   jax.ShapeDtypeStruct((nb, CONV_W - 1, CONV_CH), F32)],
        compiler_params=_cparams(("arbitrary",)),
        name="ssd_sample",
    )(p_main.reshape(nb, 1, ODD_MAIN), p_small.reshape(nb, 1, ODD_SMALL), cs, s0,
      cwx, cwbc, cbx, cbbc, dtb, alog, dskip, nw)


def _pairs_to_heads(s):
    b, p, n, _ = s.shape
    return s.reshape(b, p, n, 2, 64).transpose(0, 1, 3, 2, 4).reshape(b, 2 * p, n, 64)


def _heads_to_pairs(s):
    b, h, n, d = s.shape
    return s.reshape(b, h // 2, 2, n, d).transpose(0, 1, 3, 2, 4).reshape(b, h // 2, n, 2 * d)


def _pad_lanes(v, start, width=ODD_SMALL):
    out = jnp.zeros((1, width), F32)
    return lax.dynamic_update_slice(out, v.reshape(1, -1).astype(F32), (0, start))


def kernel(x_prompt, x_sample, state_ret, cache_swa_k, cache_swa_v, cache_fox_k, cache_fox_v, cache_fox_logf,
           state_ssm, state_conv, page_table, norm_mix_pre, norm_mix_post, norm_mlp_pre, norm_mlp_post,
           w_in_even, w_out_even, ret_norm_w, swa_sinks, w_in_odd, w_out_odd, fox_fb, conv_w, conv_b,
           dt_bias, a_log, d_skip, ssd_norm_w, w_up, w_down):
    nb, seq = BATCH, SEQ
    nc = seq // CHUNK
    mp = nb * seq
    ms = DEC_BATCH
    xp = x_prompt.reshape(mp, D_MODEL)
    xs = x_sample.reshape(ms, D_MODEL)
    row = lambda v: v.reshape(1, -1)

    pos_p = jnp.arange(seq, dtype=jnp.int32)
    pos_s = jnp.full((1,), PAST_LEN, dtype=jnp.int32)
    ca_p, sa_p = _rope_tables(pos_p, RET_THETA, 8)
    cb_p, sb_p = _rope_tables(pos_p, ROPE_THETA_B, 2)
    ca_s, sa_s = _rope_tables(pos_s, RET_THETA, 8)
    cb_s, sb_s = _rope_tables(pos_s, ROPE_THETA_B, 2)

    we = w_in_even[0]
    we_main = we[:, :EVEN_MAIN].astype(BF16)
    we_small = we[:, EVEN_MAIN:]
    wo = w_out_even[0].astype(BF16)
    wo_a, wo_b = wo[:1024], wo[1024:]
    g_pre, g_post = row(norm_mix_pre[0]), row(norm_mix_post[0])
    gm_pre, gm_post = row(norm_mlp_pre[0]), row(norm_mlp_post[0])
    gn_w = row(ret_norm_w[0])
    sinks = swa_sinks[0]

    casts0 = [(w_up, 0, D_MODEL, D_FF), (w_down, 0, D_FF, D_MODEL),
              (w_in_odd[0][:, :3072], None, D_MODEL, 3072), (w_in_odd[0][:, 3080:5640], None, D_MODEL, 2560)]
    pm, psm, sm, ssm_, wu_b, wd_b, wod_qkv, wod_zx = _proj(
        xp, xs, g_pre, (we_main,), we_small, casts0, odd=False, tm=256, tn=512, seq=seq)
    out_a, ret_p = _ret_prompt(pm, ca_p, sa_p, gn_w, nb=nb, nc=nc)
    out_b, swak_p, swav_p = _swa_prompt(pm, psm, cb_p, sb_p, sinks, nb=nb, nc=nc)
    mix_s, ret_s, swak_s, swav_s = _even_sample(
        sm, ssm_, state_ret[0].reshape(ms, 4, 128, 128), cache_swa_k[0].reshape(ms, WINDOW, 128),
        cache_swa_v[0].reshape(ms, WINDOW, 128), ca_s, sa_s, cb_s, sb_s, gn_w, sinks)
    mix_s = mix_s.reshape(ms, 2048).astype(BF16)
    xp, xs = _outproj(out_a, out_b, xp, mix_s[:, :1024], mix_s[:, 1024:], xs, wo_a, wo_b, g_post, tm=512)
    xp, xs = _mlp(xp, xs, gm_pre, gm_post, wu_b, wd_b, tm=512, tf=1024)

    wod = w_in_odd[0]
    wod_small = jnp.concatenate(
        [wod[:, 3072:3080], wod[:, 5640:5656], jnp.zeros((D_MODEL, ODD_SMALL - 24), F32)], axis=1)
    wo1 = w_out_odd[0].astype(BF16)
    wo_c, wo_d = wo1[:1024], wo1[1024:]
    g_pre, g_post = row(norm_mix_pre[1]), row(norm_mix_post[1])
    gm_pre, gm_post = row(norm_mlp_pre[1]), row(norm_mlp_post[1])
    fb = _pad_lanes(fox_fb[0], 0)
    dtb = _pad_lanes(dt_bias[0], DT_LANE)
    alog = _pad_lanes(a_log[0], DT_LANE)
    cw = conv_w[0]
    cwx, cwbc = cw[:, :DI_D], cw[:, DI_D:]
    cbx, cbbc = row(conv_b[0][:DI_D]), row(conv_b[0][DI_D:])
    dskip = row(jnp.repeat(d_skip[0], HD_D))
    nw = row(ssd_norm_w[0])

    pm, psm, sm, ssm_, wu_b, wd_b, lf_p, fc_p, lf_s, k3_p, v3_p = _proj(
        xp, xs, g_pre, (wod_qkv, wod_zx), wod_small, [(w_up, 1, D_MODEL, D_FF), (w_down, 1, D_FF, D_MODEL)],
        fb, odd=True, tm=256, tn=512, seq=seq)
    fc = fc_p[:, :H_C].reshape(nb, seq, H_C).transpose(0, 2, 1)
    out_d, ssm_pairs_p, conv_p = _ssd_prompt(pm, psm, cwx, cwbc, cbx, cbbc, dtb, alog, dskip, nw, nb=nb, nc=nc)
    fox_k_p = k3_p.reshape(1, nb, seq, H_C, HD_C)
    fox_v_p = v3_p.reshape(1, nb, seq, H_C, HD_C)
    fox_lf_p = lf_p[:, :H_C].reshape(1, nb, seq, H_C)

    q_s = sm[:, 0:1024].reshape(ms, H_C, HD_C)
    k_s = sm[:, 1024:2048].reshape(ms, H_C, HD_C)
    v_s = sm[:, 2048:3072].reshape(ms, H_C, HD_C)
    lf_s8 = lf_s[:, :H_C]
    out_c_s, out_c = _fox(page_table, q_s, k_s, v_s, lf_s8.reshape(ms, H_C, 1),
                          cache_fox_k[0], cache_fox_v[0], cache_fox_logf[0].transpose(0, 2, 1),
                          pm, fc[..., None], fc[:, :, None, :], pp=16, nb_p=nb, seq=seq)
    out_d_s, ssm_pairs_s, conv_s = _ssd_sample(sm, ssm_, state_conv[0], _heads_to_pairs(state_ssm[0]),
                                               cwx, cwbc, cbx, cbbc, dtb, alog, dskip, nw)
    xp, xs = _outproj(out_c, out_d, xp, out_c_s.reshape(ms, 1024).astype(BF16),
                      out_d_s.reshape(ms, DI_D).astype(BF16), xs, wo_c, wo_d, g_post, tm=512)
    xp, xs = _mlp(xp, xs, gm_pre, gm_post, wu_b, wd_b, tm=512, tf=1024)

    return (
        xp.reshape(nb, seq, D_MODEL), xs.reshape(ms, 1, D_MODEL),
        ret_p.reshape(1, nb, H_A, DK_A, DV_A), ret_s.reshape(1, ms, H_A, DK_A, DV_A),
        swak_p.reshape(1, nb, WINDOW, KV_B, HD_B), swav_p.reshape(1, nb, WINDOW, KV_B, HD_B),
        swak_s.reshape(1, ms, WINDOW, KV_B, HD_B), swav_s.reshape(1, ms, WINDOW, KV_B, HD_B),
        fox_k_p, fox_v_p, fox_lf_p,
        k_s.reshape(1, ms, 1, H_C, HD_C), v_s.reshape(1, ms, 1, H_C, HD_C), lf_s8.reshape(1, ms, 1, H_C),
        _pairs_to_heads(ssm_pairs_p)[None], _pairs_to_heads(ssm_pairs_s)[None],
        conv_p[None], conv_s[None],
    )
```

```python
import functools

import numpy as np
import jax
import jax.numpy as jnp
from jax import lax
from jax.experimental import pallas as pl
from jax.experimental.pallas import tpu as pltpu

F32 = jnp.float32
BF16 = jnp.bfloat16

D_MODEL = 2048
BATCH = 4
SEQ = 2048
DEC_BATCH = 32
PAST_LEN = 16384
PAGE_SIZE = 128
D_FF = 4 * D_MODEL
EPS = 1e-6
GN_EPS = 1e-5
CHUNK = 128

H_A, DK_A, DV_A = 8, 64, 128
RET_THETA = 10000.0
H_B, KV_B, HD_B = 16, 2, 64
WINDOW = 128
ROPE_THETA_B = 150000.0
H_C, HD_C = 8, 128
H_D, HD_D, G_D, N_D = 16, 64, 2, 128
CONV_W = 4
DI_D = H_D * HD_D
CONV_CH = DI_D + 2 * G_D * N_D

EVEN_MAIN = 4096
EVEN_SMALL = 256
ODD_MAIN = 5632
ODD_SMALL = 128
DT_LANE = 8

NEG = -1e30
VMEM_LIMIT = 56 * 1024 * 1024
MLP_VMEM_LIMIT = 60 * 1024 * 1024
MLP_SUB_ROWS = 256
MIX_NBAT = 2
SAMPLE_NSEQ = 4

_LOG_GAMMA = [float(v) for v in np.log1p(-np.exp2(-5.0 - np.arange(H_A, dtype=np.float64)))]


def _cparams(sem, vmem_limit=VMEM_LIMIT):
    return pltpu.CompilerParams(dimension_semantics=sem, vmem_limit_bytes=vmem_limit)


def _silu(x):
    return x * jax.nn.sigmoid(x)


def _softplus(x):
    return jnp.maximum(x, 0.0) + jnp.log1p(jnp.exp(-jnp.abs(x)))


def _log_sigmoid(x):
    return jnp.minimum(x, 0.0) - jnp.log1p(jnp.exp(-jnp.abs(x)))


def _rms(x, g):
    ms = jnp.mean(x * x, axis=-1, keepdims=True)
    return (x * lax.rsqrt(ms + EPS)) * g


def _dot(a, b):
    return jnp.dot(a, b, preferred_element_type=F32)


def _dot_nt(a, b):
    return lax.dot_general(a, b, (((1,), (1,)), ((), ())), preferred_element_type=F32)


def _split3(x):
    hi = x.astype(BF16)
    r = x - hi.astype(F32)
    mid = r.astype(BF16)
    lo = (r - mid.astype(F32)).astype(BF16)
    return hi, mid, lo


def _exact_left01(m01, x):
    hi, mid, lo = _split3(x)
    return _dot(m01, hi) + _dot(m01, mid) + _dot(m01, lo)


def _exact_right01(x, m01):
    hi, mid, lo = _split3(x)
    return _dot(hi, m01) + _dot(mid, m01) + _dot(lo, m01)


def _tri_lower(n):
    r = lax.broadcasted_iota(jnp.int32, (n, n), 0)
    c = lax.broadcasted_iota(jnp.int32, (n, n), 1)
    return jnp.where(r >= c, 1.0, 0.0).astype(BF16)


def _rope64(x, c, s):
    w = x.shape[-1]
    ax = x.ndim - 1
    lane = lax.broadcasted_iota(jnp.int32, x.shape, ax)
    first = (lane & 32) == 0
    left = pltpu.roll(x, w - 32, axis=ax)
    right = pltpu.roll(x, 32, axis=ax)
    return x * c + jnp.where(first, left, right) * s


def _rope_tables(pos, theta, reps):
    inv = 1.0 / (theta ** (jnp.arange(32, dtype=F32) * (2.0 / 64)))
    ang = pos.astype(F32)[:, None] * inv[None, :]
    cos, sin = jnp.cos(ang), jnp.sin(ang)
    c = jnp.concatenate([cos, cos], axis=-1)
    s = jnp.concatenate([-sin, sin], axis=-1)
    return jnp.tile(c, (1, reps)), jnp.tile(s, (1, reps))


def _proj_kernel(*refs, odd, n_w, n_c, tm, tn, seq):
    x_ref, xs_ref, g_ref = refs[0:3]
    w_refs = refs[3:3 + n_w]
    wsm_ref = refs[3 + n_w]
    cast_src = refs[4 + n_w:4 + n_w + n_c]
    refs = refs[4 + n_w + n_c:]
    if odd:
        fb_ref, refs = refs[0], refs[1:]
    o_ref, osm_ref, os_ref, ossm_ref = refs[0:4]
    cast_dst = refs[4:4 + n_c]
    refs = refs[4 + n_c:]
    kv_refs = None
    if odd:
        lf_ref, fc_ref, lfs_ref, k3_ref, v3_ref, h_scr, carry_scr = refs
        kv_refs = (k3_ref, v3_ref)
    else:
        (h_scr,) = refs
    i = pl.program_id(0)
    for src, dst in zip(cast_src, cast_dst):
        dst[...] = src[...].astype(BF16)
    width = H_C * HD_C

    def rows(x_r, o_r, osm_r, n, kv=None):
        h_scr[0:n, :] = _rms(x_r[...], g_ref[...]).astype(BF16)
        sm = _dot(h_scr[0:n, :], wsm_ref[...])
        osm_r[...] = sm
        base = 0
        for w_ref in w_refs:
            for t in range(w_ref.shape[1] // tn):
                c0 = base + t * tn
                val = _dot(h_scr[0:n, :], w_ref[:, t * tn:(t + 1) * tn])
                o_r[:, c0:c0 + tn] = val
                if kv is not None and width <= c0 < 3 * width:
                    dst = kv[(c0 - width) // width]
                    h0 = ((c0 - width) % width) // HD_C
                    for hh in range(tn // HD_C):
                        dst[:, h0 + hh, :] = val[:, HD_C * hh:HD_C * (hh + 1)]
            base += w_ref.shape[1]
        return sm

    sm = rows(x_ref, o_ref, osm_ref, tm, kv_refs)
    if odd:
        lf = _log_sigmoid(sm + fb_ref[...])
        lf_ref[...] = lf

        @pl.when((i * tm) % seq == 0)
        def _():
            carry_scr[...] = jnp.zeros_like(carry_scr)

        f = _exact_left01(_tri_lower(tm), lf) + carry_scr[...]
        fc_ref[...] = f
        carry_scr[...] = f[tm - 1:tm, :]

    @pl.when(i == 0)
    def _():
        sms = rows(xs_ref, os_ref, ossm_ref, xs_ref.shape[0])
        if odd:
            lfs_ref[...] = _log_sigmoid(sms + fb_ref[...])


def _proj(x, xs, g, w_mains, w_small, casts, fb=None, *, odd, tm, tn, seq):
    m, ms = x.shape[0], xs.shape[0]
    n_main = sum(w.shape[1] for w in w_mains)
    n_small = w_small.shape[1]
    ni = m // tm
    c2 = lambda i: (0, 0)
    r2 = lambda i: (i, 0)
    in_specs = [
        pl.BlockSpec((tm, D_MODEL), r2),
        pl.BlockSpec((ms, D_MODEL), c2),
        pl.BlockSpec((1, D_MODEL), c2),
    ] + [pl.BlockSpec((D_MODEL, w.shape[1]), c2, pipeline_mode=pl.Buffered(1)) for w in w_mains] + [
        pl.BlockSpec((D_MODEL, n_small), c2),
    ]
    args = [x, xs, g, *w_mains, w_small]
    small = jax.ShapeDtypeStruct((m, n_small), F32)
    small_s = jax.ShapeDtypeStruct((ms, n_small), F32)
    small_spec = pl.BlockSpec((tm, n_small), r2)
    small_s_spec = pl.BlockSpec((ms, n_small), c2)
    out_shape = [jax.ShapeDtypeStruct((m, n_main), F32), small, jax.ShapeDtypeStruct((ms, n_main), F32), small_s]
    out_specs = [pl.BlockSpec((tm, n_main), r2), small_spec, pl.BlockSpec((ms, n_main), c2), small_s_spec]
    for src, lead, nrows, ncols in casts:
        rb = nrows // ni
        if lead is None:
            in_specs.append(pl.BlockSpec((rb, ncols), r2))
        else:
            in_specs.append(pl.BlockSpec((None, rb, ncols), lambda i, lead=lead: (lead, i, 0)))
        args.append(src)
        out_shape.append(jax.ShapeDtypeStruct((nrows, ncols), BF16))
        out_specs.append(pl.BlockSpec((rb, ncols), r2))
    scratch = [pltpu.VMEM((tm, D_MODEL), BF16)]
    if odd:
        in_specs.append(pl.BlockSpec((1, n_small), c2))
        args.append(fb)
        rows3 = jax.ShapeDtypeStruct((m, H_C, HD_C), F32)
        rows3_spec = pl.BlockSpec((tm, H_C, HD_C), lambda i: (i, 0, 0))
        out_shape += [small, small, small_s, rows3, rows3]
        out_specs += [small_spec, small_spec, small_s_spec, rows3_spec, rows3_spec]
        scratch.append(pltpu.VMEM((1, n_small), F32))
    return pl.pallas_call(
        functools.partial(_proj_kernel, odd=odd, n_w=len(w_mains), n_c=len(casts), tm=tm, tn=tn, seq=seq),
        grid=(ni,), in_specs=in_specs, out_specs=out_specs, out_shape=out_shape,
        scratch_shapes=scratch, compiler_params=_cparams(("arbitrary",), MLP_VMEM_LIMIT),
        name="proj_odd" if odd else "proj_even",
    )(*args)


def _outproj_kernel(a_ref, b_ref, x_ref, as_ref, bs_ref, xs_ref, wa_ref, wb_ref, g_ref, o_ref, os_ref):
    g = g_ref[...]
    n = x_ref.shape[0]
    sub = min(n, MLP_SUB_ROWS)
    for r in range(n // sub):
        rows = slice(r * sub, (r + 1) * sub)
        y = _dot(a_ref[rows, :], wa_ref[...]) + _dot(b_ref[rows, :], wb_ref[...])
        o_ref[rows, :] = x_ref[rows, :] + _rms(y, g)

    @pl.when(pl.program_id(0) == 0)
    def _():
        ys = _dot(as_ref[...], wa_ref[...]) + _dot(bs_ref[...], wb_ref[...])
        os_ref[...] = xs_ref[...] + _rms(ys, g)


def _outproj(a, b, x, a_s, b_s, xs, wa, wb, g, *, tm):
    m, ms = x.shape[0], xs.shape[0]
    ka, kb = a.shape[1], b.shape[1]
    c2 = lambda i: (0, 0)
    return pl.pallas_call(
        _outproj_kernel,
        grid=(m // tm,),
        in_specs=[
            pl.BlockSpec((tm, ka), lambda i: (i, 0)),
            pl.BlockSpec((tm, kb), lambda i: (i, 0)),
            pl.BlockSpec((tm, D_MODEL), lambda i: (i, 0)),
            pl.BlockSpec((ms, ka), c2),
            pl.BlockSpec((ms, kb), c2),
            pl.BlockSpec((ms, D_MODEL), c2),
            pl.BlockSpec((ka, D_MODEL), c2),
            pl.BlockSpec((kb, D_MODEL), c2),
            pl.BlockSpec((1, D_MODEL), c2),
        ],
        out_specs=[pl.BlockSpec((tm, D_MODEL), lambda i: (i, 0)), pl.BlockSpec((ms, D_MODEL), c2)],
        out_shape=[jax.ShapeDtypeStruct((m, D_MODEL), F32), jax.ShapeDtypeStruct((ms, D_MODEL), F32)],
        compiler_params=_cparams(("arbitrary",)),
        name="outproj",
    )(a, b, x, a_s, b_s, xs, wa, wb, g)


def _mlp_kernel(x_ref, xs_ref, gpre_ref, gpost_ref, wu_ref, wd_ref, o_ref, os_ref, h_scr, hs_scr):
    i = pl.program_id(0)
    j = pl.program_id(1)
    last = pl.num_programs(1) - 1

    def group(x_r, o_r, h_s):
        @pl.when(j == 0)
        def _():
            h_s[...] = _rms(x_r[...], gpre_ref[...]).astype(BF16)
            o_r[...] = jnp.zeros_like(o_r)

        n = x_r.shape[0]
        sub = min(n, MLP_SUB_ROWS)
        for r in range(n // sub):
            rows = slice(r * sub, (r + 1) * sub)
            u = jnp.maximum(_dot(h_s[rows, :], wu_ref[...]), 0.0)
            o_r[rows, :] += _dot((u * u).astype(BF16), wd_ref[...])

        @pl.when(j == last)
        def _():
            o_r[...] = x_r[...] + _rms(o_r[...], gpost_ref[...])

    group(x_ref, o_ref, h_scr)

    @pl.when(i == 0)
    def _():
        group(xs_ref, os_ref, hs_scr)


def _mlp(x, xs, gpre, gpost, w_up, w_down, *, tm, tf):
    m, ms = x.shape[0], xs.shape[0]
    c2 = lambda i, j: (0, 0)
    return pl.pallas_call(
        _mlp_kernel,
        grid=(m // tm, D_FF // tf),
        in_specs=[
            pl.BlockSpec((tm, D_MODEL), lambda i, j: (i, 0)),
            pl.BlockSpec((ms, D_MODEL), c2),
            pl.BlockSpec((1, D_MODEL), c2),
            pl.BlockSpec((1, D_MODEL), c2),
            pl.BlockSpec((D_MODEL, tf), lambda i, j: (0, j)),
            pl.BlockSpec((tf, D_MODEL), lambda i, j: (j, 0)),
        ],
        out_specs=[pl.BlockSpec((tm, D_MODEL), lambda i, j: (i, 0)), pl.BlockSpec((ms, D_MODEL), c2)],
        out_shape=[jax.ShapeDtypeStruct((m, D_MODEL), F32), jax.ShapeDtypeStruct((ms, D_MODEL), F32)],
        scratch_shapes=[pltpu.VMEM((tm, D_MODEL), BF16), pltpu.VMEM((ms, D_MODEL), BF16)],
        compiler_params=_cparams(("arbitrary", "arbitrary"), MLP_VMEM_LIMIT),
        name="mlp",
    )(x, xs, gpre, gpost, w_up, w_down)


def _group_norm_gate(y, gate, gw):
    mu = jnp.mean(y, axis=-1, keepdims=True)
    d = y - mu
    var = jnp.mean(d * d, axis=-1, keepdims=True)
    return _silu(gate) * (d * lax.rsqrt(var + GN_EPS) * gw)


def _ret_prompt_kernel(q_all, k_all, v_all, g_all, cos_ref, sin_ref, gn_ref, oa_all, st_all,
                       s_all, d_scr, e_scr, t_scr):
    b = pl.program_id(0)
    c = pl.program_id(1)
    t_i = lax.broadcasted_iota(jnp.int32, (CHUNK, CHUNK), 0)
    s_i = lax.broadcasted_iota(jnp.int32, (CHUNK, CHUNK), 1)
    lo = s_i < 64

    @pl.when((b == 0) & (c == 0))
    def _():
        tf = t_i.astype(F32)
        sf = s_i.astype(F32)
        for h in range(H_A):
            d_scr[h] = jnp.where(t_i >= s_i, jnp.exp((tf - sf) * _LOG_GAMMA[h]), 0.0)
            e_scr[h] = jnp.exp((tf + 1.0) * _LOG_GAMMA[h])
        for p in range(H_A // 2):
            lg = jnp.where(lo, _LOG_GAMMA[2 * p], _LOG_GAMMA[2 * p + 1])
            t_scr[p] = jnp.exp((CHUNK - 1.0 - tf) * lg)

    @pl.when(c == 0)
    def _():
        s_all[...] = jnp.zeros_like(s_all)

    cos, sin = cos_ref[...], sin_ref[...]
    top = t_i < 64
    for u in range(MIX_NBAT):
        v_ref, g_ref, oa_ref, s_scr = v_all.at[u], g_all.at[u], oa_all.at[u], s_all.at[u]
        qr = _rope64(q_all[u], cos, sin)
        kr = _rope64(k_all[u], cos, sin) * (DK_A ** -0.5)
        for p in range(H_A // 2):
            sl = slice(128 * p, 128 * (p + 1))
            qp, kp = qr[:, sl], kr[:, sl]
            kb = kp.astype(BF16)
            s_old = s_scr[p]
            s_old_b = s_old.astype(BF16)
            ktt = jnp.transpose(kp * t_scr[p]).astype(BF16)
            upd = []
            for e in range(2):
                h = 2 * p + e
                hs = slice(128 * h, 128 * (h + 1))
                qm = jnp.where(lo if e == 0 else ~lo, qp, 0.0).astype(BF16)
                vh = v_ref[:, hs].astype(BF16)
                att = (_dot_nt(qm, kb) * d_scr[h]).astype(BF16)
                y = _dot(att, vh) + _dot(qm, s_old_b) * e_scr[h]
                oa_ref[:, hs] = _group_norm_gate(y, g_ref[:, hs], gn_ref[:, hs]).astype(BF16)
                upd.append(_dot(ktt, vh))
            g128 = jnp.where(top, float(np.exp(CHUNK * _LOG_GAMMA[2 * p])),
                             float(np.exp(CHUNK * _LOG_GAMMA[2 * p + 1])))
            s_scr[p] = g128 * s_old + jnp.where(top, upd[0], upd[1])

    @pl.when(c == pl.num_programs(1) - 1)
    def _():
        st_all[...] = s_all[...]


def _batch_groups(a, nb, seq):
    return a.reshape(nb // MIX_NBAT, MIX_NBAT, seq, a.shape[-1])


def _ret_prompt(p_main, cos, sin, gn_w, *, nb, nc):
    seq = nc * CHUNK
    pg = _batch_groups(p_main, nb, seq)
    blk = lambda w, col: pl.BlockSpec((None, MIX_NBAT, CHUNK, w), lambda b, c: (b, 0, c, col))
    oa, st = pl.pallas_call(
        _ret_prompt_kernel,
        grid=(nb // MIX_NBAT, nc),
        in_specs=[
            blk(512, 0), blk(512, 1), blk(1024, 1), blk(1024, 2),
            pl.BlockSpec((CHUNK, 512), lambda b, c: (c, 0)),
            pl.BlockSpec((CHUNK, 512), lambda b, c: (c, 0)),
            pl.BlockSpec((1, 1024), lambda b, c: (0, 0)),
        ],
        out_specs=[
            blk(1024, 0),
            pl.BlockSpec((None, MIX_NBAT, 4, 128, 128), lambda b, c: (b, 0, 0, 0, 0)),
        ],
        out_shape=[jax.ShapeDtypeStruct((nb // MIX_NBAT, MIX_NBAT, seq, 1024), BF16),
                   jax.ShapeDtypeStruct((nb // MIX_NBAT, MIX_NBAT, 4, 128, 128), F32)],
        scratch_shapes=[pltpu.VMEM((MIX_NBAT, 4, 128, 128), F32), pltpu.VMEM((H_A, 128, 128), F32),
                        pltpu.VMEM((H_A, 128, 128), F32), pltpu.VMEM((4, 128, 128), F32)],
        compiler_params=_cparams(("arbitrary", "arbitrary")),
        name="ret_prompt",
    )(pg, pg, pg, pg, cos, sin, gn_w)
    return oa.reshape(nb * seq, 1024), st.reshape(nb, 4, 128, 128)


def _swa_padded(x2):
    lane = lax.broadcasted_iota(jnp.int32, x2.shape, 1)
    lo = lane < 64
    xr = pltpu.roll(x2, 64, axis=1)
    z = jnp.zeros_like(x2)
    return {
        (0, 0): jnp.where(lo, x2, z).astype(BF16), (0, 1): jnp.where(lo, z, xr).astype(BF16),
        (1, 0): jnp.where(lo, xr, z).astype(BF16), (1, 1): jnp.where(lo, z, x2).astype(BF16),
    }


def _swa_attend(q_pair_fn, kpad, vpad, valid, sink_ref, store_fn, rows):
    per_group = (H_B // KV_B) // 2
    for g in range(KV_B):
        pairs = list(range(g * per_group, (g + 1) * per_group))
        qs = jnp.concatenate([q_pair_fn(jj) for jj in pairs], axis=0).astype(BF16)
        acc = None
        for e in range(2):
            s = _dot_nt(qs, kpad[(g, e)])
            if valid is not None:
                s = jnp.where(valid, s, NEG)
            sink = jnp.concatenate([jnp.full((rows, 1), sink_ref[2 * jj + e], F32) for jj in pairs], axis=0)
            mx = jnp.maximum(jnp.max(s, axis=-1, keepdims=True), sink)
            pr = jnp.exp(s - mx)
            den = jnp.sum(pr, axis=-1, keepdims=True) + jnp.exp(sink - mx)
            o = _dot(pr.astype(BF16), vpad[(g, e)]) / den
            acc = o if acc is None else acc + o
        for t, jj in enumerate(pairs):
            store_fn(jj, acc[t * rows:(t + 1) * rows])


def _swa_prompt_kernel(sink_ref, q_all, kvc_all, kvp_all, cc_ref, sc_ref, cp_ref, sp_ref,
                       ob_all, ko_all, vo_all):
    n = pl.program_id(1)
    cc, sc = cc_ref[...], sc_ref[...]
    stacked = WINDOW * (H_B // KV_B) // 2
    i = lax.broadcasted_iota(jnp.int32, (stacked, 2 * WINDOW), 0) & (WINDOW - 1)
    j = lax.broadcasted_iota(jnp.int32, (stacked, 2 * WINDOW), 1)
    valid = (j >= i + 1) & (j <= i + WINDOW) & ((n > 0) | (j >= WINDOW))
    for u in range(MIX_NBAT):
        q_ref, ob_ref = q_all.at[u], ob_all.at[u]
        kvc, kvp = kvc_all[u], kvp_all[u]
        kc = _rope64(kvc[:, :128], cc, sc)
        kp = _rope64(kvp[:, :128], cp_ref[...], sp_ref[...])
        vc = kvc[:, 128:]
        kpad = _swa_padded(jnp.concatenate([kp, kc], axis=0))
        vpad = _swa_padded(jnp.concatenate([kvp[:, 128:], vc], axis=0))

        def q_pair(jj, q_ref=q_ref):
            return _rope64(q_ref[:, 128 * jj:128 * (jj + 1)], cc, sc) * (HD_B ** -0.5)

        def store(jj, acc, ob_ref=ob_ref):
            ob_ref[:, 128 * jj:128 * (jj + 1)] = acc.astype(BF16)

        _swa_attend(q_pair, kpad, vpad, valid, sink_ref, store, WINDOW)

        @pl.when(n == pl.num_programs(1) - 1)
        def _(u=u, kc=kc, vc=vc):
            ko_all[u] = kc
            vo_all[u] = vc


def _swa_prompt(p_main, p_small, cos, sin, sinks, *, nb, nc):
    seq = nc * CHUNK
    ng = nb // MIX_NBAT
    pg, psg = _batch_groups(p_main, nb, seq), _batch_groups(p_small, nb, seq)
    prev = lambda n: jnp.maximum(n - 1, 0)
    win = pl.BlockSpec((None, MIX_NBAT, WINDOW, 128), lambda b, n: (b, 0, 0, 0))
    ob, ko, vo = pl.pallas_call(
        _swa_prompt_kernel,
        grid=(ng, nc),
        in_specs=[
            pl.BlockSpec(memory_space=pltpu.SMEM),
            pl.BlockSpec((None, MIX_NBAT, CHUNK, 1024), lambda b, n: (b, 0, n, 3)),
            pl.BlockSpec((None, MIX_NBAT, CHUNK, 256), lambda b, n: (b, 0, n, 0)),
            pl.BlockSpec((None, MIX_NBAT, CHUNK, 256), lambda b, n: (b, 0, prev(n), 0)),
            pl.BlockSpec((CHUNK, 128), lambda b, n: (n, 0)),
            pl.BlockSpec((CHUNK, 128), lambda b, n: (n, 0)),
            pl.BlockSpec((CHUNK, 128), lambda b, n: (prev(n), 0)),
            pl.BlockSpec((CHUNK, 128), lambda b, n: (prev(n), 0)),
        ],
        out_specs=[pl.BlockSpec((None, MIX_NBAT, CHUNK, 1024), lambda b, n: (b, 0, n, 0)), win, win],
        out_shape=[jax.ShapeDtypeStruct((ng, MIX_NBAT, seq, 1024), BF16),
                   jax.ShapeDtypeStruct((ng, MIX_NBAT, WINDOW, 128), F32),
                   jax.ShapeDtypeStruct((ng, MIX_NBAT, WINDOW, 128), F32)],
        compiler_params=_cparams(("arbitrary", "arbitrary")),
        name="swa_prompt",
    )(sinks, pg, psg, psg, cos, sin, cos, sin)
    return ob.reshape(nb * seq, 1024), ko.reshape(nb, WINDOW, 128), vo.reshape(nb, WINDOW, 128)


def _col_bcast(row128):
    return jnp.transpose(jnp.broadcast_to(row128, (128, 128)))


def _even_sample_kernel(sink_ref, p_all, ps_all, s0_all, kbuf_all, vbuf_all, ca_ref, sa_ref, cb_ref, sb_ref,
                        gn_ref, mix_all, st_all, ko_all, vo_all):
    for u in range(SAMPLE_NSEQ):
        _even_sample_one(sink_ref, p_all.at[u], ps_all.at[u], s0_all.at[u], kbuf_all.at[u], vbuf_all.at[u],
                         ca_ref, sa_ref, cb_ref, sb_ref, gn_ref,
                         mix_all.at[u], st_all.at[u], ko_all.at[u], vo_all.at[u])


def _even_sample_one(sink_ref, p_ref, ps_ref, s0_ref, kbuf_ref, vbuf_ref, ca_ref, sa_ref, cb_ref, sb_ref,
                     gn_ref, mix_ref, st_ref, ko_ref, vo_ref):
    row = p_ref[...]
    ca, sa = ca_ref[...], sa_ref[...]
    qr = _rope64(jnp.broadcast_to(row[:, 0:512], (8, 512)), ca, sa)
    kr = _rope64(jnp.broadcast_to(row[:, 512:1024], (8, 512)), ca, sa) * (DK_A ** -0.5)
    r_i = lax.broadcasted_iota(jnp.int32, (128, 128), 0)
    top = r_i < 64
    for p in range(H_A // 2):
        sl = slice(128 * p, 128 * (p + 1))
        kcol = _col_bcast(kr[0:1, sl])
        qcol = _col_bcast(qr[0:1, sl])
        he, ho = 2 * p, 2 * p + 1
        v_e = row[:, 1024 + 128 * he:1024 + 128 * (he + 1)]
        v_o = row[:, 1024 + 128 * ho:1024 + 128 * (ho + 1)]
        gam = jnp.where(top, float(np.exp(_LOG_GAMMA[he])), float(np.exp(_LOG_GAMMA[ho])))
        s_new = gam * s0_ref[p] + kcol * jnp.where(top, v_e, v_o)
        st_ref[p] = s_new
        prod = qcol * s_new
        for e, h in ((0, he), (1, ho)):
            y = jnp.sum(jnp.where(top if e == 0 else ~top, prod, 0.0), axis=0, keepdims=True)
            hs = slice(128 * h, 128 * (h + 1))
            gate = row[:, 2048 + 128 * h:2048 + 128 * (h + 1)]
            mix_ref[:, hs] = _group_norm_gate(y, gate, gn_ref[:, hs])

    cb, sb = cb_ref[...], sb_ref[...]
    ps = ps_ref[...]
    kn = _rope64(jnp.broadcast_to(ps[:, 0:128], (8, 128)), cb, sb)[0:1]
    vn = ps[:, 128:256]
    last = r_i == WINDOW - 1
    k_new = jnp.where(last, kn, pltpu.roll(kbuf_ref[...], WINDOW - 1, axis=0))
    v_new = jnp.where(last, vn, pltpu.roll(vbuf_ref[...], WINDOW - 1, axis=0))
    ko_ref[...] = k_new
    vo_ref[...] = v_new
    kpad = _swa_padded(k_new)
    vpad = _swa_padded(v_new)

    def q_pair(jj):
        q8 = jnp.broadcast_to(row[:, 3072 + 128 * jj:3072 + 128 * (jj + 1)], (8, 128))
        return _rope64(q8, cb, sb) * (HD_B ** -0.5)

    def store(jj, acc):
        mix_ref[:, 1024 + 128 * jj:1024 + 128 * (jj + 1)] = acc[0:1]

    _swa_attend(q_pair, kpad, vpad, None, sink_ref, store, 8)


def _even_sample(p_main, p_small, s0, kbuf, vbuf, ca, sa, cb, sb, gn_w, sinks):
    nb = p_main.shape[0]
    b3 = lambda b: (b, 0, 0)
    b4 = lambda b: (b, 0, 0, 0)
    c2 = lambda b: (0, 0)
    ns = SAMPLE_NSEQ
    return pl.pallas_call(
        _even_sample_kernel,
        grid=(nb // ns,),
        in_specs=[
            pl.BlockSpec(memory_space=pltpu.SMEM),
            pl.BlockSpec((ns, 1, EVEN_MAIN), b3),
            pl.BlockSpec((ns, 1, EVEN_SMALL), b3),
            pl.BlockSpec((ns, 4, 128, 128), b4),
            pl.BlockSpec((ns, WINDOW, 128), b3),
            pl.BlockSpec((ns, WINDOW, 128), b3),
            pl.BlockSpec((1, 512), c2), pl.BlockSpec((1, 512), c2),
            pl.BlockSpec((1, 128), c2), pl.BlockSpec((1, 128), c2),
            pl.BlockSpec((1, 1024), c2),
        ],
        out_specs=[
            pl.BlockSpec((ns, 1, 2048), b3),
            pl.BlockSpec((ns, 4, 128, 128), b4),
            pl.BlockSpec((ns, WINDOW, 128), b3),
            pl.BlockSpec((ns, WINDOW, 128), b3),
        ],
        out_shape=[jax.ShapeDtypeStruct((nb, 1, 2048), F32), jax.ShapeDtypeStruct((nb, 4, 128, 128), F32),
                   jax.ShapeDtypeStruct((nb, WINDOW, 128), F32), jax.ShapeDtypeStruct((nb, WINDOW, 128), F32)],
        compiler_params=_cparams(("arbitrary",)),
        name="even_sample",
    )(sinks, p_main.reshape(nb, 1, EVEN_MAIN), p_small.reshape(nb, 1, EVEN_SMALL), s0, kbuf, vbuf,
      ca, sa, cb, sb, gn_w)


def _fox_q_tile(qi, q_ref, fcol_ref, frow_ref, kb_scr, vb_scr, o_ref, tq):
    r_i = lax.broadcasted_iota(jnp.int32, (tq, tq), 0)
    c_i = lax.broadcasted_iota(jnp.int32, (tq, tq), 1)
    q = (q_ref[...] * (HD_C ** -0.5)).astype(BF16)
    fq = fcol_ref[...]
    diag = slice(qi * tq, (qi + 1) * tq)
    s_d = jnp.where(c_i <= r_i, _dot_nt(q, kb_scr[diag, :]) + (fq - frow_ref[:, diag]), NEG)
    m = jnp.max(s_d, axis=-1, keepdims=True)
    if qi > 0:
        past = slice(0, qi * tq)
        s_p = _dot_nt(q, kb_scr[past, :]) + (fq - frow_ref[:, past])
        m = jnp.maximum(m, jnp.max(s_p, axis=-1, keepdims=True))
    p_d = jnp.exp(s_d - m)
    l = jnp.sum(p_d, axis=-1, keepdims=True)
    acc = _dot(p_d.astype(BF16), vb_scr[diag, :])
    if qi > 0:
        p_p = jnp.exp(s_p - m)
        l = l + jnp.sum(p_p, axis=-1, keepdims=True)
        acc = acc + _dot(p_p.astype(BF16), vb_scr[past, :])
    o_ref[...] = (acc / l).astype(BF16)


def _fox_kernel(*refs, pp, tq):
    pt_ref = refs[0]
    q_ref, kn_ref, vn_ref, lfn_ref = refs[1:5]
    k_refs = refs[5:5 + pp]
    v_refs = refs[5 + pp:5 + 2 * pp]
    lf_pool = refs[5 + 2 * pp]
    pq_ref, pk_ref, pv_ref, fcol_ref, frow_ref = refs[6 + 2 * pp:11 + 2 * pp]
    o_ref, po_ref = refs[11 + 2 * pp:13 + 2 * pp]
    m_scr, l_scr, acc_scr, carry_scr, rexp_scr, tri_scr, kb_scr, vb_scr = refs[13 + 2 * pp:]
    b = pl.program_id(0)
    s_id = pl.program_id(1)
    rows = PAGE_SIZE * H_C

    @pl.when(s_id == 0)
    def _():
        kb_scr[...] = pk_ref[...].astype(BF16)
        vb_scr[...] = pv_ref[...].astype(BF16)

    for qi in range(kb_scr.shape[0] // tq):
        @pl.when(s_id == qi)
        def _(qi=qi):
            _fox_q_tile(qi, pq_ref, fcol_ref, frow_ref, kb_scr, vb_scr, po_ref, tq)

    @pl.when((b == 0) & (s_id == 0))
    def _():
        r = lax.broadcasted_iota(jnp.int32, (PAGE_SIZE, rows), 0)
        c = lax.broadcasted_iota(jnp.int32, (PAGE_SIZE, rows), 1)
        rexp_scr[...] = jnp.where((c >> 3) == r, 1.0, 0.0).astype(BF16)
        r2 = lax.broadcasted_iota(jnp.int32, (PAGE_SIZE, PAGE_SIZE), 0)
        c2 = lax.broadcasted_iota(jnp.int32, (PAGE_SIZE, PAGE_SIZE), 1)
        tri_scr[...] = jnp.where(r2 <= c2, 1.0, 0.0).astype(BF16)

    @pl.when(s_id == 0)
    def _():
        m_scr[...] = jnp.full_like(m_scr, NEG)
        l_scr[...] = jnp.zeros_like(l_scr)
        acc_scr[...] = jnp.zeros_like(acc_scr)
        carry_scr[...] = jnp.zeros_like(carry_scr)

    qs = q_ref[...] * (HD_C ** -0.5)
    qb = qs.astype(BF16)
    h_i = lax.broadcasted_iota(jnp.int32, (H_C, rows), 0)
    c_i = lax.broadcasted_iota(jnp.int32, (H_C, rows), 1)
    diag = (c_i & (H_C - 1)) == h_i
    m, l, acc, carry = m_scr[...], l_scr[...], acc_scr[...], carry_scr[...]

    lf_all = jnp.concatenate([lf_pool[pt_ref[b, s_id * pp + r]] for r in range(pp)], axis=0)
    cum_in = _exact_right01(lf_all, tri_scr[...])
    offs = []
    for r in range(pp):
        offs.append(carry)
        carry = carry + cum_in[H_C * r:H_C * (r + 1), PAGE_SIZE - 1:PAGE_SIZE]
    gexp = _exact_right01(cum_in + jnp.concatenate(offs, axis=0), rexp_scr[...])

    logits = []
    for r in range(pp):
        k2t = jnp.transpose(k_refs[r][...].reshape(rows, HD_C)).astype(BF16)
        logits.append(jnp.where(diag, _dot(qb, k2t) - gexp[H_C * r:H_C * (r + 1)], NEG))
    mx = logits[0]
    for r in range(1, pp):
        mx = jnp.maximum(mx, logits[r])
    m_new = jnp.maximum(m, jnp.max(mx, axis=-1, keepdims=True))
    a = jnp.exp(m - m_new)
    psum = pv = None
    for r in range(pp):
        p = jnp.exp(logits[r] - m_new)
        o = _dot(p.astype(BF16), v_refs[r][...].reshape(rows, HD_C).astype(BF16))
        psum = p if psum is None else psum + p
        pv = o if pv is None else pv + o
    l = a * l + jnp.sum(psum, axis=-1, keepdims=True)
    acc = a * acc + pv
    m = m_new
    m_scr[...] = m
    l_scr[...] = l
    acc_scr[...] = acc
    carry_scr[...] = carry

    @pl.when(s_id == pl.num_programs(1) - 1)
    def _():
        fq = carry + lfn_ref[...]
        s_new = jnp.sum(qs * kn_ref[...], axis=-1, keepdims=True)
        m_past = m + fq
        mx = jnp.maximum(m_past, s_new)
        wp = jnp.exp(m_past - mx)
        wn = jnp.exp(s_new - mx)
        o_ref[...] = (acc * wp + wn * vn_ref[...]) / (l * wp + wn)


def _fox(page_table, q, k_new, v_new, lf_new, cache_k, cache_v, cache_lf, p_main, fcol, frow, *, pp, nb_p, seq):
    nb, n_pages = page_table.shape
    ns = n_pages // pp
    tq = seq // ns
    assert nb == nb_p * H_C and ns * tq == seq and ns * pp == n_pages
    b3 = lambda b, s, pt: (b, 0, 0)
    kv_spec = lambda r: pl.BlockSpec((None, PAGE_SIZE, H_C, HD_C), lambda b, s, pt: (pt[b, s * pp + r], 0, 0, 0))
    in_specs = [pl.BlockSpec((None, H_C, HD_C), b3), pl.BlockSpec((None, H_C, HD_C), b3),
                pl.BlockSpec((None, H_C, HD_C), b3), pl.BlockSpec((None, H_C, 1), b3)]
    in_specs += [kv_spec(r) for r in range(pp)] + [kv_spec(r) for r in range(pp)]
    in_specs += [
        pl.BlockSpec(cache_lf.shape, lambda b, s, pt: (0, 0, 0), pipeline_mode=pl.Buffered(1)),
        pl.BlockSpec((tq, HD_C), lambda b, s, pt: ((b // H_C) * ns + s, b % H_C)),
        pl.BlockSpec((seq, HD_C), lambda b, s, pt: (b // H_C, H_C + b % H_C), pipeline_mode=pl.Buffered(1)),
        pl.BlockSpec((seq, HD_C), lambda b, s, pt: (b // H_C, 2 * H_C + b % H_C), pipeline_mode=pl.Buffered(1)),
        pl.BlockSpec((None, None, tq, 1), lambda b, s, pt: (b // H_C, b % H_C, s, 0)),
        pl.BlockSpec((None, None, 1, seq), lambda b, s, pt: (b // H_C, b % H_C, 0, 0)),
    ]
    grid_spec = pltpu.PrefetchScalarGridSpec(
        num_scalar_prefetch=1, grid=(nb, ns), in_specs=in_specs,
        out_specs=[pl.BlockSpec((None, H_C, HD_C), b3),
                   pl.BlockSpec((tq, HD_C), lambda b, s, pt: ((b // H_C) * ns + s, b % H_C))],
        scratch_shapes=[pltpu.VMEM((H_C, 1), F32), pltpu.VMEM((H_C, 1), F32), pltpu.VMEM((H_C, HD_C), F32),
                        pltpu.VMEM((H_C, 1), F32), pltpu.VMEM((PAGE_SIZE, PAGE_SIZE * H_C), BF16),
                        pltpu.VMEM((PAGE_SIZE, PAGE_SIZE), BF16),
                        pltpu.VMEM((seq, HD_C), BF16), pltpu.VMEM((seq, HD_C), BF16)],
    )
    return pl.pallas_call(
        functools.partial(_fox_kernel, pp=pp, tq=tq),
        grid_spec=grid_spec,
        out_shape=[jax.ShapeDtypeStruct((nb, H_C, HD_C), F32),
                   jax.ShapeDtypeStruct((nb_p * seq, H_C * HD_C), BF16)],
        compiler_params=_cparams(("arbitrary", "arbitrary"), MLP_VMEM_LIMIT),
        name="fox",
    )(page_table, q, k_new, v_new, lf_new, *([cache_k] * pp), *([cache_v] * pp), cache_lf,
      p_main, p_main, p_main, fcol, frow)


def _lane_col(x, lane):
    return x[:, lane:lane + 1]


def _ssd_prompt_kernel(x_all, bc_all, z_all, dt_all, cwx_ref, cwbc_ref, cbx_ref, cbbc_ref, dtb_ref, alog_ref,
                       dskip_ref, nw_ref, od_all, st_all, cv_all, xpx_all, xpbc_all, s_all, y_all):
    c = pl.program_id(1)
    nc = pl.num_programs(1)

    @pl.when(c == 0)
    def _():
        xpx_all[:, 0:8, :] = jnp.zeros((MIX_NBAT, 8, DI_D), F32)
        xpbc_all[:, 0:8, :] = jnp.zeros((MIX_NBAT, 8, 512), F32)
        s_all[...] = jnp.zeros_like(s_all)

    for u in range(MIX_NBAT):
        _ssd_chunk(x_all.at[u], bc_all.at[u], z_all.at[u], dt_all.at[u], cwx_ref, cwbc_ref, cbx_ref, cbbc_ref,
                   dtb_ref, alog_ref, dskip_ref, nw_ref, od_all.at[u], cv_all.at[u],
                   xpx_all.at[u], xpbc_all.at[u], s_all.at[u], y_all.at[u])

    @pl.when(c == nc - 1)
    def _():
        st_all[...] = s_all[...]


def _ssd_chunk(x_ref, bc_ref, z_ref, dt_ref, cwx_ref, cwbc_ref, cbx_ref, cbbc_ref, dtb_ref, alog_ref,
               dskip_ref, nw_ref, od_ref, cv_ref, xpx_scr, xpbc_scr, s_scr, y_scr):
    xpx_scr[8:8 + CHUNK, :] = x_ref[...]
    xpbc_scr[8:8 + CHUNK, :] = bc_ref[...]
    cx = cbx_ref[...]
    cbc = cbbc_ref[...]
    for k in range(CONV_W):
        w = CONV_W - 1 - k
        cx = cx + cwx_ref[w:w + 1, :] * xpx_scr[8 - k:8 - k + CHUNK, :]
        cbc = cbc + cwbc_ref[w:w + 1, :] * xpbc_scr[8 - k:8 - k + CHUNK, :]

    cv_ref[:, 0:DI_D] = xpx_scr[CHUNK + 5:CHUNK + 8, :]
    cv_ref[:, DI_D:CONV_CH] = xpbc_scr[CHUNK + 5:CHUNK + 8, :]
    xpx_scr[0:8, :] = xpx_scr[CHUNK:CHUNK + 8, :]
    xpbc_scr[0:8, :] = xpbc_scr[CHUNK:CHUNK + 8, :]

    xs = _silu(cx)
    bcs = _silu(cbc)
    dt = _softplus(dt_ref[...] + dtb_ref[...])
    la = dt * (-jnp.exp(alog_ref[...]))
    cum = _exact_left01(_tri_lower(CHUNK), la)
    cum_t = jnp.transpose(cum)
    t_i = lax.broadcasted_iota(jnp.int32, (CHUNK, CHUNK), 0)
    s_i = lax.broadcasted_iota(jnp.int32, (CHUNK, CHUNK), 1)
    causal = t_i >= s_i
    lo = s_i < 64
    att_base, bt = [], []
    for g in range(G_D):
        bg = bcs[:, 128 * g:128 * (g + 1)]
        cg = bcs[:, 256 + 128 * g:256 + 128 * (g + 1)]
        att_base.append(_dot_nt(cg.astype(BF16), bg.astype(BF16)))
        bt.append(jnp.transpose(bg).astype(BF16))
    ss = jnp.zeros((CHUNK, 1), F32)
    for p in range(H_D // 2):
        g = (2 * p) // (H_D // G_D)
        sl = slice(128 * p, 128 * (p + 1))
        le, lo_ = DT_LANE + 2 * p, DT_LANE + 2 * p + 1
        dt_pair = jnp.where(lo, _lane_col(dt, le), _lane_col(dt, lo_))
        cum_pair = jnp.where(lo, _lane_col(cum, le), _lane_col(cum, lo_))
        clast = cum_pair[CHUNK - 1:CHUNK, :]
        xs_p = xs[:, sl]
        xdt = xs_p * dt_pair
        s_old = s_scr[p]
        cgb = bcs[:, 256 + 128 * g:256 + 128 * (g + 1)].astype(BF16)
        y = _dot(cgb, s_old.astype(BF16)) * jnp.exp(cum_pair)
        for e in range(2):
            ln = DT_LANE + 2 * p + e
            diff = _lane_col(cum, ln) - cum_t[ln:ln + 1, :]
            att = (att_base[g] * jnp.exp(jnp.where(causal, diff, NEG))).astype(BF16)
            xm = jnp.where(lo if e == 0 else ~lo, xdt, 0.0).astype(BF16)
            y = y + _dot(att, xm)
        y = (y + xs_p * dskip_ref[:, sl]) * _silu(z_ref[:, sl])
        y_scr[:, sl] = y
        ss = ss + jnp.sum(y * y, axis=-1, keepdims=True)
        txdt = (xdt * jnp.exp(clast - cum_pair)).astype(BF16)
        s_scr[p] = jnp.exp(clast) * s_old + _dot(bt[g], txdt)
    inv = lax.rsqrt(ss * (1.0 / DI_D) + EPS)
    od_ref[...] = (y_scr[...] * inv * nw_ref[...]).astype(BF16)


def _ssd_prompt(p_main, p_small, cwx, cwbc, cbx, cbbc, dtb, alog, dskip, nw, *, nb, nc):
    seq = nc * CHUNK
    ng = nb // MIX_NBAT
    pg, psg = _batch_groups(p_main, nb, seq), _batch_groups(p_small, nb, seq)
    blk = lambda w, col: pl.BlockSpec((None, MIX_NBAT, CHUNK, w), lambda b, c: (b, 0, c, col))
    c2 = lambda b, c: (0, 0)
    od, st, cv = pl.pallas_call(
        _ssd_prompt_kernel,
        grid=(ng, nc),
        in_specs=[
            blk(1024, 4), blk(512, 10), blk(1024, 3), blk(ODD_SMALL, 0),
            pl.BlockSpec((CONV_W, DI_D), c2), pl.BlockSpec((CONV_W, 512), c2),
            pl.BlockSpec((1, DI_D), c2), pl.BlockSpec((1, 512), c2),
            pl.BlockSpec((1, ODD_SMALL), c2), pl.BlockSpec((1, ODD_SMALL), c2),
            pl.BlockSpec((1, DI_D), c2), pl.BlockSpec((1, DI_D), c2),
        ],
        out_specs=[
            blk(DI_D, 0),
            pl.BlockSpec((None, MIX_NBAT, 8, 128, 128), lambda b, c: (b, 0, 0, 0, 0)),
            pl.BlockSpec((None, MIX_NBAT, CONV_W - 1, CONV_CH), lambda b, c: (b, 0, 0, 0)),
        ],
        out_shape=[jax.ShapeDtypeStruct((ng, MIX_NBAT, seq, DI_D), BF16),
                   jax.ShapeDtypeStruct((ng, MIX_NBAT, 8, 128, 128), F32),
                   jax.ShapeDtypeStruct((ng, MIX_NBAT, CONV_W - 1, CONV_CH), F32)],
        scratch_shapes=[pltpu.VMEM((MIX_NBAT, CHUNK + 8, DI_D), F32), pltpu.VMEM((MIX_NBAT, CHUNK + 8, 512), F32),
                        pltpu.VMEM((MIX_NBAT, 8, 128, 128), F32), pltpu.VMEM((MIX_NBAT, CHUNK, DI_D), F32)],
        compiler_params=_cparams(("arbitrary", "arbitrary")),
        name="ssd_prompt",
    )(pg, pg, pg, psg, cwx, cwbc, cbx, cbbc, dtb, alog, dskip, nw)
    return od.reshape(nb * seq, DI_D), st.reshape(nb, 8, 128, 128), cv.reshape(nb, CONV_W - 1, CONV_CH)


def _ssd_sample_kernel(p_all, dt_all, cs_all, s0_all, cwx_ref, cwbc_ref, cbx_ref, cbbc_ref, dtb_ref, alog_ref,
                       dskip_ref, nw_ref, od_all, st_all, cv_all):
    for u in range(SAMPLE_NSEQ):
        _ssd_sample_one(p_all.at[u], dt_all.at[u], cs_all.at[u], s0_all.at[u], cwx_ref, cwbc_ref, cbx_ref,
                        cbbc_ref, dtb_ref, alog_ref, dskip_ref, nw_ref, od_all.at[u], st_all.at[u], cv_all.at[u])


def _ssd_sample_one(p_ref, dt_ref, cs_ref, s0_ref, cwx_ref, cwbc_ref, cbx_ref, cbbc_ref, dtb_ref, alog_ref,
                    dskip_ref, nw_ref, od_ref, st_ref, cv_ref):
    row = p_ref[...]
    xn = row[:, 4096:5120]
    bcn = row[:, 5120:5632]
    cx = cbx_ref[...] + cwx_ref[3:4, :] * xn
    cbc = cbbc_ref[...] + cwbc_ref[3:4, :] * bcn
    for w in range(CONV_W - 1):
        cx = cx + cwx_ref[w:w + 1, :] * cs_ref[w:w + 1, 0:DI_D]
        cbc = cbc + cwbc_ref[w:w + 1, :] * cs_ref[w:w + 1, DI_D:CONV_CH]
    cv_ref[0:1, :] = cs_ref[1:2, :]
    cv_ref[1:2, :] = cs_ref[2:3, :]
    cv_ref[2:3, 0:DI_D] = xn
    cv_ref[2:3, DI_D:CONV_CH] = bcn
    xs = _silu(cx)
    bcs = _silu(cbc)
    dt = _softplus(dt_ref[...] + dtb_ref[...])
    da = jnp.exp(dt * (-jnp.exp(alog_ref[...])))
    lane = lax.broadcasted_iota(jnp.int32, (1, 128), 1)
    lo = lane < 64
    bcol = [_col_bcast(bcs[:, 128 * g:128 * (g + 1)]) for g in range(G_D)]
    ccol = [_col_bcast(bcs[:, 256 + 128 * g:256 + 128 * (g + 1)]) for g in range(G_D)]
    ys = []
    ss = jnp.zeros((1, 1), F32)
    for p in range(H_D // 2):
        g = (2 * p) // (H_D // G_D)
        sl = slice(128 * p, 128 * (p + 1))
        le, lo_ = DT_LANE + 2 * p, DT_LANE + 2 * p + 1
        dt_pair = jnp.where(lo, _lane_col(dt, le), _lane_col(dt, lo_))
        da_pair = jnp.where(lo, _lane_col(da, le), _lane_col(da, lo_))
        xs_p = xs[:, sl]
        s_old = jnp.concatenate([s0_ref[2 * p], s0_ref[2 * p + 1]], axis=1)
        s_new = da_pair * s_old + bcol[g] * (xs_p * dt_pair)
        st_ref[2 * p] = s_new[:, :HD_D]
        st_ref[2 * p + 1] = s_new[:, HD_D:]
        y = jnp.sum(ccol[g] * s_new, axis=0, keepdims=True)
        y = (y + xs_p * dskip_ref[:, sl]) * _silu(row[:, 3072 + 128 * p:3072 + 128 * (p + 1)])
        ys.append(y)
        ss = ss + jnp.sum(y * y, axis=-1, keepdims=True)
    inv = lax.rsqrt(ss * (1.0 / DI_D) + EPS)
    for p in range(H_D // 2):
        sl = slice(128 * p, 128 * (p + 1))
        od_ref[:, sl] = ys[p] * inv * nw_ref[:, sl]


def _ssd_sample(p_main, p_small, cs, s0, cwx, cwbc, cbx, cbbc, dtb, alog, dskip, nw):
    nb = p_main.shape[0]
    b3 = lambda b: (b, 0, 0)
    b4 = lambda b: (b, 0, 0, 0)
    c2 = lambda b: (0, 0)
    ns = SAMPLE_NSEQ
    return pl.pallas_call(
        _ssd_sample_kernel,
        grid=(nb // ns,),
        in_specs=[
            pl.BlockSpec((ns, 1, ODD_MAIN), b3),
            pl.BlockSpec((ns, 1, ODD_SMALL), b3),
            pl.BlockSpec((ns, CONV_W - 1, CONV_CH), b3),
            pl.BlockSpec((ns, H_D, N_D, HD_D), b4),
            pl.BlockSpec((CONV_W, DI_D), c2), pl.BlockSpec((CONV_W, 512), c2),
            pl.BlockSpec((1, DI_D), c2), pl.BlockSpec((1, 512), c2),
            pl.BlockSpec((1, ODD_SMALL), c2), pl.BlockSpec((1, ODD_SMALL), c2),
            pl.BlockSpec((1, DI_D), c2), pl.BlockSpec((1, DI_D), c2),
        ],
        out_specs=[
            pl.BlockSpec((ns, 1, DI_D), b3),
            pl.BlockSpec((ns, H_D, N_D, HD_D), b4),
            pl.BlockSpec((ns, CONV_W - 1, CONV_CH), b3),
        ],
        out_shape=[jax.ShapeDtypeStruct((nb, 1, DI_D), F32), jax.ShapeDtypeStruct((nb, H_D, N_D, HD_D), F32),
                   jax.ShapeDtypeStruct((nb, CONV_W - 1, CONV_CH), F32)],
        compiler_params=_cparams(("arbitrary",)),
        name="ssd_sample",
    )(p_main.reshape(nb, 1, ODD_MAIN), p_small.reshape(nb, 1, ODD_SMALL), cs, s0,
      cwx, cwbc, cbx, cbbc, dtb, alog, dskip, nw)


def _pairs_to_heads(s):
    b, p, n, _ = s.shape
    return s.reshape(b, p, n, 2, 64).transpose(0, 1, 3, 2, 4).reshape(b, 2 * p, n, 64)


def _pad_lanes(v, start, width=ODD_SMALL):
    out = jnp.zeros((1, width), F32)
    return lax.dynamic_update_slice(out, v.reshape(1, -1).astype(F32), (0, start))


def kernel(x_prompt, x_sample, state_ret, cache_swa_k, cache_swa_v, cache_fox_k, cache_fox_v, cache_fox_logf,
           state_ssm, state_conv, page_table, norm_mix_pre, norm_mix_post, norm_mlp_pre, norm_mlp_post,
           w_in_even, w_out_even, ret_norm_w, swa_sinks, w_in_odd, w_out_odd, fox_fb, conv_w, conv_b,
           dt_bias, a_log, d_skip, ssd_norm_w, w_up, w_down):
    nb, seq = BATCH, SEQ
    nc = seq // CHUNK
    mp = nb * seq
    ms = DEC_BATCH
    xp = x_prompt.reshape(mp, D_MODEL)
    xs = x_sample.reshape(ms, D_MODEL)
    row = lambda v: v.reshape(1, -1)

    pos_p = jnp.arange(seq, dtype=jnp.int32)
    pos_s = jnp.full((1,), PAST_LEN, dtype=jnp.int32)
    ca_p, sa_p = _rope_tables(pos_p, RET_THETA, 8)
    cb_p, sb_p = _rope_tables(pos_p, ROPE_THETA_B, 2)
    ca_s, sa_s = _rope_tables(pos_s, RET_THETA, 8)
    cb_s, sb_s = _rope_tables(pos_s, ROPE_THETA_B, 2)

    we = w_in_even[0]
    we_main = we[:, :EVEN_MAIN].astype(BF16)
    we_small = we[:, EVEN_MAIN:].astype(BF16)
    wo = w_out_even[0].astype(BF16)
    wo_a, wo_b = wo[:1024], wo[1024:]
    g_pre, g_post = row(norm_mix_pre[0]), row(norm_mix_post[0])
    gm_pre, gm_post = row(norm_mlp_pre[0]), row(norm_mlp_post[0])
    gn_w = row(ret_norm_w[0])
    sinks = swa_sinks[0]

    pm, psm, sm, ssm_, wu_b, wd_b = _proj(
        xp, xs, g_pre, (we_main,), we_small, [(w_up, 0, D_MODEL, D_FF), (w_down, 0, D_FF, D_MODEL)],
        odd=False, tm=256, tn=512, seq=seq)
    out_a, ret_p = _ret_prompt(pm, ca_p, sa_p, gn_w, nb=nb, nc=nc)
    out_b, swak_p, swav_p = _swa_prompt(pm, psm, cb_p, sb_p, sinks, nb=nb, nc=nc)
    mix_s, ret_s, swak_s, swav_s = _even_sample(
        sm, ssm_, state_ret[0].reshape(ms, 4, 128, 128), cache_swa_k[0].reshape(ms, WINDOW, 128),
        cache_swa_v[0].reshape(ms, WINDOW, 128), ca_s, sa_s, cb_s, sb_s, gn_w, sinks)
    mix_s = mix_s.reshape(ms, 2048).astype(BF16)
    xp, xs = _outproj(out_a, out_b, xp, mix_s[:, :1024], mix_s[:, 1024:], xs, wo_a, wo_b, g_post, tm=512)
    xp, xs = _mlp(xp, xs, gm_pre, gm_post, wu_b, wd_b, tm=512, tf=1024)

    wod = w_in_odd[0].astype(BF16)
    wod_qkv = wod[:, :3072]
    wod_zx = wod[:, 3080:5640]
    wod_small = jnp.concatenate(
        [wod[:, 3072:3080], wod[:, 5640:5656], jnp.zeros((D_MODEL, ODD_SMALL - 24), BF16)], axis=1)
    wo1 = w_out_odd[0].astype(BF16)
    wo_c, wo_d = wo1[:1024], wo1[1024:]
    g_pre, g_post = row(norm_mix_pre[1]), row(norm_mix_post[1])
    gm_pre, gm_post = row(norm_mlp_pre[1]), row(norm_mlp_post[1])
    fb = _pad_lanes(fox_fb[0], 0)
    dtb = _pad_lanes(dt_bias[0], DT_LANE)
    alog = _pad_lanes(a_log[0], DT_LANE)
    cw = conv_w[0]
    cwx, cwbc = cw[:, :DI_D], cw[:, DI_D:]
    cbx, cbbc = row(conv_b[0][:DI_D]), row(conv_b[0][DI_D:])
    dskip = row(jnp.repeat(d_skip[0], HD_D))
    nw = row(ssd_norm_w[0])

    pm, psm, sm, ssm_, wu_b, wd_b, lf_p, fc_p, lf_s, k3_p, v3_p = _proj(
        xp, xs, g_pre, (wod_qkv, wod_zx), wod_small, [(w_up, 1, D_MODEL, D_FF), (w_down, 1, D_FF, D_MODEL)],
        fb, odd=True, tm=256, tn=512, seq=seq)
    fc = fc_p[:, :H_C].reshape(nb, seq, H_C).transpose(0, 2, 1)
    out_d, ssm_pairs_p, conv_p = _ssd_prompt(pm, psm, cwx, cwbc, cbx, cbbc, dtb, alog, dskip, nw, nb=nb, nc=nc)
    fox_k_p = k3_p.reshape(1, nb, seq, H_C, HD_C)
    fox_v_p = v3_p.reshape(1, nb, seq, H_C, HD_C)
    fox_lf_p = lf_p[:, :H_C].reshape(1, nb, seq, H_C)

    q_s = sm[:, 0:1024].reshape(ms, H_C, HD_C)
    k_s = sm[:, 1024:2048].reshape(ms, H_C, HD_C)
    v_s = sm[:, 2048:3072].reshape(ms, H_C, HD_C)
    lf_s8 = lf_s[:, :H_C]
    out_c_s, out_c = _fox(page_table, q_s, k_s, v_s, lf_s8.reshape(ms, H_C, 1),
                          cache_fox_k[0], cache_fox_v[0], cache_fox_logf[0].transpose(0, 2, 1),
                          pm, fc[..., None], fc[:, :, None, :], pp=16, nb_p=nb, seq=seq)
    out_d_s, ssm_s, conv_s = _ssd_sample(sm, ssm_, state_conv[0], state_ssm[0],
                                               cwx, cwbc, cbx, cbbc, dtb, alog, dskip, nw)
    xp, xs = _outproj(out_c, out_d, xp, out_c_s.reshape(ms, 1024).astype(BF16),
                      out_d_s.reshape(ms, DI_D).astype(BF16), xs, wo_c, wo_d, g_post, tm=512)
    xp, xs = _mlp(xp, xs, gm_pre, gm_post, wu_b, wd_b, tm=512, tf=1024)

    return (
        xp.reshape(nb, seq, D_MODEL), xs.reshape(ms, 1, D_MODEL),
        ret_p.reshape(1, nb, H_A, DK_A, DV_A), ret_s.reshape(1, ms, H_A, DK_A, DV_A),
        swak_p.reshape(1, nb, WINDOW, KV_B, HD_B), swav_p.reshape(1, nb, WINDOW, KV_B, HD_B),
        swak_s.reshape(1, ms, WINDOW, KV_B, HD_B), swav_s.reshape(1, ms, WINDOW, KV_B, HD_B),
        fox_k_p, fox_v_p, fox_lf_p,
        k_s.reshape(1, ms, 1, H_C, HD_C), v_s.reshape(1, ms, 1, H_C, HD_C), lf_s8.reshape(1, ms, 1, H_C),
        _pairs_to_heads(ssm_pairs_p)[None], ssm_s[None],
        conv_p[None], conv_s[None],
    )
```

```python
import functools

import numpy as np
import jax
import jax.numpy as jnp
from jax import lax
from jax.experimental import pallas as pl
from jax.experimental.pallas import tpu as pltpu

F32 = jnp.float32
BF16 = jnp.bfloat16

D_MODEL = 2048
BATCH = 4
SEQ = 2048
DEC_BATCH = 32
PAST_LEN = 16384
PAGE_SIZE = 128
D_FF = 4 * D_MODEL
EPS = 1e-6
GN_EPS = 1e-5
CHUNK = 128

H_A, DK_A, DV_A = 8, 64, 128
RET_THETA = 10000.0
H_B, KV_B, HD_B = 16, 2, 64
WINDOW = 128
ROPE_THETA_B = 150000.0
H_C, HD_C = 8, 128
H_D, HD_D, G_D, N_D = 16, 64, 2, 128
CONV_W = 4
DI_D = H_D * HD_D
CONV_CH = DI_D + 2 * G_D * N_D

EVEN_MAIN = 4096
EVEN_SMALL = 256
ODD_MAIN = 5632
ODD_SMALL = 128
DT_LANE = 8

NEG = -1e30
VMEM_LIMIT = 56 * 1024 * 1024
MLP_VMEM_LIMIT = 60 * 1024 * 1024
MLP_SUB_ROWS = 256
MIX_NBAT = 2
SAMPLE_NSEQ = 4

_LOG_GAMMA = [float(v) for v in np.log1p(-np.exp2(-5.0 - np.arange(H_A, dtype=np.float64)))]


def _cparams(sem, vmem_limit=VMEM_LIMIT):
    return pltpu.CompilerParams(dimension_semantics=sem, vmem_limit_bytes=vmem_limit)


def _silu(x):
    return x * jax.nn.sigmoid(x)


def _softplus(x):
    return jnp.maximum(x, 0.0) + jnp.log1p(jnp.exp(-jnp.abs(x)))


def _log_sigmoid(x):
    return jnp.minimum(x, 0.0) - jnp.log1p(jnp.exp(-jnp.abs(x)))


def _rms(x, g):
    ms = jnp.mean(x * x, axis=-1, keepdims=True)
    return (x * lax.rsqrt(ms + EPS)) * g


def _dot(a, b):
    return jnp.dot(a, b, preferred_element_type=F32)


def _dot_nt(a, b):
    return lax.dot_general(a, b, (((1,), (1,)), ((), ())), preferred_element_type=F32)


def _split3(x):
    hi = x.astype(BF16)
    r = x - hi.astype(F32)
    mid = r.astype(BF16)
    lo = (r - mid.astype(F32)).astype(BF16)
    return hi, mid, lo


def _exact_left01(m01, x):
    hi, mid, lo = _split3(x)
    return _dot(m01, hi) + _dot(m01, mid) + _dot(m01, lo)


def _exact_right01(x, m01):
    hi, mid, lo = _split3(x)
    return _dot(hi, m01) + _dot(mid, m01) + _dot(lo, m01)


def _tri_lower(n):
    r = lax.broadcasted_iota(jnp.int32, (n, n), 0)
    c = lax.broadcasted_iota(jnp.int32, (n, n), 1)
    return jnp.where(r >= c, 1.0, 0.0).astype(BF16)


def _rope64(x, c, s):
    w = x.shape[-1]
    ax = x.ndim - 1
    lane = lax.broadcasted_iota(jnp.int32, x.shape, ax)
    first = (lane & 32) == 0
    left = pltpu.roll(x, w - 32, axis=ax)
    right = pltpu.roll(x, 32, axis=ax)
    return x * c + jnp.where(first, left, right) * s


def _rope_tables(pos, theta, reps):
    inv = 1.0 / (theta ** (jnp.arange(32, dtype=F32) * (2.0 / 64)))
    ang = pos.astype(F32)[:, None] * inv[None, :]
    cos, sin = jnp.cos(ang), jnp.sin(ang)
    c = jnp.concatenate([cos, cos], axis=-1)
    s = jnp.concatenate([-sin, sin], axis=-1)
    return jnp.tile(c, (1, reps)), jnp.tile(s, (1, reps))


def _proj_kernel(*refs, odd, n_w, n_c, tm, tn, seq):
    x_ref, xs_ref, g_ref = refs[0:3]
    w_refs = refs[3:3 + n_w]
    wsm_ref = refs[3 + n_w]
    cast_src = refs[4 + n_w:4 + n_w + n_c]
    refs = refs[4 + n_w + n_c:]
    if odd:
        fb_ref, refs = refs[0], refs[1:]
    o_ref, osm_ref, os_ref, ossm_ref = refs[0:4]
    cast_dst = refs[4:4 + n_c]
    refs = refs[4 + n_c:]
    kv_refs = None
    if odd:
        lf_ref, fc_ref, lfs_ref, k3_ref, v3_ref, h_scr, carry_scr = refs
        kv_refs = (k3_ref, v3_ref)
    else:
        (h_scr,) = refs
    i = pl.program_id(0)
    for src, dst in zip(cast_src, cast_dst):
        dst[...] = src[...].astype(BF16)
    width = H_C * HD_C

    def rows(x_r, o_r, osm_r, n, kv=None):
        h_scr[0:n, :] = _rms(x_r[...], g_ref[...]).astype(BF16)
        sm = _dot(h_scr[0:n, :], wsm_ref[...])
        osm_r[...] = sm
        base = 0
        for w_ref in w_refs:
            for t in range(w_ref.shape[1] // tn):
                c0 = base + t * tn
                val = _dot(h_scr[0:n, :], w_ref[:, t * tn:(t + 1) * tn])
                o_r[:, c0:c0 + tn] = val
                if kv is not None and width <= c0 < 3 * width:
                    dst = kv[(c0 - width) // width]
                    h0 = ((c0 - width) % width) // HD_C
                    for hh in range(tn // HD_C):
                        dst[:, h0 + hh, :] = val[:, HD_C * hh:HD_C * (hh + 1)]
            base += w_ref.shape[1]
        return sm

    sm = rows(x_ref, o_ref, osm_ref, tm, kv_refs)
    if odd:
        lf = _log_sigmoid(sm + fb_ref[...])
        lf_ref[...] = lf

        @pl.when((i * tm) % seq == 0)
        def _():
            carry_scr[...] = jnp.zeros_like(carry_scr)

        f = _exact_left01(_tri_lower(tm), lf) + carry_scr[...]
        fc_ref[...] = f
        carry_scr[...] = f[tm - 1:tm, :]

    @pl.when(i == 0)
    def _():
        sms = rows(xs_ref, os_ref, ossm_ref, xs_ref.shape[0])
        if odd:
            lfs_ref[...] = _log_sigmoid(sms + fb_ref[...])


def _proj(x, xs, g, w_mains, w_small, casts, fb=None, *, odd, tm, tn, seq):
    m, ms = x.shape[0], xs.shape[0]
    n_main = sum(w.shape[1] for w in w_mains)
    n_small = w_small.shape[1]
    ni = m // tm
    c2 = lambda i: (0, 0)
    r2 = lambda i: (i, 0)
    in_specs = [
        pl.BlockSpec((tm, D_MODEL), r2),
        pl.BlockSpec((ms, D_MODEL), c2),
        pl.BlockSpec((1, D_MODEL), c2),
    ] + [pl.BlockSpec((D_MODEL, w.shape[1]), c2, pipeline_mode=pl.Buffered(1)) for w in w_mains] + [
        pl.BlockSpec((D_MODEL, n_small), c2),
    ]
    args = [x, xs, g, *w_mains, w_small]
    small = jax.ShapeDtypeStruct((m, n_small), F32)
    small_s = jax.ShapeDtypeStruct((ms, n_small), F32)
    small_spec = pl.BlockSpec((tm, n_small), r2)
    small_s_spec = pl.BlockSpec((ms, n_small), c2)
    out_shape = [jax.ShapeDtypeStruct((m, n_main), F32), small, jax.ShapeDtypeStruct((ms, n_main), F32), small_s]
    out_specs = [pl.BlockSpec((tm, n_main), r2), small_spec, pl.BlockSpec((ms, n_main), c2), small_s_spec]
    for src, lead, nrows, ncols in casts:
        rb = nrows // ni
        if lead is None:
            in_specs.append(pl.BlockSpec((rb, ncols), r2))
        else:
            in_specs.append(pl.BlockSpec((None, rb, ncols), lambda i, lead=lead: (lead, i, 0)))
        args.append(src)
        out_shape.append(jax.ShapeDtypeStruct((nrows, ncols), BF16))
        out_specs.append(pl.BlockSpec((rb, ncols), r2))
    scratch = [pltpu.VMEM((tm, D_MODEL), BF16)]
    if odd:
        in_specs.append(pl.BlockSpec((1, n_small), c2))
        args.append(fb)
        rows3 = jax.ShapeDtypeStruct((m, H_C, HD_C), F32)
        rows3_spec = pl.BlockSpec((tm, H_C, HD_C), lambda i: (i, 0, 0))
        out_shape += [small, small, small_s, rows3, rows3]
        out_specs += [small_spec, small_spec, small_s_spec, rows3_spec, rows3_spec]
        scratch.append(pltpu.VMEM((1, n_small), F32))
    return pl.pallas_call(
        functools.partial(_proj_kernel, odd=odd, n_w=len(w_mains), n_c=len(casts), tm=tm, tn=tn, seq=seq),
        grid=(ni,), in_specs=in_specs, out_specs=out_specs, out_shape=out_shape,
        scratch_shapes=scratch, compiler_params=_cparams(("arbitrary",), MLP_VMEM_LIMIT),
        name="proj_odd" if odd else "proj_even",
    )(*args)


def _outproj_kernel(a_ref, b_ref, x_ref, as_ref, bs_ref, xs_ref, wa_ref, wb_ref, g_ref, o_ref, os_ref):
    g = g_ref[...]
    n = x_ref.shape[0]
    sub = min(n, MLP_SUB_ROWS)
    for r in range(n // sub):
        rows = slice(r * sub, (r + 1) * sub)
        y = _dot(a_ref[rows, :], wa_ref[...]) + _dot(b_ref[rows, :], wb_ref[...])
        o_ref[rows, :] = x_ref[rows, :] + _rms(y, g)

    @pl.when(pl.program_id(0) == 0)
    def _():
        ys = _dot(as_ref[...], wa_ref[...]) + _dot(bs_ref[...], wb_ref[...])
        os_ref[...] = xs_ref[...] + _rms(ys, g)


def _outproj(a, b, x, a_s, b_s, xs, wa, wb, g, *, tm):
    m, ms = x.shape[0], xs.shape[0]
    ka, kb = a.shape[1], b.shape[1]
    c2 = lambda i: (0, 0)
    return pl.pallas_call(
        _outproj_kernel,
        grid=(m // tm,),
        in_specs=[
            pl.BlockSpec((tm, ka), lambda i: (i, 0)),
            pl.BlockSpec((tm, kb), lambda i: (i, 0)),
            pl.BlockSpec((tm, D_MODEL), lambda i: (i, 0)),
            pl.BlockSpec((ms, ka), c2),
            pl.BlockSpec((ms, kb), c2),
            pl.BlockSpec((ms, D_MODEL), c2),
            pl.BlockSpec((ka, D_MODEL), c2),
            pl.BlockSpec((kb, D_MODEL), c2),
            pl.BlockSpec((1, D_MODEL), c2),
        ],
        out_specs=[pl.BlockSpec((tm, D_MODEL), lambda i: (i, 0)), pl.BlockSpec((ms, D_MODEL), c2)],
        out_shape=[jax.ShapeDtypeStruct((m, D_MODEL), F32), jax.ShapeDtypeStruct((ms, D_MODEL), F32)],
        compiler_params=_cparams(("arbitrary",)),
        name="outproj",
    )(a, b, x, a_s, b_s, xs, wa, wb, g)


def _mlp_kernel(x_ref, xs_ref, gpre_ref, gpost_ref, wu_ref, wd_ref, o_ref, os_ref, h_scr, hs_scr):
    i = pl.program_id(0)
    j = pl.program_id(1)
    last = pl.num_programs(1) - 1

    def group(x_r, o_r, h_s):
        @pl.when(j == 0)
        def _():
            h_s[...] = _rms(x_r[...], gpre_ref[...]).astype(BF16)
            o_r[...] = jnp.zeros_like(o_r)

        n = x_r.shape[0]
        sub = min(n, MLP_SUB_ROWS)
        for r in range(n // sub):
            rows = slice(r * sub, (r + 1) * sub)
            u = jnp.maximum(_dot(h_s[rows, :], wu_ref[...]), 0.0)
            o_r[rows, :] += _dot((u * u).astype(BF16), wd_ref[...])

        @pl.when(j == last)
        def _():
            o_r[...] = x_r[...] + _rms(o_r[...], gpost_ref[...])

    group(x_ref, o_ref, h_scr)

    @pl.when(i == 0)
    def _():
        group(xs_ref, os_ref, hs_scr)


def _mlp(x, xs, gpre, gpost, w_up, w_down, *, tm, tf):
    m, ms = x.shape[0], xs.shape[0]
    c2 = lambda i, j: (0, 0)
    return pl.pallas_call(
        _mlp_kernel,
        grid=(m // tm, D_FF // tf),
        in_specs=[
            pl.BlockSpec((tm, D_MODEL), lambda i, j: (i, 0)),
            pl.BlockSpec((ms, D_MODEL), c2),
            pl.BlockSpec((1, D_MODEL), c2),
            pl.BlockSpec((1, D_MODEL), c2),
            pl.BlockSpec((D_MODEL, tf), lambda i, j: (0, j)),
            pl.BlockSpec((tf, D_MODEL), lambda i, j: (j, 0)),
        ],
        out_specs=[pl.BlockSpec((tm, D_MODEL), lambda i, j: (i, 0)), pl.BlockSpec((ms, D_MODEL), c2)],
        out_shape=[jax.ShapeDtypeStruct((m, D_MODEL), F32), jax.ShapeDtypeStruct((ms, D_MODEL), F32)],
        scratch_shapes=[pltpu.VMEM((tm, D_MODEL), BF16), pltpu.VMEM((ms, D_MODEL), BF16)],
        compiler_params=_cparams(("arbitrary", "arbitrary"), MLP_VMEM_LIMIT),
        name="mlp",
    )(x, xs, gpre, gpost, w_up, w_down)


def _group_norm_gate(y, gate, gw):
    mu = jnp.mean(y, axis=-1, keepdims=True)
    d = y - mu
    var = jnp.mean(d * d, axis=-1, keepdims=True)
    return _silu(gate) * (d * lax.rsqrt(var + GN_EPS) * gw)


def _ret_prompt_kernel(q_all, k_all, v_all, g_all, cos_ref, sin_ref, gn_ref, oa_all, st_all,
                       s_all, d_scr, e_scr, t_scr):
    b = pl.program_id(0)
    c = pl.program_id(1)
    t_i = lax.broadcasted_iota(jnp.int32, (CHUNK, CHUNK), 0)
    s_i = lax.broadcasted_iota(jnp.int32, (CHUNK, CHUNK), 1)
    lo = s_i < 64

    @pl.when((b == 0) & (c == 0))
    def _():
        tf = t_i.astype(F32)
        sf = s_i.astype(F32)
        for h in range(H_A):
            d_scr[h] = jnp.where(t_i >= s_i, jnp.exp((tf - sf) * _LOG_GAMMA[h]), 0.0)
            e_scr[h] = jnp.exp((tf + 1.0) * _LOG_GAMMA[h])
        for p in range(H_A // 2):
            lg = jnp.where(lo, _LOG_GAMMA[2 * p], _LOG_GAMMA[2 * p + 1])
            t_scr[p] = jnp.exp((CHUNK - 1.0 - tf) * lg)

    @pl.when(c == 0)
    def _():
        s_all[...] = jnp.zeros_like(s_all)

    cos, sin = cos_ref[...], sin_ref[...]
    top = t_i < 64
    for u in range(MIX_NBAT):
        v_ref, g_ref, oa_ref, s_scr = v_all.at[u], g_all.at[u], oa_all.at[u], s_all.at[u]
        qr = _rope64(q_all[u], cos, sin)
        kr = _rope64(k_all[u], cos, sin) * (DK_A ** -0.5)
        for p in range(H_A // 2):
            sl = slice(128 * p, 128 * (p + 1))
            qp, kp = qr[:, sl], kr[:, sl]
            kb = kp.astype(BF16)
            s_old = s_scr[p]
            s_old_b = s_old.astype(BF16)
            ktt = jnp.transpose(kp * t_scr[p]).astype(BF16)
            upd = []
            for e in range(2):
                h = 2 * p + e
                hs = slice(128 * h, 128 * (h + 1))
                qm = jnp.where(lo if e == 0 else ~lo, qp, 0.0).astype(BF16)
                vh = v_ref[:, hs].astype(BF16)
                att = (_dot_nt(qm, kb) * d_scr[h]).astype(BF16)
                y = _dot(att, vh) + _dot(qm, s_old_b) * e_scr[h]
                oa_ref[:, hs] = _group_norm_gate(y, g_ref[:, hs], gn_ref[:, hs]).astype(BF16)
                upd.append(_dot(ktt, vh))
            g128 = jnp.where(top, float(np.exp(CHUNK * _LOG_GAMMA[2 * p])),
                             float(np.exp(CHUNK * _LOG_GAMMA[2 * p + 1])))
            s_scr[p] = g128 * s_old + jnp.where(top, upd[0], upd[1])

    @pl.when(c == pl.num_programs(1) - 1)
    def _():
        st_all[...] = s_all[...]


def _batch_groups(a, nb, seq):
    return a.reshape(nb // MIX_NBAT, MIX_NBAT, seq, a.shape[-1])


def _ret_prompt(p_main, cos, sin, gn_w, *, nb, nc):
    seq = nc * CHUNK
    pg = _batch_groups(p_main, nb, seq)
    blk = lambda w, col: pl.BlockSpec((None, MIX_NBAT, CHUNK, w), lambda b, c: (b, 0, c, col))
    oa, st = pl.pallas_call(
        _ret_prompt_kernel,
        grid=(nb // MIX_NBAT, nc),
        in_specs=[
            blk(512, 0), blk(512, 1), blk(1024, 1), blk(1024, 2),
            pl.BlockSpec((CHUNK, 512), lambda b, c: (c, 0)),
            pl.BlockSpec((CHUNK, 512), lambda b, c: (c, 0)),
            pl.BlockSpec((1, 1024), lambda b, c: (0, 0)),
        ],
        out_specs=[
            blk(1024, 0),
            pl.BlockSpec((None, MIX_NBAT, 4, 128, 128), lambda b, c: (b, 0, 0, 0, 0)),
        ],
        out_shape=[jax.ShapeDtypeStruct((nb // MIX_NBAT, MIX_NBAT, seq, 1024), BF16),
                   jax.ShapeDtypeStruct((nb // MIX_NBAT, MIX_NBAT, 4, 128, 128), F32)],
        scratch_shapes=[pltpu.VMEM((MIX_NBAT, 4, 128, 128), F32), pltpu.VMEM((H_A, 128, 128), F32),
                        pltpu.VMEM((H_A, 128, 128), F32), pltpu.VMEM((4, 128, 128), F32)],
        compiler_params=_cparams(("arbitrary", "arbitrary")),
        name="ret_prompt",
    )(pg, pg, pg, pg, cos, sin, gn_w)
    return oa.reshape(nb * seq, 1024), st.reshape(nb, 4, 128, 128)


def _swa_padded(x2):
    lane = lax.broadcasted_iota(jnp.int32, x2.shape, 1)
    lo = lane < 64
    xr = pltpu.roll(x2, 64, axis=1)
    z = jnp.zeros_like(x2)
    return {
        (0, 0): jnp.where(lo, x2, z).astype(BF16), (0, 1): jnp.where(lo, z, xr).astype(BF16),
        (1, 0): jnp.where(lo, xr, z).astype(BF16), (1, 1): jnp.where(lo, z, x2).astype(BF16),
    }


def _swa_attend(q_pair_fn, kpad, vpad, valid, sink_ref, store_fn, rows):
    per_group = (H_B // KV_B) // 2
    for g in range(KV_B):
        pairs = list(range(g * per_group, (g + 1) * per_group))
        qs = jnp.concatenate([q_pair_fn(jj) for jj in pairs], axis=0).astype(BF16)
        acc = None
        for e in range(2):
            s = _dot_nt(qs, kpad[(g, e)])
            if valid is not None:
                s = jnp.where(valid, s, NEG)
            sink = jnp.concatenate([jnp.full((rows, 1), sink_ref[2 * jj + e], F32) for jj in pairs], axis=0)
            mx = jnp.maximum(jnp.max(s, axis=-1, keepdims=True), sink)
            pr = jnp.exp(s - mx)
            den = jnp.sum(pr, axis=-1, keepdims=True) + jnp.exp(sink - mx)
            o = _dot(pr.astype(BF16), vpad[(g, e)]) / den
            acc = o if acc is None else acc + o
        for t, jj in enumerate(pairs):
            store_fn(jj, acc[t * rows:(t + 1) * rows])


def _swa_prompt_kernel(sink_ref, q_all, kvc_all, kvp_all, cc_ref, sc_ref, cp_ref, sp_ref,
                       ob_all, ko_all, vo_all):
    n = pl.program_id(1)
    cc, sc = cc_ref[...], sc_ref[...]
    stacked = WINDOW * (H_B // KV_B) // 2
    i = lax.broadcasted_iota(jnp.int32, (stacked, 2 * WINDOW), 0) & (WINDOW - 1)
    j = lax.broadcasted_iota(jnp.int32, (stacked, 2 * WINDOW), 1)
    valid = (j >= i + 1) & (j <= i + WINDOW) & ((n > 0) | (j >= WINDOW))
    for u in range(MIX_NBAT):
        q_ref, ob_ref = q_all.at[u], ob_all.at[u]
        kvc, kvp = kvc_all[u], kvp_all[u]
        kc = _rope64(kvc[:, :128], cc, sc)
        kp = _rope64(kvp[:, :128], cp_ref[...], sp_ref[...])
        vc = kvc[:, 128:]
        kpad = _swa_padded(jnp.concatenate([kp, kc], axis=0))
        vpad = _swa_padded(jnp.concatenate([kvp[:, 128:], vc], axis=0))

        def q_pair(jj, q_ref=q_ref):
            return _rope64(q_ref[:, 128 * jj:128 * (jj + 1)], cc, sc) * (HD_B ** -0.5)

        def store(jj, acc, ob_ref=ob_ref):
            ob_ref[:, 128 * jj:128 * (jj + 1)] = acc.astype(BF16)

        _swa_attend(q_pair, kpad, vpad, valid, sink_ref, store, WINDOW)

        @pl.when(n == pl.num_programs(1) - 1)
        def _(u=u, kc=kc, vc=vc):
            ko_all[u] = kc
            vo_all[u] = vc


def _swa_prompt(p_main, p_small, cos, sin, sinks, *, nb, nc):
    seq = nc * CHUNK
    ng = nb // MIX_NBAT
    pg, psg = _batch_groups(p_main, nb, seq), _batch_groups(p_small, nb, seq)
    prev = lambda n: jnp.maximum(n - 1, 0)
    win = pl.BlockSpec((None, MIX_NBAT, WINDOW, 128), lambda b, n: (b, 0, 0, 0))
    ob, ko, vo = pl.pallas_call(
        _swa_prompt_kernel,
        grid=(ng, nc),
        in_specs=[
            pl.BlockSpec(memory_space=pltpu.SMEM),
            pl.BlockSpec((None, MIX_NBAT, CHUNK, 1024), lambda b, n: (b, 0, n, 3)),
            pl.BlockSpec((None, MIX_NBAT, CHUNK, 256), lambda b, n: (b, 0, n, 0)),
            pl.BlockSpec((None, MIX_NBAT, CHUNK, 256), lambda b, n: (b, 0, prev(n), 0)),
            pl.BlockSpec((CHUNK, 128), lambda b, n: (n, 0)),
            pl.BlockSpec((CHUNK, 128), lambda b, n: (n, 0)),
            pl.BlockSpec((CHUNK, 128), lambda b, n: (prev(n), 0)),
            pl.BlockSpec((CHUNK, 128), lambda b, n: (prev(n), 0)),
        ],
        out_specs=[pl.BlockSpec((None, MIX_NBAT, CHUNK, 1024), lambda b, n: (b, 0, n, 0)), win, win],
        out_shape=[jax.ShapeDtypeStruct((ng, MIX_NBAT, seq, 1024), BF16),
                   jax.ShapeDtypeStruct((ng, MIX_NBAT, WINDOW, 128), F32),
                   jax.ShapeDtypeStruct((ng, MIX_NBAT, WINDOW, 128), F32)],
        compiler_params=_cparams(("arbitrary", "arbitrary")),
        name="swa_prompt",
    )(sinks, pg, psg, psg, cos, sin, cos, sin)
    return ob.reshape(nb * seq, 1024), ko.reshape(nb, WINDOW, 128), vo.reshape(nb, WINDOW, 128)


def _col_bcast(row128):
    return jnp.transpose(jnp.broadcast_to(row128, (128, 128)))


def _even_sample_kernel(sink_ref, p_all, ps_all, s0_all, kbuf_all, vbuf_all, ca_ref, sa_ref, cb_ref, sb_ref,
                        gn_ref, mix_all, st_all, ko_all, vo_all):
    for u in range(SAMPLE_NSEQ):
        _even_sample_one(sink_ref, p_all.at[u], ps_all.at[u], s0_all.at[u], kbuf_all.at[u], vbuf_all.at[u],
                         ca_ref, sa_ref, cb_ref, sb_ref, gn_ref,
                         mix_all.at[u], st_all.at[u], ko_all.at[u], vo_all.at[u])


def _even_sample_one(sink_ref, p_ref, ps_ref, s0_ref, kbuf_ref, vbuf_ref, ca_ref, sa_ref, cb_ref, sb_ref,
                     gn_ref, mix_ref, st_ref, ko_ref, vo_ref):
    row = p_ref[...]
    ca, sa = ca_ref[...], sa_ref[...]
    qr = _rope64(jnp.broadcast_to(row[:, 0:512], (8, 512)), ca, sa)
    kr = _rope64(jnp.broadcast_to(row[:, 512:1024], (8, 512)), ca, sa) * (DK_A ** -0.5)
    r_i = lax.broadcasted_iota(jnp.int32, (128, 128), 0)
    top = r_i < 64
    for p in range(H_A // 2):
        sl = slice(128 * p, 128 * (p + 1))
        kcol = _col_bcast(kr[0:1, sl])
        qcol = _col_bcast(qr[0:1, sl])
        he, ho = 2 * p, 2 * p + 1
        v_e = row[:, 1024 + 128 * he:1024 + 128 * (he + 1)]
        v_o = row[:, 1024 + 128 * ho:1024 + 128 * (ho + 1)]
        gam = jnp.where(top, float(np.exp(_LOG_GAMMA[he])), float(np.exp(_LOG_GAMMA[ho])))
        s_new = gam * s0_ref[p] + kcol * jnp.where(top, v_e, v_o)
        st_ref[p] = s_new
        prod = qcol * s_new
        for e, h in ((0, he), (1, ho)):
            y = jnp.sum(jnp.where(top if e == 0 else ~top, prod, 0.0), axis=0, keepdims=True)
            hs = slice(128 * h, 128 * (h + 1))
            gate = row[:, 2048 + 128 * h:2048 + 128 * (h + 1)]
            mix_ref[:, hs] = _group_norm_gate(y, gate, gn_ref[:, hs])

    cb, sb = cb_ref[...], sb_ref[...]
    ps = ps_ref[...]
    kn = _rope64(jnp.broadcast_to(ps[:, 0:128], (8, 128)), cb, sb)[0:1]
    vn = ps[:, 128:256]
    last = r_i == WINDOW - 1
    k_new = jnp.where(last, kn, pltpu.roll(kbuf_ref[...], WINDOW - 1, axis=0))
    v_new = jnp.where(last, vn, pltpu.roll(vbuf_ref[...], WINDOW - 1, axis=0))
    ko_ref[...] = k_new
    vo_ref[...] = v_new
    kpad = _swa_padded(k_new)
    vpad = _swa_padded(v_new)

    def q_pair(jj):
        q8 = jnp.broadcast_to(row[:, 3072 + 128 * jj:3072 + 128 * (jj + 1)], (8, 128))
        return _rope64(q8, cb, sb) * (HD_B ** -0.5)

    def store(jj, acc):
        mix_ref[:, 1024 + 128 * jj:1024 + 128 * (jj + 1)] = acc[0:1]

    _swa_attend(q_pair, kpad, vpad, None, sink_ref, store, 8)


def _even_sample(p_main, p_small, s0, kbuf, vbuf, ca, sa, cb, sb, gn_w, sinks):
    nb = p_main.shape[0]
    b3 = lambda b: (b, 0, 0)
    b4 = lambda b: (b, 0, 0, 0)
    c2 = lambda b: (0, 0)
    ns = SAMPLE_NSEQ
    return pl.pallas_call(
        _even_sample_kernel,
        grid=(nb // ns,),
        in_specs=[
            pl.BlockSpec(memory_space=pltpu.SMEM),
            pl.BlockSpec((ns, 1, EVEN_MAIN), b3),
            pl.BlockSpec((ns, 1, EVEN_SMALL), b3),
            pl.BlockSpec((ns, 4, 128, 128), b4),
            pl.BlockSpec((ns, WINDOW, 128), b3),
            pl.BlockSpec((ns, WINDOW, 128), b3),
            pl.BlockSpec((1, 512), c2), pl.BlockSpec((1, 512), c2),
            pl.BlockSpec((1, 128), c2), pl.BlockSpec((1, 128), c2),
            pl.BlockSpec((1, 1024), c2),
        ],
        out_specs=[
            pl.BlockSpec((ns, 1, 2048), b3),
            pl.BlockSpec((ns, 4, 128, 128), b4),
            pl.BlockSpec((ns, WINDOW, 128), b3),
            pl.BlockSpec((ns, WINDOW, 128), b3),
        ],
        out_shape=[jax.ShapeDtypeStruct((nb, 1, 2048), F32), jax.ShapeDtypeStruct((nb, 4, 128, 128), F32),
                   jax.ShapeDtypeStruct((nb, WINDOW, 128), F32), jax.ShapeDtypeStruct((nb, WINDOW, 128), F32)],
        compiler_params=_cparams(("arbitrary",)),
        name="even_sample",
    )(sinks, p_main.reshape(nb, 1, EVEN_MAIN), p_small.reshape(nb, 1, EVEN_SMALL), s0, kbuf, vbuf,
      ca, sa, cb, sb, gn_w)


def _fox_q_tile(qi, q_ref, frow_ref, kb_scr, vb_scr, o_ref, tq):
    r_i = lax.broadcasted_iota(jnp.int32, (tq, tq), 0)
    c_i = lax.broadcasted_iota(jnp.int32, (tq, tq), 1)
    q = (q_ref[...] * (HD_C ** -0.5)).astype(BF16)
    diag = slice(qi * tq, (qi + 1) * tq)
    fq = jnp.transpose(jnp.broadcast_to(frow_ref[:, diag], (tq, tq)))[:, 0:1]
    s_d = jnp.where(c_i <= r_i, _dot_nt(q, kb_scr[diag, :]) + (fq - frow_ref[:, diag]), NEG)
    m = jnp.max(s_d, axis=-1, keepdims=True)
    if qi > 0:
        past = slice(0, qi * tq)
        s_p = _dot_nt(q, kb_scr[past, :]) + (fq - frow_ref[:, past])
        m = jnp.maximum(m, jnp.max(s_p, axis=-1, keepdims=True))
    p_d = jnp.exp(s_d - m)
    l = jnp.sum(p_d, axis=-1, keepdims=True)
    acc = _dot(p_d.astype(BF16), vb_scr[diag, :])
    if qi > 0:
        p_p = jnp.exp(s_p - m)
        l = l + jnp.sum(p_p, axis=-1, keepdims=True)
        acc = acc + _dot(p_p.astype(BF16), vb_scr[past, :])
    o_ref[...] = (acc / l).astype(BF16)


def _fox_kernel(*refs, pp, tq):
    pt_ref = refs[0]
    q_ref, kn_ref, vn_ref, lfn_ref = refs[1:5]
    k_refs = refs[5:5 + pp]
    v_refs = refs[5 + pp:5 + 2 * pp]
    lf_pool = refs[5 + 2 * pp]
    pq_ref, pk_ref, pv_ref, frow_ref = refs[6 + 2 * pp:10 + 2 * pp]
    o_ref, po_ref = refs[10 + 2 * pp:12 + 2 * pp]
    m_scr, l_scr, acc_scr, carry_scr, rexp_scr, tri_scr, kb_scr, vb_scr = refs[12 + 2 * pp:]
    b = pl.program_id(0)
    s_id = pl.program_id(1)
    rows = PAGE_SIZE * H_C

    @pl.when(s_id == 0)
    def _():
        kb_scr[...] = pk_ref[...].astype(BF16)
        vb_scr[...] = pv_ref[...].astype(BF16)

    for qi in range(kb_scr.shape[0] // tq):
        @pl.when(s_id == qi)
        def _(qi=qi):
            _fox_q_tile(qi, pq_ref, frow_ref, kb_scr, vb_scr, po_ref, tq)

    @pl.when((b == 0) & (s_id == 0))
    def _():
        r = lax.broadcasted_iota(jnp.int32, (PAGE_SIZE, rows), 0)
        c = lax.broadcasted_iota(jnp.int32, (PAGE_SIZE, rows), 1)
        rexp_scr[...] = jnp.where((c >> 3) == r, 1.0, 0.0).astype(BF16)
        r2 = lax.broadcasted_iota(jnp.int32, (PAGE_SIZE, PAGE_SIZE), 0)
        c2 = lax.broadcasted_iota(jnp.int32, (PAGE_SIZE, PAGE_SIZE), 1)
        tri_scr[...] = jnp.where(r2 <= c2, 1.0, 0.0).astype(BF16)

    @pl.when(s_id == 0)
    def _():
        m_scr[...] = jnp.full_like(m_scr, NEG)
        l_scr[...] = jnp.zeros_like(l_scr)
        acc_scr[...] = jnp.zeros_like(acc_scr)
        carry_scr[...] = jnp.zeros_like(carry_scr)

    qs = q_ref[...] * (HD_C ** -0.5)
    qb = qs.astype(BF16)
    h_i = lax.broadcasted_iota(jnp.int32, (H_C, rows), 0)
    c_i = lax.broadcasted_iota(jnp.int32, (H_C, rows), 1)
    diag = (c_i & (H_C - 1)) == h_i
    m, l, acc, carry = m_scr[...], l_scr[...], acc_scr[...], carry_scr[...]

    lf_all = jnp.concatenate([lf_pool[pt_ref[b, s_id * pp + r]] for r in range(pp)], axis=0)
    cum_in = _exact_right01(lf_all, tri_scr[...])
    offs = []
    for r in range(pp):
        offs.append(carry)
        carry = carry + cum_in[H_C * r:H_C * (r + 1), PAGE_SIZE - 1:PAGE_SIZE]
    gexp = _exact_right01(cum_in + jnp.concatenate(offs, axis=0), rexp_scr[...])

    logits = []
    for r in range(pp):
        k2t = jnp.transpose(k_refs[r][...].reshape(rows, HD_C)).astype(BF16)
        logits.append(jnp.where(diag, _dot(qb, k2t) - gexp[H_C * r:H_C * (r + 1)], NEG))
    mx = logits[0]
    for r in range(1, pp):
        mx = jnp.maximum(mx, logits[r])
    m_new = jnp.maximum(m, jnp.max(mx, axis=-1, keepdims=True))
    a = jnp.exp(m - m_new)
    psum = pv = None
    for r in range(pp):
        p = jnp.exp(logits[r] - m_new)
        o = _dot(p.astype(BF16), v_refs[r][...].reshape(rows, HD_C).astype(BF16))
        psum = p if psum is None else psum + p
        pv = o if pv is None else pv + o
    l = a * l + jnp.sum(psum, axis=-1, keepdims=True)
    acc = a * acc + pv
    m = m_new
    m_scr[...] = m
    l_scr[...] = l
    acc_scr[...] = acc
    carry_scr[...] = carry

    @pl.when(s_id == pl.num_programs(1) - 1)
    def _():
        fq = carry + lfn_ref[...]
        s_new = jnp.sum(qs * kn_ref[...], axis=-1, keepdims=True)
        m_past = m + fq
        mx = jnp.maximum(m_past, s_new)
        wp = jnp.exp(m_past - mx)
        wn = jnp.exp(s_new - mx)
        o_ref[...] = (acc * wp + wn * vn_ref[...]) / (l * wp + wn)


def _fox(page_table, q, k_new, v_new, lf_new, cache_k, cache_v, cache_lf, p_main, frow, *, pp, nb_p, seq):
    nb, n_pages = page_table.shape
    ns = n_pages // pp
    tq = seq // ns
    assert nb == nb_p * H_C and ns * tq == seq and ns * pp == n_pages
    b3 = lambda b, s, pt: (b, 0, 0)
    kv_spec = lambda r: pl.BlockSpec((None, PAGE_SIZE, H_C, HD_C), lambda b, s, pt: (pt[b, s * pp + r], 0, 0, 0))
    in_specs = [pl.BlockSpec((None, H_C, HD_C), b3), pl.BlockSpec((None, H_C, HD_C), b3),
                pl.BlockSpec((None, H_C, HD_C), b3), pl.BlockSpec((None, H_C, 1), b3)]
    in_specs += [kv_spec(r) for r in range(pp)] + [kv_spec(r) for r in range(pp)]
    in_specs += [
        pl.BlockSpec(cache_lf.shape, lambda b, s, pt: (0, 0, 0), pipeline_mode=pl.Buffered(1)),
        pl.BlockSpec((tq, HD_C), lambda b, s, pt: ((b // H_C) * ns + s, b % H_C)),
        pl.BlockSpec((seq, HD_C), lambda b, s, pt: (b // H_C, H_C + b % H_C), pipeline_mode=pl.Buffered(1)),
        pl.BlockSpec((seq, HD_C), lambda b, s, pt: (b // H_C, 2 * H_C + b % H_C), pipeline_mode=pl.Buffered(1)),
        pl.BlockSpec((None, None, 1, seq), lambda b, s, pt: (b // H_C, b % H_C, 0, 0)),
    ]
    grid_spec = pltpu.PrefetchScalarGridSpec(
        num_scalar_prefetch=1, grid=(nb, ns), in_specs=in_specs,
        out_specs=[pl.BlockSpec((None, H_C, HD_C), b3),
                   pl.BlockSpec((tq, HD_C), lambda b, s, pt: ((b // H_C) * ns + s, b % H_C))],
        scratch_shapes=[pltpu.VMEM((H_C, 1), F32), pltpu.VMEM((H_C, 1), F32), pltpu.VMEM((H_C, HD_C), F32),
                        pltpu.VMEM((H_C, 1), F32), pltpu.VMEM((PAGE_SIZE, PAGE_SIZE * H_C), BF16),
                        pltpu.VMEM((PAGE_SIZE, PAGE_SIZE), BF16),
                        pltpu.VMEM((seq, HD_C), BF16), pltpu.VMEM((seq, HD_C), BF16)],
    )
    return pl.pallas_call(
        functools.partial(_fox_kernel, pp=pp, tq=tq),
        grid_spec=grid_spec,
        out_shape=[jax.ShapeDtypeStruct((nb, H_C, HD_C), F32),
                   jax.ShapeDtypeStruct((nb_p * seq, H_C * HD_C), BF16)],
        compiler_params=_cparams(("arbitrary", "arbitrary"), MLP_VMEM_LIMIT),
        name="fox",
    )(page_table, q, k_new, v_new, lf_new, *([cache_k] * pp), *([cache_v] * pp), cache_lf,
      p_main, p_main, p_main, frow)


def _lane_col(x, lane):
    return x[:, lane:lane + 1]


def _ssd_prompt_kernel(x_all, bc_all, z_all, dt_all, cwx_ref, cwbc_ref, cbx_ref, cbbc_ref, dtb_ref, alog_ref,
                       dskip_ref, nw_ref, od_all, st_all, cv_all, xpx_all, xpbc_all, s_all, y_all):
    c = pl.program_id(1)
    nc = pl.num_programs(1)

    @pl.when(c == 0)
    def _():
        xpx_all[:, 0:8, :] = jnp.zeros((MIX_NBAT, 8, DI_D), F32)
        xpbc_all[:, 0:8, :] = jnp.zeros((MIX_NBAT, 8, 512), F32)
        s_all[...] = jnp.zeros_like(s_all)

    for u in range(MIX_NBAT):
        _ssd_chunk(x_all.at[u], bc_all.at[u], z_all.at[u], dt_all.at[u], cwx_ref, cwbc_ref, cbx_ref, cbbc_ref,
                   dtb_ref, alog_ref, dskip_ref, nw_ref, od_all.at[u], cv_all.at[u],
                   xpx_all.at[u], xpbc_all.at[u], s_all.at[u], y_all.at[u])

    @pl.when(c == nc - 1)
    def _():
        st_all[...] = s_all[...]


def _ssd_chunk(x_ref, bc_ref, z_ref, dt_ref, cwx_ref, cwbc_ref, cbx_ref, cbbc_ref, dtb_ref, alog_ref,
               dskip_ref, nw_ref, od_ref, cv_ref, xpx_scr, xpbc_scr, s_scr, y_scr):
    xpx_scr[8:8 + CHUNK, :] = x_ref[...]
    xpbc_scr[8:8 + CHUNK, :] = bc_ref[...]
    cx = cbx_ref[...]
    cbc = cbbc_ref[...]
    for k in range(CONV_W):
        w = CONV_W - 1 - k
        cx = cx + cwx_ref[w:w + 1, :] * xpx_scr[8 - k:8 - k + CHUNK, :]
        cbc = cbc + cwbc_ref[w:w + 1, :] * xpbc_scr[8 - k:8 - k + CHUNK, :]

    cv_ref[:, 0:DI_D] = xpx_scr[CHUNK + 5:CHUNK + 8, :]
    cv_ref[:, DI_D:CONV_CH] = xpbc_scr[CHUNK + 5:CHUNK + 8, :]
    xpx_scr[0:8, :] = xpx_scr[CHUNK:CHUNK + 8, :]
    xpbc_scr[0:8, :] = xpbc_scr[CHUNK:CHUNK + 8, :]

    xs = _silu(cx)
    bcs = _silu(cbc)
    dt = _softplus(dt_ref[...] + dtb_ref[...])
    la = dt * (-jnp.exp(alog_ref[...]))
    cum = _exact_left01(_tri_lower(CHUNK), la)
    cum_t = jnp.transpose(cum)
    t_i = lax.broadcasted_iota(jnp.int32, (CHUNK, CHUNK), 0)
    s_i = lax.broadcasted_iota(jnp.int32, (CHUNK, CHUNK), 1)
    causal = t_i >= s_i
    lo = s_i < 64
    att_base, bt = [], []
    for g in range(G_D):
        bg = bcs[:, 128 * g:128 * (g + 1)]
        cg = bcs[:, 256 + 128 * g:256 + 128 * (g + 1)]
        att_base.append(_dot_nt(cg.astype(BF16), bg.astype(BF16)))
        bt.append(jnp.transpose(bg).astype(BF16))
    ss = jnp.zeros((CHUNK, 1), F32)
    for p in range(H_D // 2):
        g = (2 * p) // (H_D // G_D)
        sl = slice(128 * p, 128 * (p + 1))
        le, lo_ = DT_LANE + 2 * p, DT_LANE + 2 * p + 1
        dt_pair = jnp.where(lo, _lane_col(dt, le), _lane_col(dt, lo_))
        cum_pair = jnp.where(lo, _lane_col(cum, le), _lane_col(cum, lo_))
        clast = cum_pair[CHUNK - 1:CHUNK, :]
        xs_p = xs[:, sl]
        xdt = xs_p * dt_pair
        s_old = s_scr[p]
        cgb = bcs[:, 256 + 128 * g:256 + 128 * (g + 1)].astype(BF16)
        y = _dot(cgb, s_old.astype(BF16)) * jnp.exp(cum_pair)
        for e in range(2):
            ln = DT_LANE + 2 * p + e
            diff = _lane_col(cum, ln) - cum_t[ln:ln + 1, :]
            att = (att_base[g] * jnp.exp(jnp.where(causal, diff, NEG))).astype(BF16)
            xm = jnp.where(lo if e == 0 else ~lo, xdt, 0.0).astype(BF16)
            y = y + _dot(att, xm)
        y = (y + xs_p * dskip_ref[:, sl]) * _silu(z_ref[:, sl])
        y_scr[:, sl] = y
        ss = ss + jnp.sum(y * y, axis=-1, keepdims=True)
        txdt = (xdt * jnp.exp(clast - cum_pair)).astype(BF16)
        s_scr[p] = jnp.exp(clast) * s_old + _dot(bt[g], txdt)
    inv = lax.rsqrt(ss * (1.0 / DI_D) + EPS)
    od_ref[...] = (y_scr[...] * inv * nw_ref[...]).astype(BF16)


def _ssd_prompt(p_main, p_small, cwx, cwbc, cbx, cbbc, dtb, alog, dskip, nw, *, nb, nc):
    seq = nc * CHUNK
    ng = nb // MIX_NBAT
    pg, psg = _batch_groups(p_main, nb, seq), _batch_groups(p_small, nb, seq)
    blk = lambda w, col: pl.BlockSpec((None, MIX_NBAT, CHUNK, w), lambda b, c: (b, 0, c, col))
    c2 = lambda b, c: (0, 0)
    od, st, cv = pl.pallas_call(
        _ssd_prompt_kernel,
        grid=(ng, nc),
        in_specs=[
            blk(1024, 4), blk(512, 10), blk(1024, 3), blk(ODD_SMALL, 0),
            pl.BlockSpec((CONV_W, DI_D), c2), pl.BlockSpec((CONV_W, 512), c2),
            pl.BlockSpec((1, DI_D), c2), pl.BlockSpec((1, 512), c2),
            pl.BlockSpec((1, ODD_SMALL), c2), pl.BlockSpec((1, ODD_SMALL), c2),
            pl.BlockSpec((1, DI_D), c2), pl.BlockSpec((1, DI_D), c2),
        ],
        out_specs=[
            blk(DI_D, 0),
            pl.BlockSpec((None, MIX_NBAT, 8, 128, 128), lambda b, c: (b, 0, 0, 0, 0)),
            pl.BlockSpec((None, MIX_NBAT, CONV_W - 1, CONV_CH), lambda b, c: (b, 0, 0, 0)),
        ],
        out_shape=[jax.ShapeDtypeStruct((ng, MIX_NBAT, seq, DI_D), BF16),
                   jax.ShapeDtypeStruct((ng, MIX_NBAT, 8, 128, 128), F32),
                   jax.ShapeDtypeStruct((ng, MIX_NBAT, CONV_W - 1, CONV_CH), F32)],
        scratch_shapes=[pltpu.VMEM((MIX_NBAT, CHUNK + 8, DI_D), F32), pltpu.VMEM((MIX_NBAT, CHUNK + 8, 512), F32),
                        pltpu.VMEM((MIX_NBAT, 8, 128, 128), F32), pltpu.VMEM((MIX_NBAT, CHUNK, DI_D), F32)],
        compiler_params=_cparams(("arbitrary", "arbitrary")),
        name="ssd_prompt",
    )(pg, pg, pg, psg, cwx, cwbc, cbx, cbbc, dtb, alog, dskip, nw)
    return od.reshape(nb * seq, DI_D), st.reshape(nb, 8, 128, 128), cv.reshape(nb, CONV_W - 1, CONV_CH)


def _ssd_sample_kernel(p_all, dt_all, cs_all, s0_all, cwx_ref, cwbc_ref, cbx_ref, cbbc_ref, dtb_ref, alog_ref,
                       dskip_ref, nw_ref, od_all, st_all, cv_all):
    for u in range(SAMPLE_NSEQ):
        _ssd_sample_one(p_all.at[u], dt_all.at[u], cs_all.at[u], s0_all.at[u], cwx_ref, cwbc_ref, cbx_ref,
                        cbbc_ref, dtb_ref, alog_ref, dskip_ref, nw_ref, od_all.at[u], st_all.at[u], cv_all.at[u])


def _ssd_sample_one(p_ref, dt_ref, cs_ref, s0_ref, cwx_ref, cwbc_ref, cbx_ref, cbbc_ref, dtb_ref, alog_ref,
                    dskip_ref, nw_ref, od_ref, st_ref, cv_ref):
    row = p_ref[...]
    xn = row[:, 4096:5120]
    bcn = row[:, 5120:5632]
    cx = cbx_ref[...] + cwx_ref[3:4, :] * xn
    cbc = cbbc_ref[...] + cwbc_ref[3:4, :] * bcn
    for w in range(CONV_W - 1):
        cx = cx + cwx_ref[w:w + 1, :] * cs_ref[w:w + 1, 0:DI_D]
        cbc = cbc + cwbc_ref[w:w + 1, :] * cs_ref[w:w + 1, DI_D:CONV_CH]
    cv_ref[0:1, :] = cs_ref[1:2, :]
    cv_ref[1:2, :] = cs_ref[2:3, :]
    cv_ref[2:3, 0:DI_D] = xn
    cv_ref[2:3, DI_D:CONV_CH] = bcn
    xs = _silu(cx)
    bcs = _silu(cbc)
    dt = _softplus(dt_ref[...] + dtb_ref[...])
    da = jnp.exp(dt * (-jnp.exp(alog_ref[...])))
    lane = lax.broadcasted_iota(jnp.int32, (1, 128), 1)
    lo = lane < 64
    top = lax.broadcasted_iota(jnp.int32, (128, 128), 0) < 64
    ys = []
    ss = jnp.zeros((1, 1), F32)
    for p in range(H_D // 2):
        g = (2 * p) // (H_D // G_D)
        sl = slice(128 * p, 128 * (p + 1))
        le, lo_ = DT_LANE + 2 * p, DT_LANE + 2 * p + 1
        dt_pair = jnp.where(lo, _lane_col(dt, le), _lane_col(dt, lo_))
        da_rows = jnp.where(top, _lane_col(da, le), _lane_col(da, lo_))
        xs_p = xs[:, sl]
        xcol = _col_bcast(xs_p * dt_pair)
        s_new = da_rows * s0_ref[p] + xcol * bcs[:, 128 * g:128 * (g + 1)]
        st_ref[p] = s_new
        ycol = jnp.sum(s_new * bcs[:, 256 + 128 * g:256 + 128 * (g + 1)], axis=-1, keepdims=True)
        y = jnp.transpose(jnp.broadcast_to(ycol, (128, 128)))[0:1, :]
        y = (y + xs_p * dskip_ref[:, sl]) * _silu(row[:, 3072 + 128 * p:3072 + 128 * (p + 1)])
        ys.append(y)
        ss = ss + jnp.sum(y * y, axis=-1, keepdims=True)
    inv = lax.rsqrt(ss * (1.0 / DI_D) + EPS)
    for p in range(H_D // 2):
        sl = slice(128 * p, 128 * (p + 1))
        od_ref[:, sl] = ys[p] * inv * nw_ref[:, sl]


def _ssd_sample(p_main, p_small, cs, s0, cwx, cwbc, cbx, cbbc, dtb, alog, dskip, nw):
    nb = p_main.shape[0]
    b3 = lambda b: (b, 0, 0)
    b4 = lambda b: (b, 0, 0, 0)
    c2 = lambda b: (0, 0)
    ns = SAMPLE_NSEQ
    return pl.pallas_call(
        _ssd_sample_kernel,
        grid=(nb // ns,),
        in_specs=[
            pl.BlockSpec((ns, 1, ODD_MAIN), b3),
            pl.BlockSpec((ns, 1, ODD_SMALL), b3),
            pl.BlockSpec((ns, CONV_W - 1, CONV_CH), b3),
            pl.BlockSpec((ns, 8, 128, 128), b4),
            pl.BlockSpec((CONV_W, DI_D), c2), pl.BlockSpec((CONV_W, 512), c2),
            pl.BlockSpec((1, DI_D), c2), pl.BlockSpec((1, 512), c2),
            pl.BlockSpec((1, ODD_SMALL), c2), pl.BlockSpec((1, ODD_SMALL), c2),
            pl.BlockSpec((1, DI_D), c2), pl.BlockSpec((1, DI_D), c2),
        ],
        out_specs=[
            pl.BlockSpec((ns, 1, DI_D), b3),
            pl.BlockSpec((ns, 8, 128, 128), b4),
            pl.BlockSpec((ns, CONV_W - 1, CONV_CH), b3),
        ],
        out_shape=[jax.ShapeDtypeStruct((nb, 1, DI_D), F32), jax.ShapeDtypeStruct((nb, 8, 128, 128), F32),
                   jax.ShapeDtypeStruct((nb, CONV_W - 1, CONV_CH), F32)],
        compiler_params=_cparams(("arbitrary",)),
        name="ssd_sample",
    )(p_main.reshape(nb, 1, ODD_MAIN), p_small.reshape(nb, 1, ODD_SMALL), cs, s0,
      cwx, cwbc, cbx, cbbc, dtb, alog, dskip, nw)


def _pairs_to_heads(s):
    b, p, n, _ = s.shape
    return s.reshape(b, p, n, 2, 64).transpose(0, 1, 3, 2, 4).reshape(b, 2 * p, n, 64)


def _pad_lanes(v, start, width=ODD_SMALL):
    out = jnp.zeros((1, width), F32)
    return lax.dynamic_update_slice(out, v.reshape(1, -1).astype(F32), (0, start))


def kernel(x_prompt, x_sample, state_ret, cache_swa_k, cache_swa_v, cache_fox_k, cache_fox_v, cache_fox_logf,
           state_ssm, state_conv, page_table, norm_mix_pre, norm_mix_post, norm_mlp_pre, norm_mlp_post,
           w_in_even, w_out_even, ret_norm_w, swa_sinks, w_in_odd, w_out_odd, fox_fb, conv_w, conv_b,
           dt_bias, a_log, d_skip, ssd_norm_w, w_up, w_down):
    nb, seq = BATCH, SEQ
    nc = seq // CHUNK
    mp = nb * seq
    ms = DEC_BATCH
    xp = x_prompt.reshape(mp, D_MODEL)
    xs = x_sample.reshape(ms, D_MODEL)
    row = lambda v: v.reshape(1, -1)

    pos_p = jnp.arange(seq, dtype=jnp.int32)
    pos_s = jnp.full((1,), PAST_LEN, dtype=jnp.int32)
    ca_p, sa_p = _rope_tables(pos_p, RET_THETA, 8)
    cb_p, sb_p = _rope_tables(pos_p, ROPE_THETA_B, 2)
    ca_s, sa_s = _rope_tables(pos_s, RET_THETA, 8)
    cb_s, sb_s = _rope_tables(pos_s, ROPE_THETA_B, 2)

    we = w_in_even[0]
    we_main = we[:, :EVEN_MAIN].astype(BF16)
    we_small = we[:, EVEN_MAIN:].astype(BF16)
    wo = w_out_even[0].astype(BF16)
    wo_a, wo_b = wo[:1024], wo[1024:]
    g_pre, g_post = row(norm_mix_pre[0]), row(norm_mix_post[0])
    gm_pre, gm_post = row(norm_mlp_pre[0]), row(norm_mlp_post[0])
    gn_w = row(ret_norm_w[0])
    sinks = swa_sinks[0]

    pm, psm, sm, ssm_, wu_b, wd_b = _proj(
        xp, xs, g_pre, (we_main,), we_small, [(w_up, 0, D_MODEL, D_FF), (w_down, 0, D_FF, D_MODEL)],
        odd=False, tm=256, tn=512, seq=seq)
    out_a, ret_p = _ret_prompt(pm, ca_p, sa_p, gn_w, nb=nb, nc=nc)
    out_b, swak_p, swav_p = _swa_prompt(pm, psm, cb_p, sb_p, sinks, nb=nb, nc=nc)
    mix_s, ret_s, swak_s, swav_s = _even_sample(
        sm, ssm_, state_ret[0].reshape(ms, 4, 128, 128), cache_swa_k[0].reshape(ms, WINDOW, 128),
        cache_swa_v[0].reshape(ms, WINDOW, 128), ca_s, sa_s, cb_s, sb_s, gn_w, sinks)
    mix_s = mix_s.reshape(ms, 2048).astype(BF16)
    xp, xs = _outproj(out_a, out_b, xp, mix_s[:, :1024], mix_s[:, 1024:], xs, wo_a, wo_b, g_post, tm=512)
    xp, xs = _mlp(xp, xs, gm_pre, gm_post, wu_b, wd_b, tm=512, tf=1024)

    wod = w_in_odd[0].astype(BF16)
    wod_qkv = wod[:, :3072]
    wod_zx = wod[:, 3080:5640]
    wod_small = jnp.concatenate(
        [wod[:, 3072:3080], wod[:, 5640:5656], jnp.zeros((D_MODEL, ODD_SMALL - 24), BF16)], axis=1)
    wo1 = w_out_odd[0].astype(BF16)
    wo_c, wo_d = wo1[:1024], wo1[1024:]
    g_pre, g_post = row(norm_mix_pre[1]), row(norm_mix_post[1])
    gm_pre, gm_post = row(norm_mlp_pre[1]), row(norm_mlp_post[1])
    fb = _pad_lanes(fox_fb[0], 0)
    dtb = _pad_lanes(dt_bias[0], DT_LANE)
    alog = _pad_lanes(a_log[0], DT_LANE)
    cw = conv_w[0]
    cwx, cwbc = cw[:, :DI_D], cw[:, DI_D:]
    cbx, cbbc = row(conv_b[0][:DI_D]), row(conv_b[0][DI_D:])
    dskip = row(jnp.repeat(d_skip[0], HD_D))
    nw = row(ssd_norm_w[0])

    pm, psm, sm, ssm_, wu_b, wd_b, lf_p, fc_p, lf_s, k3_p, v3_p = _proj(
        xp, xs, g_pre, (wod_qkv, wod_zx), wod_small, [(w_up, 1, D_MODEL, D_FF), (w_down, 1, D_FF, D_MODEL)],
        fb, odd=True, tm=256, tn=512, seq=seq)
    fc = fc_p[:, :H_C].reshape(nb, seq, H_C).transpose(0, 2, 1)
    out_d, ssm_pairs_p, conv_p = _ssd_prompt(pm, psm, cwx, cwbc, cbx, cbbc, dtb, alog, dskip, nw, nb=nb, nc=nc)
    fox_k_p = k3_p.reshape(1, nb, seq, H_C, HD_C)
    fox_v_p = v3_p.reshape(1, nb, seq, H_C, HD_C)
    fox_lf_p = lf_p[:, :H_C].reshape(1, nb, seq, H_C)

    q_s = sm[:, 0:1024].reshape(ms, H_C, HD_C)
    k_s = sm[:, 1024:2048].reshape(ms, H_C, HD_C)
    v_s = sm[:, 2048:3072].reshape(ms, H_C, HD_C)
    lf_s8 = lf_s[:, :H_C]
    out_c_s, out_c = _fox(page_table, q_s, k_s, v_s, lf_s8.reshape(ms, H_C, 1),
                          cache_fox_k[0], cache_fox_v[0], cache_fox_logf[0].transpose(0, 2, 1),
                          pm, fc[:, :, None, :], pp=16, nb_p=nb, seq=seq)
    ssm_t = state_ssm[0].transpose(0, 1, 3, 2).reshape(ms, H_D // 2, 2 * HD_D, N_D)
    out_d_s, ssm_ts, conv_s = _ssd_sample(sm, ssm_, state_conv[0], ssm_t,
                                               cwx, cwbc, cbx, cbbc, dtb, alog, dskip, nw)
    xp, xs = _outproj(out_c, out_d, xp, out_c_s.reshape(ms, 1024).astype(BF16),
                      out_d_s.reshape(ms, DI_D).astype(BF16), xs, wo_c, wo_d, g_post, tm=512)
    xp, xs = _mlp(xp, xs, gm_pre, gm_post, wu_b, wd_b, tm=512, tf=1024)

    return (
        xp.reshape(nb, seq, D_MODEL), xs.reshape(ms, 1, D_MODEL),
        ret_p.reshape(1, nb, H_A, DK_A, DV_A), ret_s.reshape(1, ms, H_A, DK_A, DV_A),
        swak_p.reshape(1, nb, WINDOW, KV_B, HD_B), swav_p.reshape(1, nb, WINDOW, KV_B, HD_B),
        swak_s.reshape(1, ms, WINDOW, KV_B, HD_B), swav_s.reshape(1, ms, WINDOW, KV_B, HD_B),
        fox_k_p, fox_v_p, fox_lf_p,
        k_s.reshape(1, ms, 1, H_C, HD_C), v_s.reshape(1, ms, 1, H_C, HD_C), lf_s8.reshape(1, ms, 1, H_C),
        _pairs_to_heads(ssm_pairs_p)[None], ssm_ts.reshape(ms, H_D, HD_D, N_D).transpose(0, 1, 3, 2)[None],
        conv_p[None], conv_s[None],
    )
```

```python
import functools

import numpy as np
import jax
import jax.numpy as jnp
from jax import lax
from jax.experimental import pallas as pl
from jax.experimental.pallas import tpu as pltpu

F32 = jnp.float32
BF16 = jnp.bfloat16

D_MODEL = 2048
BATCH = 4
SEQ = 2048
DEC_BATCH = 32
PAST_LEN = 16384
PAGE_SIZE = 128
D_FF = 4 * D_MODEL
EPS = 1e-6
GN_EPS = 1e-5
CHUNK = 128

H_A, DK_A, DV_A = 8, 64, 128
RET_THETA = 10000.0
H_B, KV_B, HD_B = 16, 2, 64
WINDOW = 128
ROPE_THETA_B = 150000.0
H_C, HD_C = 8, 128
H_D, HD_D, G_D, N_D = 16, 64, 2, 128
CONV_W = 4
DI_D = H_D * HD_D
CONV_CH = DI_D + 2 * G_D * N_D

EVEN_MAIN = 4096
EVEN_SMALL = 256
ODD_MAIN = 5632
ODD_SMALL = 128
DT_LANE = 8

NEG = -1e30
VMEM_LIMIT = 56 * 1024 * 1024
MLP_VMEM_LIMIT = 60 * 1024 * 1024
MLP_SUB_ROWS = 256
MIX_NBAT = 2
SAMPLE_NSEQ = 4

_LOG_GAMMA = [float(v) for v in np.log1p(-np.exp2(-5.0 - np.arange(H_A, dtype=np.float64)))]


def _cparams(sem, vmem_limit=VMEM_LIMIT):
    return pltpu.CompilerParams(dimension_semantics=sem, vmem_limit_bytes=vmem_limit)


def _silu(x):
    return x * jax.nn.sigmoid(x)


def _softplus(x):
    return jnp.maximum(x, 0.0) + jnp.log1p(jnp.exp(-jnp.abs(x)))


def _log_sigmoid(x):
    return jnp.minimum(x, 0.0) - jnp.log1p(jnp.exp(-jnp.abs(x)))


def _rms(x, g):
    ms = jnp.mean(x * x, axis=-1, keepdims=True)
    return (x * lax.rsqrt(ms + EPS)) * g


def _dot(a, b):
    return jnp.dot(a, b, preferred_element_type=F32)


def _dot_nt(a, b):
    return lax.dot_general(a, b, (((1,), (1,)), ((), ())), preferred_element_type=F32)


def _split3(x):
    hi = x.astype(BF16)
    r = x - hi.astype(F32)
    mid = r.astype(BF16)
    lo = (r - mid.astype(F32)).astype(BF16)
    return hi, mid, lo


def _exact_left01(m01, x):
    hi, mid, lo = _split3(x)
    return _dot(m01, hi) + _dot(m01, mid) + _dot(m01, lo)


def _exact_right01(x, m01):
    hi, mid, lo = _split3(x)
    return _dot(hi, m01) + _dot(mid, m01) + _dot(lo, m01)


def _tri_lower(n):
    r = lax.broadcasted_iota(jnp.int32, (n, n), 0)
    c = lax.broadcasted_iota(jnp.int32, (n, n), 1)
    return jnp.where(r >= c, 1.0, 0.0).astype(BF16)


def _rope64(x, c, s):
    w = x.shape[-1]
    ax = x.ndim - 1
    lane = lax.broadcasted_iota(jnp.int32, x.shape, ax)
    first = (lane & 32) == 0
    left = pltpu.roll(x, w - 32, axis=ax)
    right = pltpu.roll(x, 32, axis=ax)
    return x * c + jnp.where(first, left, right) * s


def _rope_tables(pos, theta, reps):
    inv = 1.0 / (theta ** (jnp.arange(32, dtype=F32) * (2.0 / 64)))
    ang = pos.astype(F32)[:, None] * inv[None, :]
    cos, sin = jnp.cos(ang), jnp.sin(ang)
    c = jnp.concatenate([cos, cos], axis=-1)
    s = jnp.concatenate([-sin, sin], axis=-1)
    return jnp.tile(c, (1, reps)), jnp.tile(s, (1, reps))


def _proj_kernel(*refs, odd, n_w, n_c, tm, tn, seq):
    x_ref, xs_ref, g_ref = refs[0:3]
    w_refs = refs[3:3 + n_w]
    wsm_ref = refs[3 + n_w]
    cast_src = refs[4 + n_w:4 + n_w + n_c]
    refs = refs[4 + n_w + n_c:]
    if odd:
        fb_ref, refs = refs[0], refs[1:]
    o_ref, osm_ref, os_ref, ossm_ref = refs[0:4]
    cast_dst = refs[4:4 + n_c]
    refs = refs[4 + n_c:]
    kv_refs = None
    if odd:
        lf_ref, fc_ref, lfs_ref, k3_ref, v3_ref, h_scr, carry_scr = refs
        kv_refs = (k3_ref, v3_ref)
    else:
        (h_scr,) = refs
    i = pl.program_id(0)
    for src, dst in zip(cast_src, cast_dst):
        dst[...] = src[...].astype(BF16)
    width = H_C * HD_C

    def rows(x_r, o_r, osm_r, n, kv=None):
        h_scr[0:n, :] = _rms(x_r[...], g_ref[...]).astype(BF16)
        sm = _dot(h_scr[0:n, :], wsm_ref[...])
        osm_r[...] = sm
        base = 0
        for w_ref in w_refs:
            for t in range(w_ref.shape[1] // tn):
                c0 = base + t * tn
                val = _dot(h_scr[0:n, :], w_ref[:, t * tn:(t + 1) * tn])
                o_r[:, c0:c0 + tn] = val
                if kv is not None and width <= c0 < 3 * width:
                    dst = kv[(c0 - width) // width]
                    h0 = ((c0 - width) % width) // HD_C
                    for hh in range(tn // HD_C):
                        dst[:, h0 + hh, :] = val[:, HD_C * hh:HD_C * (hh + 1)]
            base += w_ref.shape[1]
        return sm

    sm = rows(x_ref, o_ref, osm_ref, tm, kv_refs)
    if odd:
        lf = _log_sigmoid(sm + fb_ref[...])
        lf_ref[...] = lf

        @pl.when((i * tm) % seq == 0)
        def _():
            carry_scr[...] = jnp.zeros_like(carry_scr)

        f = _exact_left01(_tri_lower(tm), lf) + carry_scr[...]
        fc_ref[...] = f
        carry_scr[...] = f[tm - 1:tm, :]

    @pl.when(i == 0)
    def _():
        sms = rows(xs_ref, os_ref, ossm_ref, xs_ref.shape[0])
        if odd:
            lfs_ref[...] = _log_sigmoid(sms + fb_ref[...])


def _proj(x, xs, g, w_mains, w_small, casts, fb=None, *, odd, tm, tn, seq):
    m, ms = x.shape[0], xs.shape[0]
    n_main = sum(w.shape[1] for w in w_mains)
    n_small = w_small.shape[1]
    ni = m // tm
    c2 = lambda i: (0, 0)
    r2 = lambda i: (i, 0)
    in_specs = [
        pl.BlockSpec((tm, D_MODEL), r2),
        pl.BlockSpec((ms, D_MODEL), c2),
        pl.BlockSpec((1, D_MODEL), c2),
    ] + [pl.BlockSpec((D_MODEL, w.shape[1]), c2, pipeline_mode=pl.Buffered(1)) for w in w_mains] + [
        pl.BlockSpec((D_MODEL, n_small), c2),
    ]
    args = [x, xs, g, *w_mains, w_small]
    small = jax.ShapeDtypeStruct((m, n_small), F32)
    small_s = jax.ShapeDtypeStruct((ms, n_small), F32)
    small_spec = pl.BlockSpec((tm, n_small), r2)
    small_s_spec = pl.BlockSpec((ms, n_small), c2)
    out_shape = [jax.ShapeDtypeStruct((m, n_main), F32), small, jax.ShapeDtypeStruct((ms, n_main), F32), small_s]
    out_specs = [pl.BlockSpec((tm, n_main), r2), small_spec, pl.BlockSpec((ms, n_main), c2), small_s_spec]
    for src, lead, nrows, ncols in casts:
        rb = nrows // ni
        if lead is None:
            in_specs.append(pl.BlockSpec((rb, ncols), r2))
        else:
            in_specs.append(pl.BlockSpec((None, rb, ncols), lambda i, lead=lead: (lead, i, 0)))
        args.append(src)
        out_shape.append(jax.ShapeDtypeStruct((nrows, ncols), BF16))
        out_specs.append(pl.BlockSpec((rb, ncols), r2))
    scratch = [pltpu.VMEM((tm, D_MODEL), BF16)]
    if odd:
        in_specs.append(pl.BlockSpec((1, n_small), c2))
        args.append(fb)
        rows3 = jax.ShapeDtypeStruct((m, H_C, HD_C), F32)
        rows3_spec = pl.BlockSpec((tm, H_C, HD_C), lambda i: (i, 0, 0))
        out_shape += [small, small, small_s, rows3, rows3]
        out_specs += [small_spec, small_spec, small_s_spec, rows3_spec, rows3_spec]
        scratch.append(pltpu.VMEM((1, n_small), F32))
    return pl.pallas_call(
        functools.partial(_proj_kernel, odd=odd, n_w=len(w_mains), n_c=len(casts), tm=tm, tn=tn, seq=seq),
        grid=(ni,), in_specs=in_specs, out_specs=out_specs, out_shape=out_shape,
        scratch_shapes=scratch, compiler_params=_cparams(("arbitrary",), MLP_VMEM_LIMIT),
        name="proj_odd" if odd else "proj_even",
    )(*args)


def _outproj_kernel(a_ref, b_ref, x_ref, as_ref, bs_ref, xs_ref, wa_ref, wb_ref, g_ref, o_ref, os_ref):
    g = g_ref[...]
    n = x_ref.shape[0]
    sub = min(n, MLP_SUB_ROWS)
    for r in range(n // sub):
        rows = slice(r * sub, (r + 1) * sub)
        y = _dot(a_ref[rows, :], wa_ref[...]) + _dot(b_ref[rows, :], wb_ref[...])
        o_ref[rows, :] = x_ref[rows, :] + _rms(y, g)

    @pl.when(pl.program_id(0) == 0)
    def _():
        ys = _dot(as_ref[...], wa_ref[...]) + _dot(bs_ref[...], wb_ref[...])
        os_ref[...] = xs_ref[...] + _rms(ys, g)


def _outproj(a, b, x, a_s, b_s, xs, wa, wb, g, *, tm):
    m, ms = x.shape[0], xs.shape[0]
    ka, kb = a.shape[1], b.shape[1]
    c2 = lambda i: (0, 0)
    return pl.pallas_call(
        _outproj_kernel,
        grid=(m // tm,),
        in_specs=[
            pl.BlockSpec((tm, ka), lambda i: (i, 0)),
            pl.BlockSpec((tm, kb), lambda i: (i, 0)),
            pl.BlockSpec((tm, D_MODEL), lambda i: (i, 0)),
            pl.BlockSpec((ms, ka), c2),
            pl.BlockSpec((ms, kb), c2),
            pl.BlockSpec((ms, D_MODEL), c2),
            pl.BlockSpec((ka, D_MODEL), c2),
            pl.BlockSpec((kb, D_MODEL), c2),
            pl.BlockSpec((1, D_MODEL), c2),
        ],
        out_specs=[pl.BlockSpec((tm, D_MODEL), lambda i: (i, 0)), pl.BlockSpec((ms, D_MODEL), c2)],
        out_shape=[jax.ShapeDtypeStruct((m, D_MODEL), F32), jax.ShapeDtypeStruct((ms, D_MODEL), F32)],
        compiler_params=_cparams(("arbitrary",)),
        name="outproj",
    )(a, b, x, a_s, b_s, xs, wa, wb, g)


def _mlp_kernel(x_ref, xs_ref, gpre_ref, gpost_ref, wu_ref, wd_ref, o_ref, os_ref, h_scr, hs_scr):
    i = pl.program_id(0)
    j = pl.program_id(1)
    last = pl.num_programs(1) - 1

    def group(x_r, o_r, h_s):
        @pl.when(j == 0)
        def _():
            h_s[...] = _rms(x_r[...], gpre_ref[...]).astype(BF16)
            o_r[...] = jnp.zeros_like(o_r)

        n = x_r.shape[0]
        sub = min(n, MLP_SUB_ROWS)
        for r in range(n // sub):
            rows = slice(r * sub, (r + 1) * sub)
            u = jnp.maximum(_dot(h_s[rows, :], wu_ref[...]), 0.0)
            o_r[rows, :] += _dot((u * u).astype(BF16), wd_ref[...])

        @pl.when(j == last)
        def _():
            o_r[...] = x_r[...] + _rms(o_r[...], gpost_ref[...])

    group(x_ref, o_ref, h_scr)

    @pl.when(i == 0)
    def _():
        group(xs_ref, os_ref, hs_scr)


def _mlp(x, xs, gpre, gpost, w_up, w_down, *, tm, tf):
    m, ms = x.shape[0], xs.shape[0]
    c2 = lambda i, j: (0, 0)
    return pl.pallas_call(
        _mlp_kernel,
        grid=(m // tm, D_FF // tf),
        in_specs=[
            pl.BlockSpec((tm, D_MODEL), lambda i, j: (i, 0)),
            pl.BlockSpec((ms, D_MODEL), c2),
            pl.BlockSpec((1, D_MODEL), c2),
            pl.BlockSpec((1, D_MODEL), c2),
            pl.BlockSpec((D_MODEL, tf), lambda i, j: (0, j)),
            pl.BlockSpec((tf, D_MODEL), lambda i, j: (j, 0)),
        ],
        out_specs=[pl.BlockSpec((tm, D_MODEL), lambda i, j: (i, 0)), pl.BlockSpec((ms, D_MODEL), c2)],
        out_shape=[jax.ShapeDtypeStruct((m, D_MODEL), F32), jax.ShapeDtypeStruct((ms, D_MODEL), F32)],
        scratch_shapes=[pltpu.VMEM((tm, D_MODEL), BF16), pltpu.VMEM((ms, D_MODEL), BF16)],
        compiler_params=_cparams(("arbitrary", "arbitrary"), MLP_VMEM_LIMIT),
        name="mlp",
    )(x, xs, gpre, gpost, w_up, w_down)


def _group_norm_gate(y, gate, gw):
    mu = jnp.mean(y, axis=-1, keepdims=True)
    d = y - mu
    var = jnp.mean(d * d, axis=-1, keepdims=True)
    return _silu(gate) * (d * lax.rsqrt(var + GN_EPS) * gw)


def _ret_prompt_kernel(q_all, k_all, v_all, g_all, cos_ref, sin_ref, gn_ref, oa_all, st_all,
                       s_all, d_scr, e_scr, t_scr):
    b = pl.program_id(0)
    c = pl.program_id(1)
    t_i = lax.broadcasted_iota(jnp.int32, (CHUNK, CHUNK), 0)
    s_i = lax.broadcasted_iota(jnp.int32, (CHUNK, CHUNK), 1)
    lo = s_i < 64

    @pl.when((b == 0) & (c == 0))
    def _():
        tf = t_i.astype(F32)
        sf = s_i.astype(F32)
        for h in range(H_A):
            d_scr[h] = jnp.where(t_i >= s_i, jnp.exp((tf - sf) * _LOG_GAMMA[h]), 0.0)
            e_scr[h] = jnp.exp((tf + 1.0) * _LOG_GAMMA[h])
        for p in range(H_A // 2):
            lg = jnp.where(lo, _LOG_GAMMA[2 * p], _LOG_GAMMA[2 * p + 1])
            t_scr[p] = jnp.exp((CHUNK - 1.0 - tf) * lg)

    @pl.when(c == 0)
    def _():
        s_all[...] = jnp.zeros_like(s_all)

    cos, sin = cos_ref[...], sin_ref[...]
    top = t_i < 64
    for u in range(MIX_NBAT):
        v_ref, g_ref, oa_ref, s_scr = v_all.at[u], g_all.at[u], oa_all.at[u], s_all.at[u]
        qr = _rope64(q_all[u], cos, sin)
        kr = _rope64(k_all[u], cos, sin) * (DK_A ** -0.5)
        for p in range(H_A // 2):
            sl = slice(128 * p, 128 * (p + 1))
            qp, kp = qr[:, sl], kr[:, sl]
            kb = kp.astype(BF16)
            s_old = s_scr[p]
            s_old_b = s_old.astype(BF16)
            ktt = jnp.transpose(kp * t_scr[p]).astype(BF16)
            upd = []
            for e in range(2):
                h = 2 * p + e
                hs = slice(128 * h, 128 * (h + 1))
                qm = jnp.where(lo if e == 0 else ~lo, qp, 0.0).astype(BF16)
                vh = v_ref[:, hs].astype(BF16)
                att = (_dot_nt(qm, kb) * d_scr[h]).astype(BF16)
                y = _dot(att, vh) + _dot(qm, s_old_b) * e_scr[h]
                oa_ref[:, hs] = _group_norm_gate(y, g_ref[:, hs], gn_ref[:, hs]).astype(BF16)
                upd.append(_dot(ktt, vh))
            g128 = jnp.where(top, float(np.exp(CHUNK * _LOG_GAMMA[2 * p])),
                             float(np.exp(CHUNK * _LOG_GAMMA[2 * p + 1])))
            s_scr[p] = g128 * s_old + jnp.where(top, upd[0], upd[1])

    @pl.when(c == pl.num_programs(1) - 1)
    def _():
        st_all[...] = s_all[...]


def _batch_groups(a, nb, seq):
    return a.reshape(nb // MIX_NBAT, MIX_NBAT, seq, a.shape[-1])


def _ret_prompt(p_main, cos, sin, gn_w, *, nb, nc):
    seq = nc * CHUNK
    pg = _batch_groups(p_main, nb, seq)
    blk = lambda w, col: pl.BlockSpec((None, MIX_NBAT, CHUNK, w), lambda b, c: (b, 0, c, col))
    oa, st = pl.pallas_call(
        _ret_prompt_kernel,
        grid=(nb // MIX_NBAT, nc),
        in_specs=[
            blk(512, 0), blk(512, 1), blk(1024, 1), blk(1024, 2),
            pl.BlockSpec((CHUNK, 512), lambda b, c: (c, 0)),
            pl.BlockSpec((CHUNK, 512), lambda b, c: (c, 0)),
            pl.BlockSpec((1, 1024), lambda b, c: (0, 0)),
        ],
        out_specs=[
            blk(1024, 0),
            pl.BlockSpec((None, MIX_NBAT, 4, 128, 128), lambda b, c: (b, 0, 0, 0, 0)),
        ],
        out_shape=[jax.ShapeDtypeStruct((nb // MIX_NBAT, MIX_NBAT, seq, 1024), BF16),
                   jax.ShapeDtypeStruct((nb // MIX_NBAT, MIX_NBAT, 4, 128, 128), F32)],
        scratch_shapes=[pltpu.VMEM((MIX_NBAT, 4, 128, 128), F32), pltpu.VMEM((H_A, 128, 128), F32),
                        pltpu.VMEM((H_A, 128, 128), F32), pltpu.VMEM((4, 128, 128), F32)],
        compiler_params=_cparams(("arbitrary", "arbitrary")),
        name="ret_prompt",
    )(pg, pg, pg, pg, cos, sin, gn_w)
    return oa.reshape(nb * seq, 1024), st.reshape(nb, 4, 128, 128)


def _swa_padded(x2):
    lane = lax.broadcasted_iota(jnp.int32, x2.shape, 1)
    lo = lane < 64
    xr = pltpu.roll(x2, 64, axis=1)
    z = jnp.zeros_like(x2)
    return {
        (0, 0): jnp.where(lo, x2, z).astype(BF16), (0, 1): jnp.where(lo, z, xr).astype(BF16),
        (1, 0): jnp.where(lo, xr, z).astype(BF16), (1, 1): jnp.where(lo, z, x2).astype(BF16),
    }


def _swa_attend(q_pair_fn, kpad, vpad, valid, sink_ref, store_fn, rows):
    per_group = (H_B // KV_B) // 2
    for g in range(KV_B):
        pairs = list(range(g * per_group, (g + 1) * per_group))
        qs = jnp.concatenate([q_pair_fn(jj) for jj in pairs], axis=0).astype(BF16)
        acc = None
        for e in range(2):
            s = _dot_nt(qs, kpad[(g, e)])
            if valid is not None:
                s = jnp.where(valid, s, NEG)
            sink = jnp.concatenate([jnp.full((rows, 1), sink_ref[2 * jj + e], F32) for jj in pairs], axis=0)
            mx = jnp.maximum(jnp.max(s, axis=-1, keepdims=True), sink)
            pr = jnp.exp(s - mx)
            den = jnp.sum(pr, axis=-1, keepdims=True) + jnp.exp(sink - mx)
            o = _dot(pr.astype(BF16), vpad[(g, e)]) / den
            acc = o if acc is None else acc + o
        for t, jj in enumerate(pairs):
            store_fn(jj, acc[t * rows:(t + 1) * rows])


def _swa_prompt_kernel(sink_ref, q_all, kvc_all, kvp_all, cc_ref, sc_ref, cp_ref, sp_ref,
                       ob_all, ko_all, vo_all):
    n = pl.program_id(1)
    cc, sc = cc_ref[...], sc_ref[...]
    stacked = WINDOW * (H_B // KV_B) // 2
    i = lax.broadcasted_iota(jnp.int32, (stacked, 2 * WINDOW), 0) & (WINDOW - 1)
    j = lax.broadcasted_iota(jnp.int32, (stacked, 2 * WINDOW), 1)
    valid = (j >= i + 1) & (j <= i + WINDOW) & ((n > 0) | (j >= WINDOW))
    for u in range(MIX_NBAT):
        q_ref, ob_ref = q_all.at[u], ob_all.at[u]
        kvc, kvp = kvc_all[u], kvp_all[u]
        kc = _rope64(kvc[:, :128], cc, sc)
        kp = _rope64(kvp[:, :128], cp_ref[...], sp_ref[...])
        vc = kvc[:, 128:]
        kpad = _swa_padded(jnp.concatenate([kp, kc], axis=0))
        vpad = _swa_padded(jnp.concatenate([kvp[:, 128:], vc], axis=0))

        def q_pair(jj, q_ref=q_ref):
            return _rope64(q_ref[:, 128 * jj:128 * (jj + 1)], cc, sc) * (HD_B ** -0.5)

        def store(jj, acc, ob_ref=ob_ref):
            ob_ref[:, 128 * jj:128 * (jj + 1)] = acc.astype(BF16)

        _swa_attend(q_pair, kpad, vpad, valid, sink_ref, store, WINDOW)

        @pl.when(n == pl.num_programs(1) - 1)
        def _(u=u, kc=kc, vc=vc):
            ko_all[u] = kc
            vo_all[u] = vc


def _swa_prompt(p_main, p_small, cos, sin, sinks, *, nb, nc):
    seq = nc * CHUNK
    ng = nb // MIX_NBAT
    pg, psg = _batch_groups(p_main, nb, seq), _batch_groups(p_small, nb, seq)
    prev = lambda n: jnp.maximum(n - 1, 0)
    win = pl.BlockSpec((None, MIX_NBAT, WINDOW, 128), lambda b, n: (b, 0, 0, 0))
    ob, ko, vo = pl.pallas_call(
        _swa_prompt_kernel,
        grid=(ng, nc),
        in_specs=[
            pl.BlockSpec(memory_space=pltpu.SMEM),
            pl.BlockSpec((None, MIX_NBAT, CHUNK, 1024), lambda b, n: (b, 0, n, 3)),
            pl.BlockSpec((None, MIX_NBAT, CHUNK, 256), lambda b, n: (b, 0, n, 0)),
            pl.BlockSpec((None, MIX_NBAT, CHUNK, 256), lambda b, n: (b, 0, prev(n), 0)),
            pl.BlockSpec((CHUNK, 128), lambda b, n: (n, 0)),
            pl.BlockSpec((CHUNK, 128), lambda b, n: (n, 0)),
            pl.BlockSpec((CHUNK, 128), lambda b, n: (prev(n), 0)),
            pl.BlockSpec((CHUNK, 128), lambda b, n: (prev(n), 0)),
        ],
        out_specs=[pl.BlockSpec((None, MIX_NBAT, CHUNK, 1024), lambda b, n: (b, 0, n, 0)), win, win],
        out_shape=[jax.ShapeDtypeStruct((ng, MIX_NBAT, seq, 1024), BF16),
                   jax.ShapeDtypeStruct((ng, MIX_NBAT, WINDOW, 128), F32),
                   jax.ShapeDtypeStruct((ng, MIX_NBAT, WINDOW, 128), F32)],
        compiler_params=_cparams(("arbitrary", "arbitrary")),
        name="swa_prompt",
    )(sinks, pg, psg, psg, cos, sin, cos, sin)
    return ob.reshape(nb * seq, 1024), ko.reshape(nb, WINDOW, 128), vo.reshape(nb, WINDOW, 128)


def _col_bcast(row128):
    return jnp.transpose(jnp.broadcast_to(row128, (128, 128)))


def _even_sample_kernel(sink_ref, p_all, ps_all, s0_all, kbuf_all, vbuf_all, ca_ref, sa_ref, cb_ref, sb_ref,
                        gn_ref, mix_all, st_all, ko_all, vo_all):
    for u in range(SAMPLE_NSEQ):
        _even_sample_one(sink_ref, p_all.at[u], ps_all.at[u], s0_all.at[u], kbuf_all.at[u], vbuf_all.at[u],
                         ca_ref, sa_ref, cb_ref, sb_ref, gn_ref,
                         mix_all.at[u], st_all.at[u], ko_all.at[u], vo_all.at[u])


def _even_sample_one(sink_ref, p_ref, ps_ref, s0_ref, kbuf_ref, vbuf_ref, ca_ref, sa_ref, cb_ref, sb_ref,
                     gn_ref, mix_ref, st_ref, ko_ref, vo_ref):
    row = p_ref[...]
    ca, sa = ca_ref[...], sa_ref[...]
    qr = _rope64(jnp.broadcast_to(row[:, 0:512], (8, 512)), ca, sa)
    kr = _rope64(jnp.broadcast_to(row[:, 512:1024], (8, 512)), ca, sa) * (DK_A ** -0.5)
    r_i = lax.broadcasted_iota(jnp.int32, (128, 128), 0)
    top = r_i < 64
    for p in range(H_A // 2):
        sl = slice(128 * p, 128 * (p + 1))
        kcol = _col_bcast(kr[0:1, sl])
        qcol = _col_bcast(qr[0:1, sl])
        he, ho = 2 * p, 2 * p + 1
        v_e = row[:, 1024 + 128 * he:1024 + 128 * (he + 1)]
        v_o = row[:, 1024 + 128 * ho:1024 + 128 * (ho + 1)]
        gam = jnp.where(top, float(np.exp(_LOG_GAMMA[he])), float(np.exp(_LOG_GAMMA[ho])))
        s_new = gam * s0_ref[p] + kcol * jnp.where(top, v_e, v_o)
        st_ref[p] = s_new
        prod = qcol * s_new
        for e, h in ((0, he), (1, ho)):
            y = jnp.sum(jnp.where(top if e == 0 else ~top, prod, 0.0), axis=0, keepdims=True)
            hs = slice(128 * h, 128 * (h + 1))
            gate = row[:, 2048 + 128 * h:2048 + 128 * (h + 1)]
            mix_ref[:, hs] = _group_norm_gate(y, gate, gn_ref[:, hs])

    cb, sb = cb_ref[...], sb_ref[...]
    ps = ps_ref[...]
    kn = _rope64(jnp.broadcast_to(ps[:, 0:128], (8, 128)), cb, sb)[0:1]
    vn = ps[:, 128:256]
    last = r_i == WINDOW - 1
    k_new = jnp.where(last, kn, pltpu.roll(kbuf_ref[...], WINDOW - 1, axis=0))
    v_new = jnp.where(last, vn, pltpu.roll(vbuf_ref[...], WINDOW - 1, axis=0))
    ko_ref[...] = k_new
    vo_ref[...] = v_new
    kpad = _swa_padded(k_new)
    vpad = _swa_padded(v_new)

    def q_pair(jj):
        q8 = jnp.broadcast_to(row[:, 3072 + 128 * jj:3072 + 128 * (jj + 1)], (8, 128))
        return _rope64(q8, cb, sb) * (HD_B ** -0.5)

    def store(jj, acc):
        mix_ref[:, 1024 + 128 * jj:1024 + 128 * (jj + 1)] = acc[0:1]

    _swa_attend(q_pair, kpad, vpad, None, sink_ref, store, 8)


def _even_sample(p_main, p_small, s0, kbuf, vbuf, ca, sa, cb, sb, gn_w, sinks):
    nb = p_main.shape[0]
    b3 = lambda b: (b, 0, 0)
    b4 = lambda b: (b, 0, 0, 0)
    c2 = lambda b: (0, 0)
    ns = SAMPLE_NSEQ
    return pl.pallas_call(
        _even_sample_kernel,
        grid=(nb // ns,),
        in_specs=[
            pl.BlockSpec(memory_space=pltpu.SMEM),
            pl.BlockSpec((ns, 1, EVEN_MAIN), b3),
            pl.BlockSpec((ns, 1, EVEN_SMALL), b3),
            pl.BlockSpec((ns, 4, 128, 128), b4),
            pl.BlockSpec((ns, WINDOW, 128), b3),
            pl.BlockSpec((ns, WINDOW, 128), b3),
            pl.BlockSpec((1, 512), c2), pl.BlockSpec((1, 512), c2),
            pl.BlockSpec((1, 128), c2), pl.BlockSpec((1, 128), c2),
            pl.BlockSpec((1, 1024), c2),
        ],
        out_specs=[
            pl.BlockSpec((ns, 1, 2048), b3),
            pl.BlockSpec((ns, 4, 128, 128), b4),
            pl.BlockSpec((ns, WINDOW, 128), b3),
            pl.BlockSpec((ns, WINDOW, 128), b3),
        ],
        out_shape=[jax.ShapeDtypeStruct((nb, 1, 2048), F32), jax.ShapeDtypeStruct((nb, 4, 128, 128), F32),
                   jax.ShapeDtypeStruct((nb, WINDOW, 128), F32), jax.ShapeDtypeStruct((nb, WINDOW, 128), F32)],
        compiler_params=_cparams(("arbitrary",)),
        name="even_sample",
    )(sinks, p_main.reshape(nb, 1, EVEN_MAIN), p_small.reshape(nb, 1, EVEN_SMALL), s0, kbuf, vbuf,
      ca, sa, cb, sb, gn_w)


def _fox_q_tile(qi, q_ref, frow_ref, kb_scr, vb_scr, o_ref, tq):
    r_i = lax.broadcasted_iota(jnp.int32, (tq, tq), 0)
    c_i = lax.broadcasted_iota(jnp.int32, (tq, tq), 1)
    q = (q_ref[...] * (HD_C ** -0.5)).astype(BF16)
    diag = slice(qi * tq, (qi + 1) * tq)
    fq = jnp.transpose(jnp.broadcast_to(frow_ref[:, diag], (tq, tq)))[:, 0:1]
    s_d = jnp.where(c_i <= r_i, _dot_nt(q, kb_scr[diag, :]) + (fq - frow_ref[:, diag]), NEG)
    m = jnp.max(s_d, axis=-1, keepdims=True)
    if qi > 0:
        past = slice(0, qi * tq)
        s_p = _dot_nt(q, kb_scr[past, :]) + (fq - frow_ref[:, past])
        m = jnp.maximum(m, jnp.max(s_p, axis=-1, keepdims=True))
    p_d = jnp.exp(s_d - m)
    l = jnp.sum(p_d, axis=-1, keepdims=True)
    acc = _dot(p_d.astype(BF16), vb_scr[diag, :])
    if qi > 0:
        p_p = jnp.exp(s_p - m)
        l = l + jnp.sum(p_p, axis=-1, keepdims=True)
        acc = acc + _dot(p_p.astype(BF16), vb_scr[past, :])
    o_ref[...] = (acc / l).astype(BF16)


def _fox_kernel(*refs, pp, tq):
    pt_ref = refs[0]
    q_ref, kn_ref, vn_ref, lfn_ref = refs[1:5]
    k_refs = refs[5:5 + pp]
    v_refs = refs[5 + pp:5 + 2 * pp]
    lf_pool = refs[5 + 2 * pp]
    pq_ref, pk_ref, pv_ref, frow_ref = refs[6 + 2 * pp:10 + 2 * pp]
    o_ref, po_ref = refs[10 + 2 * pp:12 + 2 * pp]
    m_scr, l_scr, acc_scr, carry_scr, rexp_scr, tri_scr, kb_scr, vb_scr = refs[12 + 2 * pp:]
    b = pl.program_id(0)
    s_id = pl.program_id(1)
    rows = PAGE_SIZE * H_C

    @pl.when(s_id == 0)
    def _():
        kb_scr[...] = pk_ref[...].astype(BF16)
        vb_scr[...] = pv_ref[...].astype(BF16)

    for qi in range(kb_scr.shape[0] // tq):
        @pl.when(s_id == qi)
        def _(qi=qi):
            _fox_q_tile(qi, pq_ref, frow_ref, kb_scr, vb_scr, po_ref, tq)

    @pl.when((b == 0) & (s_id == 0))
    def _():
        r = lax.broadcasted_iota(jnp.int32, (PAGE_SIZE, rows), 0)
        c = lax.broadcasted_iota(jnp.int32, (PAGE_SIZE, rows), 1)
        rexp_scr[...] = jnp.where((c >> 3) == r, 1.0, 0.0).astype(BF16)
        r2 = lax.broadcasted_iota(jnp.int32, (PAGE_SIZE, PAGE_SIZE), 0)
        c2 = lax.broadcasted_iota(jnp.int32, (PAGE_SIZE, PAGE_SIZE), 1)
        tri_scr[...] = jnp.where(r2 <= c2, 1.0, 0.0).astype(BF16)

    @pl.when(s_id == 0)
    def _():
        m_scr[...] = jnp.full_like(m_scr, NEG)
        l_scr[...] = jnp.zeros_like(l_scr)
        acc_scr[...] = jnp.zeros_like(acc_scr)
        carry_scr[...] = jnp.zeros_like(carry_scr)

    qs = q_ref[...] * (HD_C ** -0.5)
    qb = qs.astype(BF16)
    h_i = lax.broadcasted_iota(jnp.int32, (H_C, rows), 0)
    c_i = lax.broadcasted_iota(jnp.int32, (H_C, rows), 1)
    diag = (c_i & (H_C - 1)) == h_i
    m, l, acc, carry = m_scr[...], l_scr[...], acc_scr[...], carry_scr[...]

    lf_all = jnp.concatenate([lf_pool[pt_ref[b, s_id * pp + r]] for r in range(pp)], axis=0)
    cum_in = _exact_right01(lf_all, tri_scr[...])
    offs = []
    for r in range(pp):
        offs.append(carry)
        carry = carry + cum_in[H_C * r:H_C * (r + 1), PAGE_SIZE - 1:PAGE_SIZE]
    gexp = _exact_right01(cum_in + jnp.concatenate(offs, axis=0), rexp_scr[...])

    logits = []
    for r in range(pp):
        k2t = jnp.transpose(k_refs[r][...].reshape(rows, HD_C)).astype(BF16)
        logits.append(jnp.where(diag, _dot(qb, k2t) - gexp[H_C * r:H_C * (r + 1)], NEG))
    mx = logits[0]
    for r in range(1, pp):
        mx = jnp.maximum(mx, logits[r])
    m_new = jnp.maximum(m, jnp.max(mx, axis=-1, keepdims=True))
    a = jnp.exp(m - m_new)
    psum = pv = None
    for r in range(pp):
        p = jnp.exp(logits[r] - m_new)
        o = _dot(p.astype(BF16), v_refs[r][...].reshape(rows, HD_C).astype(BF16))
        psum = p if psum is None else psum + p
        pv = o if pv is None else pv + o
    l = a * l + jnp.sum(psum, axis=-1, keepdims=True)
    acc = a * acc + pv
    m = m_new
    m_scr[...] = m
    l_scr[...] = l
    acc_scr[...] = acc
    carry_scr[...] = carry

    @pl.when(s_id == pl.num_programs(1) - 1)
    def _():
        fq = carry + lfn_ref[...]
        s_new = jnp.sum(qs * kn_ref[...], axis=-1, keepdims=True)
        m_past = m + fq
        mx = jnp.maximum(m_past, s_new)
        wp = jnp.exp(m_past - mx)
        wn = jnp.exp(s_new - mx)
        o_ref[...] = (acc * wp + wn * vn_ref[...]) / (l * wp + wn)


def _fox(page_table, q, k_new, v_new, lf_new, cache_k, cache_v, cache_lf, p_main, frow, *, pp, nb_p, seq):
    nb, n_pages = page_table.shape
    ns = n_pages // pp
    tq = seq // ns
    assert nb == nb_p * H_C and ns * tq == seq and ns * pp == n_pages
    b3 = lambda b, s, pt: (b, 0, 0)
    kv_spec = lambda r: pl.BlockSpec((None, PAGE_SIZE, H_C, HD_C), lambda b, s, pt: (pt[b, s * pp + r], 0, 0, 0))
    in_specs = [pl.BlockSpec((None, H_C, HD_C), b3), pl.BlockSpec((None, H_C, HD_C), b3),
                pl.BlockSpec((None, H_C, HD_C), b3), pl.BlockSpec((None, H_C, 1), b3)]
    in_specs += [kv_spec(r) for r in range(pp)] + [kv_spec(r) for r in range(pp)]
    in_specs += [
        pl.BlockSpec(cache_lf.shape, lambda b, s, pt: (0, 0, 0), pipeline_mode=pl.Buffered(1)),
        pl.BlockSpec((tq, HD_C), lambda b, s, pt: ((b // H_C) * ns + s, b % H_C)),
        pl.BlockSpec((seq, HD_C), lambda b, s, pt: (b // H_C, H_C + b % H_C), pipeline_mode=pl.Buffered(1)),
        pl.BlockSpec((seq, HD_C), lambda b, s, pt: (b // H_C, 2 * H_C + b % H_C), pipeline_mode=pl.Buffered(1)),
        pl.BlockSpec((None, None, 1, seq), lambda b, s, pt: (b // H_C, b % H_C, 0, 0)),
    ]
    grid_spec = pltpu.PrefetchScalarGridSpec(
        num_scalar_prefetch=1, grid=(nb, ns), in_specs=in_specs,
        out_specs=[pl.BlockSpec((None, H_C, HD_C), b3),
                   pl.BlockSpec((tq, HD_C), lambda b, s, pt: ((b // H_C) * ns + s, b % H_C))],
        scratch_shapes=[pltpu.VMEM((H_C, 1), F32), pltpu.VMEM((H_C, 1), F32), pltpu.VMEM((H_C, HD_C), F32),
                        pltpu.VMEM((H_C, 1), F32), pltpu.VMEM((PAGE_SIZE, PAGE_SIZE * H_C), BF16),
                        pltpu.VMEM((PAGE_SIZE, PAGE_SIZE), BF16),
                        pltpu.VMEM((seq, HD_C), BF16), pltpu.VMEM((seq, HD_C), BF16)],
    )
    return pl.pallas_call(
        functools.partial(_fox_kernel, pp=pp, tq=tq),
        grid_spec=grid_spec,
        out_shape=[jax.ShapeDtypeStruct((nb, H_C, HD_C), F32),
                   jax.ShapeDtypeStruct((nb_p * seq, H_C * HD_C), BF16)],
        compiler_params=_cparams(("arbitrary", "arbitrary"), MLP_VMEM_LIMIT),
        name="fox",
    )(page_table, q, k_new, v_new, lf_new, *([cache_k] * pp), *([cache_v] * pp), cache_lf,
      p_main, p_main, p_main, frow)


def _lane_col(x, lane):
    return x[:, lane:lane + 1]


def _ssd_prompt_kernel(x_all, bc_all, z_all, dt_all, cwx_ref, cwbc_ref, cbx_ref, cbbc_ref, dtb_ref, alog_ref,
                       dskip_ref, nw_ref, od_all, st_all, cv_all, xpx_all, xpbc_all, s_all, y_all):
    c = pl.program_id(1)
    nc = pl.num_programs(1)

    @pl.when(c == 0)
    def _():
        xpx_all[:, 0:8, :] = jnp.zeros((MIX_NBAT, 8, DI_D), F32)
        xpbc_all[:, 0:8, :] = jnp.zeros((MIX_NBAT, 8, 512), F32)
        s_all[...] = jnp.zeros_like(s_all)

    for u in range(MIX_NBAT):
        _ssd_chunk(x_all.at[u], bc_all.at[u], z_all.at[u], dt_all.at[u], cwx_ref, cwbc_ref, cbx_ref, cbbc_ref,
                   dtb_ref, alog_ref, dskip_ref, nw_ref, od_all.at[u], cv_all.at[u],
                   xpx_all.at[u], xpbc_all.at[u], s_all.at[u], y_all.at[u])

    @pl.when(c == nc - 1)
    def _():
        st_all[...] = s_all[...]


def _ssd_chunk(x_ref, bc_ref, z_ref, dt_ref, cwx_ref, cwbc_ref, cbx_ref, cbbc_ref, dtb_ref, alog_ref,
               dskip_ref, nw_ref, od_ref, cv_ref, xpx_scr, xpbc_scr, s_scr, y_scr):
    xpx_scr[8:8 + CHUNK, :] = x_ref[...]
    xpbc_scr[8:8 + CHUNK, :] = bc_ref[...]
    cx = cbx_ref[...]
    cbc = cbbc_ref[...]
    for k in range(CONV_W):
        w = CONV_W - 1 - k
        cx = cx + cwx_ref[w:w + 1, :] * xpx_scr[8 - k:8 - k + CHUNK, :]
        cbc = cbc + cwbc_ref[w:w + 1, :] * xpbc_scr[8 - k:8 - k + CHUNK, :]

    cv_ref[:, 0:DI_D] = xpx_scr[CHUNK + 5:CHUNK + 8, :]
    cv_ref[:, DI_D:CONV_CH] = xpbc_scr[CHUNK + 5:CHUNK + 8, :]
    xpx_scr[0:8, :] = xpx_scr[CHUNK:CHUNK + 8, :]
    xpbc_scr[0:8, :] = xpbc_scr[CHUNK:CHUNK + 8, :]

    xs = _silu(cx)
    bcs = _silu(cbc)
    dt = _softplus(dt_ref[...] + dtb_ref[...])
    la = dt * (-jnp.exp(alog_ref[...]))
    cum = _exact_left01(_tri_lower(CHUNK), la)
    cum_t = jnp.transpose(cum)
    t_i = lax.broadcasted_iota(jnp.int32, (CHUNK, CHUNK), 0)
    s_i = lax.broadcasted_iota(jnp.int32, (CHUNK, CHUNK), 1)
    causal = t_i >= s_i
    lo = s_i < 64
    att_base, bt = [], []
    for g in range(G_D):
        bg = bcs[:, 128 * g:128 * (g + 1)]
        cg = bcs[:, 256 + 128 * g:256 + 128 * (g + 1)]
        att_base.append(_dot_nt(cg.astype(BF16), bg.astype(BF16)))
        bt.append(jnp.transpose(bg).astype(BF16))
    ss = jnp.zeros((CHUNK, 1), F32)
    for p in range(H_D // 2):
        g = (2 * p) // (H_D // G_D)
        sl = slice(128 * p, 128 * (p + 1))
        le, lo_ = DT_LANE + 2 * p, DT_LANE + 2 * p + 1
        dt_pair = jnp.where(lo, _lane_col(dt, le), _lane_col(dt, lo_))
        cum_pair = jnp.where(lo, _lane_col(cum, le), _lane_col(cum, lo_))
        clast = cum_pair[CHUNK - 1:CHUNK, :]
        xs_p = xs[:, sl]
        xdt = xs_p * dt_pair
        s_old = s_scr[p]
        cgb = bcs[:, 256 + 128 * g:256 + 128 * (g + 1)].astype(BF16)
        y = _dot(cgb, s_old.astype(BF16)) * jnp.exp(cum_pair)
        for e in range(2):
            ln = DT_LANE + 2 * p + e
            diff = _lane_col(cum, ln) - cum_t[ln:ln + 1, :]
            att = (att_base[g] * jnp.exp(jnp.where(causal, diff, NEG))).astype(BF16)
            xm = jnp.where(lo if e == 0 else ~lo, xdt, 0.0).astype(BF16)
            y = y + _dot(att, xm)
        y = (y + xs_p * dskip_ref[:, sl]) * _silu(z_ref[:, sl])
        y_scr[:, sl] = y
        ss = ss + jnp.sum(y * y, axis=-1, keepdims=True)
        txdt = (xdt * jnp.exp(clast - cum_pair)).astype(BF16)
        s_scr[p] = jnp.exp(clast) * s_old + _dot(bt[g], txdt)
    inv = lax.rsqrt(ss * (1.0 / DI_D) + EPS)
    od_ref[...] = (y_scr[...] * inv * nw_ref[...]).astype(BF16)


def _ssd_prompt(p_main, p_small, cwx, cwbc, cbx, cbbc, dtb, alog, dskip, nw, *, nb, nc):
    seq = nc * CHUNK
    ng = nb // MIX_NBAT
    pg, psg = _batch_groups(p_main, nb, seq), _batch_groups(p_small, nb, seq)
    blk = lambda w, col: pl.BlockSpec((None, MIX_NBAT, CHUNK, w), lambda b, c: (b, 0, c, col))
    c2 = lambda b, c: (0, 0)
    od, st, cv = pl.pallas_call(
        _ssd_prompt_kernel,
        grid=(ng, nc),
        in_specs=[
            blk(1024, 4), blk(512, 10), blk(1024, 3), blk(ODD_SMALL, 0),
            pl.BlockSpec((CONV_W, DI_D), c2), pl.BlockSpec((CONV_W, 512), c2),
            pl.BlockSpec((1, DI_D), c2), pl.BlockSpec((1, 512), c2),
            pl.BlockSpec((1, ODD_SMALL), c2), pl.BlockSpec((1, ODD_SMALL), c2),
            pl.BlockSpec((1, DI_D), c2), pl.BlockSpec((1, DI_D), c2),
        ],
        out_specs=[
            blk(DI_D, 0),
            pl.BlockSpec((None, MIX_NBAT, 8, 128, 128), lambda b, c: (b, 0, 0, 0, 0)),
            pl.BlockSpec((None, MIX_NBAT, CONV_W - 1, CONV_CH), lambda b, c: (b, 0, 0, 0)),
        ],
        out_shape=[jax.ShapeDtypeStruct((ng, MIX_NBAT, seq, DI_D), BF16),
                   jax.ShapeDtypeStruct((ng, MIX_NBAT, 8, 128, 128), F32),
                   jax.ShapeDtypeStruct((ng, MIX_NBAT, CONV_W - 1, CONV_CH), F32)],
        scratch_shapes=[pltpu.VMEM((MIX_NBAT, CHUNK + 8, DI_D), F32), pltpu.VMEM((MIX_NBAT, CHUNK + 8, 512), F32),
                        pltpu.VMEM((MIX_NBAT, 8, 128, 128), F32), pltpu.VMEM((MIX_NBAT, CHUNK, DI_D), F32)],
        compiler_params=_cparams(("arbitrary", "arbitrary")),
        name="ssd_prompt",
    )(pg, pg, pg, psg, cwx, cwbc, cbx, cbbc, dtb, alog, dskip, nw)
    return od.reshape(nb * seq, DI_D), st.reshape(nb, 8, 128, 128), cv.reshape(nb, CONV_W - 1, CONV_CH)


def _ssd_sample_kernel(p_all, dt_all, cs_all, s0_all, cwx_ref, cwbc_ref, cbx_ref, cbbc_ref, dtb_ref, alog_ref,
                       dskip_ref, nw_ref, od_all, st_all, cv_all):
    for u in range(SAMPLE_NSEQ):
        _ssd_sample_one(p_all.at[u], dt_all.at[u], cs_all.at[u], s0_all.at[u], cwx_ref, cwbc_ref, cbx_ref,
                        cbbc_ref, dtb_ref, alog_ref, dskip_ref, nw_ref, od_all.at[u], st_all.at[u], cv_all.at[u])


def _ssd_sample_one(p_ref, dt_ref, cs_ref, s0_ref, cwx_ref, cwbc_ref, cbx_ref, cbbc_ref, dtb_ref, alog_ref,
                    dskip_ref, nw_ref, od_ref, st_ref, cv_ref):
    row = p_ref[...]
    xn = row[:, 4096:5120]
    bcn = row[:, 5120:5632]
    cx = cbx_ref[...] + cwx_ref[3:4, :] * xn
    cbc = cbbc_ref[...] + cwbc_ref[3:4, :] * bcn
    for w in range(CONV_W - 1):
        cx = cx + cwx_ref[w:w + 1, :] * cs_ref[w:w + 1, 0:DI_D]
        cbc = cbc + cwbc_ref[w:w + 1, :] * cs_ref[w:w + 1, DI_D:CONV_CH]
    cv_ref[0:1, :] = cs_ref[1:2, :]
    cv_ref[1:2, :] = cs_ref[2:3, :]
    cv_ref[2:3, 0:DI_D] = xn
    cv_ref[2:3, DI_D:CONV_CH] = bcn
    xs = _silu(cx)
    bcs = _silu(cbc)
    dt = _softplus(dt_ref[...] + dtb_ref[...])
    da = jnp.exp(dt * (-jnp.exp(alog_ref[...])))
    lane = lax.broadcasted_iota(jnp.int32, (1, 128), 1)
    lo = lane < 64
    r_i = lax.broadcasted_iota(jnp.int32, (128, 128), 0)
    c_i = lax.broadcasted_iota(jnp.int32, (128, 128), 1)
    top = r_i < 64
    n_pairs = H_D // 2
    r8 = lax.broadcasted_iota(jnp.int32, (8, 128), 0)
    xd8 = jnp.zeros((8, 128), F32)
    for p in range(n_pairs):
        le, lo_ = DT_LANE + 2 * p, DT_LANE + 2 * p + 1
        dt_pair = jnp.where(lo, _lane_col(dt, le), _lane_col(dt, lo_))
        xd8 = jnp.where(r8 == p, xs[:, 128 * p:128 * (p + 1)] * dt_pair, xd8)
    xd_t = jnp.transpose(jnp.concatenate([xd8, jnp.zeros((120, 128), F32)], axis=0))
    ycols = jnp.zeros((128, 128), F32)
    for p in range(n_pairs):
        g = (2 * p) // (H_D // G_D)
        le, lo_ = DT_LANE + 2 * p, DT_LANE + 2 * p + 1
        da_rows = jnp.where(top, _lane_col(da, le), _lane_col(da, lo_))
        s_new = da_rows * s0_ref[p] + _lane_col(xd_t, p) * bcs[:, 128 * g:128 * (g + 1)]
        st_ref[p] = s_new
        ycol = jnp.sum(s_new * bcs[:, 256 + 128 * g:256 + 128 * (g + 1)], axis=-1, keepdims=True)
        ycols = jnp.where(c_i == p, ycol, ycols)
    y_rows = jnp.transpose(ycols)
    ys = []
    ss = jnp.zeros((1, 1), F32)
    for p in range(n_pairs):
        sl = slice(128 * p, 128 * (p + 1))
        y = (y_rows[p:p + 1, :] + xs[:, sl] * dskip_ref[:, sl]) * _silu(row[:, 3072 + 128 * p:3072 + 128 * (p + 1)])
        ys.append(y)
        ss = ss + jnp.sum(y * y, axis=-1, keepdims=True)
    inv = lax.rsqrt(ss * (1.0 / DI_D) + EPS)
    for p in range(H_D // 2):
        sl = slice(128 * p, 128 * (p + 1))
        od_ref[:, sl] = ys[p] * inv * nw_ref[:, sl]


def _ssd_sample(p_main, p_small, cs, s0, cwx, cwbc, cbx, cbbc, dtb, alog, dskip, nw):
    nb = p_main.shape[0]
    b3 = lambda b: (b, 0, 0)
    b4 = lambda b: (b, 0, 0, 0)
    c2 = lambda b: (0, 0)
    ns = SAMPLE_NSEQ
    return pl.pallas_call(
        _ssd_sample_kernel,
        grid=(nb // ns,),
        in_specs=[
            pl.BlockSpec((ns, 1, ODD_MAIN), b3),
            pl.BlockSpec((ns, 1, ODD_SMALL), b3),
            pl.BlockSpec((ns, CONV_W - 1, CONV_CH), b3),
            pl.BlockSpec((ns, 8, 128, 128), b4),
            pl.BlockSpec((CONV_W, DI_D), c2), pl.BlockSpec((CONV_W, 512), c2),
            pl.BlockSpec((1, DI_D), c2), pl.BlockSpec((1, 512), c2),
            pl.BlockSpec((1, ODD_SMALL), c2), pl.BlockSpec((1, ODD_SMALL), c2),
            pl.BlockSpec((1, DI_D), c2), pl.BlockSpec((1, DI_D), c2),
        ],
        out_specs=[
            pl.BlockSpec((ns, 1, DI_D), b3),
            pl.BlockSpec((ns, 8, 128, 128), b4),
            pl.BlockSpec((ns, CONV_W - 1, CONV_CH), b3),
        ],
        out_shape=[jax.ShapeDtypeStruct((nb, 1, DI_D), F32), jax.ShapeDtypeStruct((nb, 8, 128, 128), F32),
                   jax.ShapeDtypeStruct((nb, CONV_W - 1, CONV_CH), F32)],
        compiler_params=_cparams(("arbitrary",)),
        name="ssd_sample",
    )(p_main.reshape(nb, 1, ODD_MAIN), p_small.reshape(nb, 1, ODD_SMALL), cs, s0,
      cwx, cwbc, cbx, cbbc, dtb, alog, dskip, nw)


def _pairs_to_heads(s):
    b, p, n, _ = s.shape
    return s.reshape(b, p, n, 2, 64).transpose(0, 1, 3, 2, 4).reshape(b, 2 * p, n, 64)


def _pad_lanes(v, start, width=ODD_SMALL):
    out = jnp.zeros((1, width), F32)
    return lax.dynamic_update_slice(out, v.reshape(1, -1).astype(F32), (0, start))


def kernel(x_prompt, x_sample, state_ret, cache_swa_k, cache_swa_v, cache_fox_k, cache_fox_v, cache_fox_logf,
           state_ssm, state_conv, page_table, norm_mix_pre, norm_mix_post, norm_mlp_pre, norm_mlp_post,
           w_in_even, w_out_even, ret_norm_w, swa_sinks, w_in_odd, w_out_odd, fox_fb, conv_w, conv_b,
           dt_bias, a_log, d_skip, ssd_norm_w, w_up, w_down):
    nb, seq = BATCH, SEQ
    nc = seq // CHUNK
    mp = nb * seq
    ms = DEC_BATCH
    xp = x_prompt.reshape(mp, D_MODEL)
    xs = x_sample.reshape(ms, D_MODEL)
    row = lambda v: v.reshape(1, -1)

    pos_p = jnp.arange(seq, dtype=jnp.int32)
    pos_s = jnp.full((1,), PAST_LEN, dtype=jnp.int32)
    ca_p, sa_p = _rope_tables(pos_p, RET_THETA, 8)
    cb_p, sb_p = _rope_tables(pos_p, ROPE_THETA_B, 2)
    ca_s, sa_s = _rope_tables(pos_s, RET_THETA, 8)
    cb_s, sb_s = _rope_tables(pos_s, ROPE_THETA_B, 2)

    we = w_in_even[0]
    we_main = we[:, :EVEN_MAIN].astype(BF16)
    we_small = we[:, EVEN_MAIN:].astype(BF16)
    wo = w_out_even[0].astype(BF16)
    wo_a, wo_b = wo[:1024], wo[1024:]
    g_pre, g_post = row(norm_mix_pre[0]), row(norm_mix_post[0])
    gm_pre, gm_post = row(norm_mlp_pre[0]), row(norm_mlp_post[0])
    gn_w = row(ret_norm_w[0])
    sinks = swa_sinks[0]

    pm, psm, sm, ssm_, wu_b, wd_b = _proj(
        xp, xs, g_pre, (we_main,), we_small, [(w_up, 0, D_MODEL, D_FF), (w_down, 0, D_FF, D_MODEL)],
        odd=False, tm=256, tn=512, seq=seq)
    out_a, ret_p = _ret_prompt(pm, ca_p, sa_p, gn_w, nb=nb, nc=nc)
    out_b, swak_p, swav_p = _swa_prompt(pm, psm, cb_p, sb_p, sinks, nb=nb, nc=nc)
    mix_s, ret_s, swak_s, swav_s = _even_sample(
        sm, ssm_, state_ret[0].reshape(ms, 4, 128, 128), cache_swa_k[0].reshape(ms, WINDOW, 128),
        cache_swa_v[0].reshape(ms, WINDOW, 128), ca_s, sa_s, cb_s, sb_s, gn_w, sinks)
    mix_s = mix_s.reshape(ms, 2048).astype(BF16)
    xp, xs = _outproj(out_a, out_b, xp, mix_s[:, :1024], mix_s[:, 1024:], xs, wo_a, wo_b, g_post, tm=512)
    xp, xs = _mlp(xp, xs, gm_pre, gm_post, wu_b, wd_b, tm=512, tf=1024)

    wod = w_in_odd[0].astype(BF16)
    wod_qkv = wod[:, :3072]
    wod_zx = wod[:, 3080:5640]
    wod_small = jnp.concatenate(
        [wod[:, 3072:3080], wod[:, 5640:5656], jnp.zeros((D_MODEL, ODD_SMALL - 24), BF16)], axis=1)
    wo1 = w_out_odd[0].astype(BF16)
    wo_c, wo_d = wo1[:1024], wo1[1024:]
    g_pre, g_post = row(norm_mix_pre[1]), row(norm_mix_post[1])
    gm_pre, gm_post = row(norm_mlp_pre[1]), row(norm_mlp_post[1])
    fb = _pad_lanes(fox_fb[0], 0)
    dtb = _pad_lanes(dt_bias[0], DT_LANE)
    alog = _pad_lanes(a_log[0], DT_LANE)
    cw = conv_w[0]
    cwx, cwbc = cw[:, :DI_D], cw[:, DI_D:]
    cbx, cbbc = row(conv_b[0][:DI_D]), row(conv_b[0][DI_D:])
    dskip = row(jnp.repeat(d_skip[0], HD_D))
    nw = row(ssd_norm_w[0])

    pm, psm, sm, ssm_, wu_b, wd_b, lf_p, fc_p, lf_s, k3_p, v3_p = _proj(
        xp, xs, g_pre, (wod_qkv, wod_zx), wod_small, [(w_up, 1, D_MODEL, D_FF), (w_down, 1, D_FF, D_MODEL)],
        fb, odd=True, tm=256, tn=512, seq=seq)
    fc = fc_p[:, :H_C].reshape(nb, seq, H_C).transpose(0, 2, 1)
    out_d, ssm_pairs_p, conv_p = _ssd_prompt(pm, psm, cwx, cwbc, cbx, cbbc, dtb, alog, dskip, nw, nb=nb, nc=nc)
    fox_k_p = k3_p.reshape(1, nb, seq, H_C, HD_C)
    fox_v_p = v3_p.reshape(1, nb, seq, H_C, HD_C)
    fox_lf_p = lf_p[:, :H_C].reshape(1, nb, seq, H_C)

    q_s = sm[:, 0:1024].reshape(ms, H_C, HD_C)
    k_s = sm[:, 1024:2048].reshape(ms, H_C, HD_C)
    v_s = sm[:, 2048:3072].reshape(ms, H_C, HD_C)
    lf_s8 = lf_s[:, :H_C]
    out_c_s, out_c = _fox(page_table, q_s, k_s, v_s, lf_s8.reshape(ms, H_C, 1),
                          cache_fox_k[0], cache_fox_v[0], cache_fox_logf[0].transpose(0, 2, 1),
                          pm, fc[:, :, None, :], pp=16, nb_p=nb, seq=seq)
    ssm_t = state_ssm[0].transpose(0, 1, 3, 2).reshape(ms, H_D // 2, 2 * HD_D, N_D)
    out_d_s, ssm_ts, conv_s = _ssd_sample(sm, ssm_, state_conv[0], ssm_t,
                                               cwx, cwbc, cbx, cbbc, dtb, alog, dskip, nw)
    xp, xs = _outproj(out_c, out_d, xp, out_c_s.reshape(ms, 1024).astype(BF16),
                      out_d_s.reshape(ms, DI_D).astype(BF16), xs, wo_c, wo_d, g_post, tm=512)
    xp, xs = _mlp(xp, xs, gm_pre, gm_post, wu_b, wd_b, tm=512, tf=1024)

    return (
        xp.reshape(nb, seq, D_MODEL), xs.reshape(ms, 1, D_MODEL),
        ret_p.reshape(1, nb, H_A, DK_A, DV_A), ret_s.reshape(1, ms, H_A, DK_A, DV_A),
        swak_p.reshape(1, nb, WINDOW, KV_B, HD_B), swav_p.reshape(1, nb, WINDOW, KV_B, HD_B),
        swak_s.reshape(1, ms, WINDOW, KV_B, HD_B), swav_s.reshape(1, ms, WINDOW, KV_B, HD_B),
        fox_k_p, fox_v_p, fox_lf_p,
        k_s.reshape(1, ms, 1, H_C, HD_C), v_s.reshape(1, ms, 1, H_C, HD_C), lf_s8.reshape(1, ms, 1, H_C),
        _pairs_to_heads(ssm_pairs_p)[None], ssm_ts.reshape(ms, H_D, HD_D, N_D).transpose(0, 1, 3, 2)[None],
        conv_p[None], conv_s[None],
    )
```
